```python
import math
import jax, jax.numpy as jnp
from jax import lax
import numpy as np

D_MODEL = 1024
BATCH = 16
SEQ = 2048
DEPTH = 2

HEAD_DIM = 64
DIFF_HEADS = 4
DIFF_V_DIM = 2 * HEAD_DIM
DSA_HEADS = 4
DSA_LATENT = 128
DSA_V_DIM = 64
IDX_HEADS = 8
IDX_DIM = 32
DSA_TOPK_MAX = 256
MOBA_HEADS = 4
MOBA_BLOCK = 256
MOBA_TOPK_MAX = 3
N_BUCKETS = 32
MAX_DISTANCE = 128
N_BIAS_HEADS = DIFF_HEADS + DSA_HEADS + MOBA_HEADS
D_FF = 4 * D_MODEL
Q_BLOCK = 128
MOBA_Q_CHUNK = 32
EPS = 1e-6

A_Q = DIFF_HEADS * 2 * HEAD_DIM
A_K = DIFF_HEADS * 2 * HEAD_DIM
A_V = DIFF_HEADS * DIFF_V_DIM
B_Q = DSA_HEADS * DSA_LATENT
B_KV = DSA_LATENT
B_IQ = IDX_HEADS * IDX_DIM
B_IK = IDX_DIM
B_IW = IDX_HEADS
C_QKV = MOBA_HEADS * HEAD_DIM
G_COLS = 3 * D_MODEL
SPLIT_SIZES = (A_Q, A_K, A_V, B_Q, B_KV, B_IQ, B_IK, B_IW, C_QKV, C_QKV, C_QKV, G_COLS)
IN_COLS = A_Q + A_K + A_V + B_Q + B_KV + B_IQ + B_IK + B_IW + 3 * C_QKV + G_COLS

kernel_name = 'hybrid_gated_diff_dsa_moba_block'


def rmsnorm(x, g):
    xf = x.astype(jnp.float32)
    y = xf * lax.rsqrt(jnp.mean(xf * xf, axis=-1, keepdims=True) + EPS)
    return y.astype(x.dtype) * g


def split_cols(p):
    out = []
    off = 0
    for n in SPLIT_SIZES:
        out.append(p[..., off:off + n])
        off += n
    return out


def t5_bucket(dist):
    max_exact = N_BUCKETS // 2
    n = jnp.maximum(dist, 0)
    nf = jnp.maximum(n, 1).astype(jnp.float32)
    large = max_exact + (jnp.log(nf / max_exact) / math.log(MAX_DISTANCE / max_exact)
                         * (N_BUCKETS - max_exact)).astype(jnp.int32)
    large = jnp.minimum(large, N_BUCKETS - 1)
    return jnp.where(n < max_exact, n, large)


def diff_attention(q, k, v, lam, lam_init, subln_g, bias_a):
    B, S, H = q.shape[0], q.shape[1], q.shape[2]
    E = v.shape[-1]
    kpos = jnp.arange(S)
    scale = HEAD_DIM ** -0.5
    bias_t = bias_a.T

    def block(i):
        q0 = i * Q_BLOCK
        qb = lax.dynamic_slice_in_dim(q, q0, Q_BLOCK, axis=1)
        qpos = q0 + jnp.arange(Q_BLOCK)
        dist = qpos[:, None] - kpos[None, :]
        bias = bias_t[:, t5_bucket(dist)]
        logits = jnp.einsum('bqhmd,bkhmd->bhmqk', qb, k).astype(jnp.float32) * scale + bias[None, :, None]
        logits = jnp.where(dist >= 0, logits, -jnp.inf)
        p = jax.nn.softmax(logits, axis=-1)
        attn = p[:, :, 0] - lam * p[:, :, 1]
        return jnp.einsum('bhqk,bkhe->bqhe', attn.astype(v.dtype), v)

    o = lax.map(block, jnp.arange(S // Q_BLOCK))
    o = jnp.moveaxis(o, 0, 1).reshape(B, S, H, E)
    o = rmsnorm(o, subln_g) * (1.0 - lam_init)
    return o.reshape(B, S, H * E)


def dsa_attention(q_lat, kv_lat, iq, ik, iw, w_uv, bias_b):
    B, S, H, R = q_lat.shape
    topk = min(DSA_TOPK_MAX, S // 4)
    kpos = jnp.arange(S)
    bidx = jnp.arange(B)[:, None, None]
    hidx = jnp.arange(H)[None, None, :, None]
    bias_t = bias_b.T
    idx_scale = (IDX_HEADS ** -0.5) * (IDX_DIM ** -0.5)

    def block(i):
        q0 = i * Q_BLOCK
        qpos = q0 + jnp.arange(Q_BLOCK)
        qb = lax.dynamic_slice_in_dim(q_lat, q0, Q_BLOCK, axis=1)
        iqb = lax.dynamic_slice_in_dim(iq, q0, Q_BLOCK, axis=1)
        iwb = lax.dynamic_slice_in_dim(iw, q0, Q_BLOCK, axis=1)
        causal = qpos[:, None] >= kpos[None, :]
        s_idx = jax.nn.relu(jnp.einsum('bqhd,bkd->bqhk', iqb, ik))
        score = jnp.einsum('bqhk,bqh->bqk', s_idx, iwb).astype(jnp.float32) * idx_scale
        score = jnp.where(causal[None], score, -jnp.inf)
        _, sel = lax.top_k(score, topk)
        dist = qpos[None, :, None] - sel
        valid = dist >= 0
        kv_sel = kv_lat[bidx, sel]
        logits = jnp.einsum('bqhr,bqkr->bqhk', qb, kv_sel).astype(jnp.float32) * (R ** -0.5)
        bias = bias_t[hidx, t5_bucket(dist)[:, :, None, :]]
        logits = jnp.where(valid[:, :, None, :], logits + bias, -jnp.inf)
        p = jax.nn.softmax(logits, axis=-1)
        return jnp.einsum('bqhk,bqkr->bqhr', p.astype(kv_sel.dtype), kv_sel)

    o = lax.map(block, jnp.arange(S // Q_BLOCK))
    o = jnp.moveaxis(o, 0, 1).reshape(B, S, H, R)
    o = jnp.einsum('bshr,hre->bshe', o, w_uv)
    return o.reshape(B, S, H * DSA_V_DIM)


def moba_attention(q, k, v, bias_c):
    B, S, H, Dh = q.shape
    nb = -(-S // MOBA_BLOCK)
    sp = nb * MOBA_BLOCK
    pad = ((0, 0), (0, sp - S), (0, 0), (0, 0))
    k_bh = jnp.pad(k, pad).reshape(B, nb, MOBA_BLOCK, H, Dh).transpose(0, 3, 1, 2, 4)
    v_bh = jnp.pad(v, pad).reshape(B, nb, MOBA_BLOCK, H, Dh).transpose(0, 3, 1, 2, 4)
    k_mean = jnp.mean(k_bh, axis=3)
    topb = min(MOBA_TOPK_MAX, nb)
    bias_t = bias_c.T
    bi = jnp.arange(B)[:, None, None, None]
    hi = jnp.arange(H)[None, :, None, None]
    hi5 = jnp.arange(H)[None, :, None, None, None]
    blk_ids = jnp.arange(nb)
    in_blk = jnp.arange(MOBA_BLOCK)
    scale = Dh ** -0.5

    def chunk(i):
        q0 = i * MOBA_Q_CHUNK
        qpos = q0 + jnp.arange(MOBA_Q_CHUNK)
        j = q0 // MOBA_BLOCK
        qc = lax.dynamic_slice_in_dim(q, q0, MOBA_Q_CHUNK, axis=1).transpose(0, 2, 1, 3)
        gate = jnp.einsum('bhqd,bhnd->bhqn', qc, k_mean).astype(jnp.float32)
        gate = jnp.where(blk_ids < j, gate, -jnp.inf)
        _, sel = lax.top_k(gate, topb)
        sel_valid = sel < j
        k_sel = k_bh[bi, hi, sel]
        v_sel = v_bh[bi, hi, sel]
        lp = jnp.einsum('bhqd,bhqnkd->bhqnk', qc, k_sel).astype(jnp.float32) * scale
        dist_p = qpos[None, None, :, None, None] - (sel[..., None] * MOBA_BLOCK + in_blk)
        lp = lp + bias_t[hi5, t5_bucket(dist_p)]
        lp = jnp.where(sel_valid[..., None], lp, -jnp.inf).reshape(B, H, MOBA_Q_CHUNK, topb * MOBA_BLOCK)
        k_own = lax.dynamic_slice_in_dim(k_bh, j, 1, axis=2)[:, :, 0]
        v_own = lax.dynamic_slice_in_dim(v_bh, j, 1, axis=2)[:, :, 0]
        lo = jnp.einsum('bhqd,bhkd->bhqk', qc, k_own).astype(jnp.float32) * scale
        dist_o = qpos[:, None] - (j * MOBA_BLOCK + in_blk)[None, :]
        lo = jnp.where(dist_o >= 0, lo + bias_t[:, t5_bucket(dist_o)][None], -jnp.inf)
        p = jax.nn.softmax(jnp.concatenate([lp, lo], axis=-1), axis=-1)
        pp = p[..., :topb * MOBA_BLOCK].reshape(B, H, MOBA_Q_CHUNK, topb, MOBA_BLOCK)
        po = p[..., topb * MOBA_BLOCK:]
        o = (jnp.einsum('bhqnk,bhqnkd->bhqd', pp.astype(v.dtype), v_sel)
             + jnp.einsum('bhqk,bhkd->bhqd', po.astype(v.dtype), v_own))
        return o.transpose(0, 2, 1, 3)

    o = lax.map(chunk, jnp.arange(S // MOBA_Q_CHUNK))
    return jnp.moveaxis(o, 0, 1).reshape(B, S, H * Dh)


def setup_inputs(seed: int = 0) -> dict:
    key = jax.random.key(seed)
    ks = jax.random.split(key, 20)

    def nrm(k, shape, s):
        return jax.random.normal(k, shape, jnp.float32) * s

    return {
        'x': nrm(ks[0], (BATCH, SEQ, D_MODEL), 1.0),
        'c': nrm(ks[1], (BATCH, D_MODEL), 1.0),
        'rel_bias': nrm(ks[2], (N_BUCKETS, N_BIAS_HEADS), 0.5),
        'ada_w': nrm(ks[3], (DEPTH, D_MODEL, 6 * D_MODEL), D_MODEL ** -0.5),
        'ada_b': nrm(ks[4], (DEPTH, 6 * D_MODEL), 0.02),
        'norm_mix': 1.0 + nrm(ks[5], (DEPTH, D_MODEL), 0.02),
        'w_in': nrm(ks[6], (DEPTH, D_MODEL, IN_COLS), D_MODEL ** -0.5),
        'gate_b': nrm(ks[7], (DEPTH, G_COLS), 0.02),
        'diff_lambda': nrm(ks[8], (DEPTH, 4, HEAD_DIM), 0.1),
        'diff_subln': 1.0 + nrm(ks[9], (DEPTH, DIFF_V_DIM), 0.02),
        'dsa_kv_norm': 1.0 + nrm(ks[10], (DEPTH, DSA_LATENT), 0.02),
        'dsa_w_uv': nrm(ks[11], (DEPTH, DSA_HEADS, DSA_LATENT, DSA_V_DIM), DSA_LATENT ** -0.5),
        'w_br_a': nrm(ks[12], (DEPTH, A_V, D_MODEL), A_V ** -0.5),
        'w_br_b': nrm(ks[13], (DEPTH, DSA_HEADS * DSA_V_DIM, D_MODEL), (DSA_HEADS * DSA_V_DIM) ** -0.5),
        'w_br_c': nrm(ks[14], (DEPTH, C_QKV, D_MODEL), C_QKV ** -0.5),
        'w_o': nrm(ks[15], (DEPTH, D_MODEL, D_MODEL), D_MODEL ** -0.5),
        'norm_mlp': 1.0 + nrm(ks[16], (DEPTH, D_MODEL), 0.02),
        'w_ff1': nrm(ks[17], (DEPTH, D_MODEL, D_FF), D_MODEL ** -0.5),
        'w_ff2': nrm(ks[18], (DEPTH, D_FF, D_MODEL), D_FF ** -0.5),
        'norm_final': 1.0 + nrm(ks[19], (D_MODEL,), 0.02),
    }


def reference(x, c, rel_bias, ada_w, ada_b, norm_mix, w_in, gate_b, diff_lambda, diff_subln,
              dsa_kv_norm, dsa_w_uv, w_br_a, w_br_b, w_br_c, w_o, norm_mlp, w_ff1, w_ff2,
              norm_final):
    B, S, _ = x.shape
    bias_tab = rel_bias.astype(jnp.float32)
    bias_a = bias_tab[:, :DIFF_HEADS]
    bias_b = bias_tab[:, DIFF_HEADS:DIFF_HEADS + DSA_HEADS]
    bias_c = bias_tab[:, DIFF_HEADS + DSA_HEADS:]
    cond = jax.nn.silu(c)
    for l in range(DEPTH):
        mod = (cond @ ada_w[l] + ada_b[l]).reshape(B, 6, D_MODEL)[:, :, None, :]
        shift1, scale1, gate1 = mod[:, 0], mod[:, 1], mod[:, 2]
        shift2, scale2, gate2 = mod[:, 3], mod[:, 4], mod[:, 5]

        u = rmsnorm(x, norm_mix[l]) * (1.0 + scale1) + shift1
        aq, ak, av, bq, bkv, biq, bik, biw, cq, ck, cv, g = split_cols(u @ w_in[l])

        lam_init = 0.8 - 0.6 * math.exp(-0.3 * l)
        dl = diff_lambda[l].astype(jnp.float32)
        lam = jnp.exp(jnp.sum(dl[0] * dl[1])) - jnp.exp(jnp.sum(dl[2] * dl[3])) + lam_init
        y_a = diff_attention(aq.reshape(B, S, DIFF_HEADS, 2, HEAD_DIM),
                             ak.reshape(B, S, DIFF_HEADS, 2, HEAD_DIM),
                             av.reshape(B, S, DIFF_HEADS, DIFF_V_DIM),
                             lam, lam_init, diff_subln[l], bias_a) @ w_br_a[l]
        y_b = dsa_attention(bq.reshape(B, S, DSA_HEADS, DSA_LATENT),
                            rmsnorm(bkv, dsa_kv_norm[l]),
                            biq.reshape(B, S, IDX_HEADS, IDX_DIM), bik, biw,
                            dsa_w_uv[l], bias_b) @ w_br_b[l]
        y_c = moba_attention(cq.reshape(B, S, MOBA_HEADS, HEAD_DIM),
                             ck.reshape(B, S, MOBA_HEADS, HEAD_DIM),
                             cv.reshape(B, S, MOBA_HEADS, HEAD_DIM), bias_c) @ w_br_c[l]
        gates = jax.nn.sigmoid(g + gate_b[l]).reshape(B, S, 3, D_MODEL)
        merged = gates[:, :, 0] * y_a + gates[:, :, 1] * y_b + gates[:, :, 2] * y_c
        x = x + gate1 * (merged @ w_o[l])

        u2 = rmsnorm(x, norm_mlp[l]) * (1.0 + scale2) + shift2
        h = jnp.square(jax.nn.relu(u2 @ w_ff1[l]))
        x = x + gate2 * (h @ w_ff2[l])
    return rmsnorm(x, norm_final)
```

```python
import functools
import math

import jax
import jax.numpy as jnp
from jax import lax
from jax.experimental import pallas as pl
from jax.experimental.pallas import tpu as pltpu

F32 = jnp.float32
BF16 = jnp.bfloat16

D_MODEL = 1024
HEAD_DIM = 64
DIFF_HEADS = 4
DIFF_V_DIM = 2 * HEAD_DIM
DSA_HEADS = 4
DSA_LATENT = 128
DSA_V_DIM = 64
IDX_HEADS = 8
IDX_DIM = 32
DSA_TOPK_MAX = 256
MOBA_HEADS = 4
MOBA_BLOCK = 256
MOBA_TOPK_MAX = 3
N_BUCKETS = 32
MAX_DISTANCE = 128
D_FF = 4 * D_MODEL
EPS = 1e-6

LANES = 128
ATT_TILE = MOBA_BLOCK
NEG = -1e30
INT_MIN = -2 ** 31
VMEM_LIMIT = 52 * 1024 * 1024

_O_AQ, _O_AK, _O_AV, _O_BQ, _O_BKV, _O_BIQ, _O_BIK, _O_BIW, _O_CQ, _O_CK, _O_CV, _O_G = (
    0, 512, 1024, 1536, 2048, 2176, 2432, 2464, 2472, 2728, 2984, 3240)
IN_COLS = 6312

BLK_G, BLK_AQ, BLK_AK, BLK_AV, BLK_BQ, BLK_BIQ, BLK_BKV, BLK_IK, BLK_IW, BLK_CQ, BLK_CK, BLK_CV = (
    0, 24, 28, 32, 36, 40, 42, 43, 44, 45, 47, 49)
N_BLKS = 51
PACKED_COLS = N_BLKS * LANES


def _packed_columns():
    src, keep = [], []

    def seg(off, n):
        src.extend(range(off, off + n))
        keep.extend([1.0] * n)

    seg(_O_G, 3 * D_MODEL)
    seg(_O_AQ, 512)
    seg(_O_AK, 512)
    seg(_O_AV, 512)
    seg(_O_BQ, 512)
    seg(_O_BIQ, 256)
    seg(_O_BKV, 128)
    for _ in range(LANES // IDX_DIM):
        seg(_O_BIK, IDX_DIM)
    seg(_O_BIW, IDX_HEADS)
    src.extend([0] * (LANES - IDX_HEADS))
    keep.extend([0.0] * (LANES - IDX_HEADS))
    seg(_O_CQ, 256)
    seg(_O_CK, 256)
    seg(_O_CV, 256)
    assert len(src) == PACKED_COLS
    return src, keep


def _nt(a, b):
    return lax.dot_general(a, b, (((1,), (1,)), ((), ())), preferred_element_type=F32)


def _nn(a, b):
    return jnp.dot(a, b, preferred_element_type=F32)


def _eye(n):
    r = lax.broadcasted_iota(jnp.int32, (n, n), 0)
    c = lax.broadcasted_iota(jnp.int32, (n, n), 1)
    return jnp.where(r == c, 1.0, 0.0).astype(BF16)


def _lane_group_mask(x_bf16, group, width):
    lane = lax.broadcasted_iota(jnp.int32, x_bf16.shape, 1)
    keep = (lane >= group * width) & (lane < (group + 1) * width)
    return jnp.where(keep, x_bf16.astype(F32), 0.0)


def _ada_kernel(c_ref, w_ref, b_ref, o_ref):
    c = c_ref[...]
    cond = c * jax.nn.sigmoid(c)
    o_ref[0] = _nn(cond, w_ref[0]) + b_ref[0]


def _ada(c, ada_w, ada_b):
    depth, d, n = ada_w.shape
    bsz = c.shape[0]
    tn = 1536
    return pl.pallas_call(
        _ada_kernel,
        grid=(depth, n // tn),
        in_specs=[
            pl.BlockSpec((bsz, d), lambda l, j: (0, 0)),
            pl.BlockSpec((1, d, tn), lambda l, j: (l, 0, j)),
            pl.BlockSpec((1, 1, tn), lambda l, j: (l, 0, j)),
        ],
        out_specs=pl.BlockSpec((1, bsz, tn), lambda l, j: (l, 0, j)),
        out_shape=jax.ShapeDtypeStruct((depth, bsz, n), F32),
        compiler_params=pltpu.CompilerParams(
            dimension_semantics=("arbitrary", "arbitrary"), vmem_limit_bytes=VMEM_LIMIT),
        name="ada_mod",
    )(c, ada_w, ada_b.reshape(depth, 1, n))


def _norm_modulate(x, g, shift, scale):
    ms = jnp.mean(x * x, axis=-1, keepdims=True)
    y = x * lax.rsqrt(ms + EPS) * g
    return y * (1.0 + scale) + shift


def _inproj_kernel(x_ref, mod_ref, g_ref, w_ref, o_ref, u_ref):
    @pl.when(pl.program_id(1) == 0)
    def _():
        u = _norm_modulate(x_ref[...], g_ref[...], mod_ref[0, 0:1, :], mod_ref[0, 1:2, :])
        u_ref[...] = u.astype(BF16)

    o_ref[...] = _nn(u_ref[...], w_ref[...]).astype(o_ref.dtype)


def _inproj(x2, mod_l, g, w_packed, seq):
    m, d = x2.shape
    n = w_packed.shape[1]
    tm = min(1024, seq)
    tn = n // 3
    return pl.pallas_call(
        _inproj_kernel,
        grid=(m // tm, n // tn),
        in_specs=[
            pl.BlockSpec((tm, d), lambda i, j: (i, 0)),
            pl.BlockSpec((1, 6, d), lambda i, j: ((i * tm) // seq, 0, 0)),
            pl.BlockSpec((1, d), lambda i, j: (0, 0)),
            pl.BlockSpec((d, tn), lambda i, j: (0, j)),
        ],
        out_specs=pl.BlockSpec((tm, tn), lambda i, j: (i, j)),
        out_shape=jax.ShapeDtypeStruct((m, n), BF16),
        scratch_shapes=[pltpu.VMEM((tm, d), BF16)],
        compiler_params=pltpu.CompilerParams(
            dimension_semantics=("parallel", "arbitrary"), vmem_limit_bytes=VMEM_LIMIT),
        name="in_proj",
    )(x2, mod_l, g.reshape(1, d), w_packed)


def _softmax_step(s, vt, m_ref, l_ref, acc_ref, idx):
    m_old = m_ref[idx]
    m_new = jnp.maximum(m_old, jnp.max(s, axis=0, keepdims=True))
    alpha = jnp.exp(m_old - m_new)
    p = jnp.exp(s - m_new)
    l_ref[idx] = alpha * l_ref[idx] + jnp.sum(p, axis=0, keepdims=True)
    acc_ref[idx] = alpha * acc_ref[idx] + _nn(vt, p.astype(BF16))
    m_ref[idx] = m_new


def _init_softmax_state(m_ref, l_ref, acc_ref):
    m_ref[...] = jnp.full(m_ref.shape, NEG, F32)
    l_ref[...] = jnp.zeros(l_ref.shape, F32)
    acc_ref[...] = jnp.zeros(acc_ref.shape, F32)


def _transpose_tiles(src_ref, dst_ref, n_tiles, tile):
    eye = _eye(LANES)
    for j in range(n_tiles):
        dst_ref[j] = _nt(eye, src_ref[0, j * tile:(j + 1) * tile, :]).astype(BF16)


def _diff_kernel(q_ref, k_ref, v_ref, bias_ref, dl_ref, g_ref, o_ref,
                 vt_ref, m_ref, l_ref, acc_ref, *, lam_init, n_tiles):
    t = ATT_TILE
    i = pl.program_id(2)

    @pl.when(i == 0)
    def _():
        _transpose_tiles(v_ref, vt_ref, n_tiles, t)

    scale = HEAD_DIM ** -0.5
    q = q_ref[0]
    q_lo = (_lane_group_mask(q, 0, HEAD_DIM) * scale).astype(BF16)
    q_hi = (_lane_group_mask(q, 1, HEAD_DIM) * scale).astype(BF16)
    _init_softmax_state(m_ref, l_ref, acc_ref)

    def body(j, carry):
        kblk = k_ref[0, pl.ds(pl.multiple_of(j * t, t), t), :]
        bias = bias_ref[jnp.minimum(i - j, 2)]
        vt = vt_ref[j]
        _softmax_step(_nt(kblk, q_lo) + bias, vt, m_ref, l_ref, acc_ref, 0)
        _softmax_step(_nt(kblk, q_hi) + bias, vt, m_ref, l_ref, acc_ref, 1)
        return carry

    lax.fori_loop(0, i + 1, body, 0)

    dl = dl_ref[...]
    lam = (jnp.exp(jnp.sum(dl[0:1] * dl[1:2], keepdims=True))
           - jnp.exp(jnp.sum(dl[2:3] * dl[3:4], keepdims=True)) + lam_init)
    o = acc_ref[0] / l_ref[0] - lam * (acc_ref[1] / l_ref[1])
    ms = jnp.mean(o * o, axis=0, keepdims=True)
    o = o * lax.rsqrt(ms + EPS) * g_ref[...] * (1.0 - lam_init)
    o_ref[0] = _nt(_eye(t), o.astype(BF16)).astype(o_ref.dtype)


def _diff_attention(p3, bias_tiles, diff_lambda_l, subln_g, lam_init):
    bsz, seq, _ = p3.shape
    t = ATT_TILE
    n_tiles = seq // t
    kern = functools.partial(_diff_kernel, lam_init=lam_init, n_tiles=n_tiles)
    return pl.pallas_call(
        kern,
        grid=(bsz, DIFF_HEADS, n_tiles),
        in_specs=[
            pl.BlockSpec((1, t, LANES), lambda b, h, i: (b, i, BLK_AQ + h)),
            pl.BlockSpec((1, seq, LANES), lambda b, h, i: (b, 0, BLK_AK + h)),
            pl.BlockSpec((1, seq, LANES), lambda b, h, i: (b, 0, BLK_AV + h)),
            pl.BlockSpec((None, 3, t, t), lambda b, h, i: (h, 0, 0, 0)),
            pl.BlockSpec((4, HEAD_DIM), lambda b, h, i: (0, 0)),
            pl.BlockSpec((DIFF_V_DIM, 1), lambda b, h, i: (0, 0)),
        ],
        out_specs=pl.BlockSpec((1, t, LANES), lambda b, h, i: (b, i, h)),
        out_shape=jax.ShapeDtypeStruct((bsz, seq, DIFF_HEADS * DIFF_V_DIM), BF16),
        scratch_shapes=[
            pltpu.VMEM((n_tiles, LANES, t), BF16),
            pltpu.VMEM((2, 1, t), F32),
            pltpu.VMEM((2, 1, t), F32),
            pltpu.VMEM((2, DIFF_V_DIM, t), F32),
        ],
        compiler_params=pltpu.CompilerParams(
            dimension_semantics=("parallel", "parallel", "arbitrary"),
            vmem_limit_bytes=VMEM_LIMIT),
        name="diff_attn",
    )(p3, p3, p3, bias_tiles, diff_lambda_l, subln_g.reshape(DIFF_V_DIM, 1))


def _dsa_kernel(q_ref, kv_ref, iq_ref, ik_ref, iw_ref, bias_ref, g_ref, wuv_ref, o_ref,
                kvn_ref, kvt_ref, keys_ref, thr_ref, cut_ref, m_ref, l_ref, acc_ref,
                *, n_tiles, topk):
    t = ATT_TILE
    i = pl.program_id(1)
    n_live = i + 1

    @pl.when(i == 0)
    def _():
        kv = kv_ref[0].astype(F32)
        ms = jnp.mean(kv * kv, axis=-1, keepdims=True)
        kvn_ref[...] = (kv * lax.rsqrt(ms + EPS) * g_ref[...]).astype(BF16)
        eye = _eye(LANES)
        for j in range(n_tiles):
            kvt_ref[j] = _nt(eye, kvn_ref[j * t:(j + 1) * t, :]).astype(BF16)

    groups = LANES // IDX_DIM
    iq = iq_ref[0]
    iq_heads = [
        _lane_group_mask(iq[:, (h // groups) * LANES:(h // groups + 1) * LANES],
                         h % groups, IDX_DIM).astype(BF16)
        for h in range(IDX_HEADS)
    ]
    sel_rows = lax.broadcasted_iota(jnp.int32, (IDX_HEADS, LANES), 0)
    sel_cols = lax.broadcasted_iota(jnp.int32, (IDX_HEADS, LANES), 1)
    pick = jnp.where(sel_rows == sel_cols, 1.0, 0.0).astype(BF16)
    iw_t = _nt(pick, iw_ref[0])
    idx_scale = (IDX_HEADS ** -0.5) * (IDX_DIM ** -0.5)
    row = lax.broadcasted_iota(jnp.int32, (t, t), 0)
    col = lax.broadcasted_iota(jnp.int32, (t, t), 1)

    def score_body(j, carry):
        ik = ik_ref[0, pl.ds(pl.multiple_of(j * t, t), t), :]
        sc = jnp.zeros((t, t), F32)
        for h in range(IDX_HEADS):
            sc = sc + jnp.maximum(_nt(ik, iq_heads[h]), 0.0) * iw_t[h:h + 1, :]
        sc = sc * idx_scale
        bits = pltpu.bitcast(sc, jnp.int32)
        key = bits ^ ((bits >> 31) & 0x7FFFFFFF)
        valid = (i * t + col) >= (j * t + row)
        keys_ref[j] = jnp.where(valid, key, INT_MIN)
        return carry

    lax.fori_loop(0, n_live, score_body, 0)

    def count(pred):
        def cb(j, acc):
            hit = pred(keys_ref[j], j)
            return acc + jnp.sum(jnp.where(hit, 1.0, 0.0), axis=0, keepdims=True)
        return lax.fori_loop(0, n_live, cb, jnp.zeros((1, t), F32))

    def bit_body(it, thr):
        cand = thr ^ lax.shift_left(jnp.int32(1), 31 - it)
        cnt = count(lambda k, j: k >= cand)
        return jnp.where(cnt >= topk, cand, thr)

    thr = lax.fori_loop(0, 32, bit_body, jnp.full((1, t), INT_MIN, jnp.int32))
    n_gt = count(lambda k, j: k > thr)
    n_ge = count(lambda k, j: k >= thr)
    need = topk - n_gt
    has_rank = thr > INT_MIN
    excess = jnp.where(has_rank & (n_ge > topk), 1.0, 0.0)
    thr_ref[...] = thr
    cut_ref[...] = jnp.where(has_rank, jnp.int32(2 * n_tiles * t), jnp.int32(0))

    @pl.when(jnp.max(excess) > 0.0)
    def _():
        n_bits = (2 * n_tiles * t - 1).bit_length()

        def cut_body(it, cut):
            cand = cut | lax.shift_left(jnp.int32(1), n_bits - 1 - it)
            cnt = count(lambda k, j: (k == thr) & ((j * t + row) < cand))
            return jnp.where(cnt <= need, cand, cut)

        cut = lax.fori_loop(0, n_bits, cut_body, jnp.zeros((1, t), jnp.int32))
        cut_ref[...] = jnp.where(has_rank, cut, jnp.int32(0))

    _init_softmax_state(m_ref, l_ref, acc_ref)
    scale = DSA_LATENT ** -0.5

    def att_body(j, carry):
        kvb = kvn_ref[pl.ds(pl.multiple_of(j * t, t), t), :]
        kvt = kvt_ref[j]
        key = keys_ref[j]
        thr_v = thr_ref[...]
        sel = (key > thr_v) | ((key == thr_v) & ((j * t + row) < cut_ref[...]))
        tile_kind = jnp.minimum(i - j, 2)
        for h in range(DSA_HEADS):
            q_h = q_ref[0, :, h * LANES:(h + 1) * LANES]
            s = _nt(kvb, q_h) * scale + bias_ref[h, tile_kind]
            _softmax_step(jnp.where(sel, s, NEG), kvt, m_ref, l_ref, acc_ref, h)
        return carry

    lax.fori_loop(0, n_live, att_body, 0)

    y_t = jnp.zeros((DSA_HEADS * DSA_V_DIM, t), F32)
    for h in range(DSA_HEADS):
        o_h = (acc_ref[h] / l_ref[h]).astype(BF16)
        y_t = y_t + _nn(wuv_ref[h], o_h)
    o_ref[0] = _nt(_eye(t), y_t.astype(BF16)).astype(o_ref.dtype)


def _dsa_attention(p3, bias_tiles, kv_norm_g, wuv_t_pad):
    bsz, seq, _ = p3.shape
    t = ATT_TILE
    n_tiles = seq // t
    topk = min(DSA_TOPK_MAX, seq // 4)
    kern = functools.partial(_dsa_kernel, n_tiles=n_tiles, topk=float(topk))
    n_out = DSA_HEADS * DSA_V_DIM
    return pl.pallas_call(
        kern,
        grid=(bsz, n_tiles),
        in_specs=[
            pl.BlockSpec((1, t, 4 * LANES), lambda b, i: (b, i, BLK_BQ // 4)),
            pl.BlockSpec((1, seq, LANES), lambda b, i: (b, 0, BLK_BKV)),
            pl.BlockSpec((1, t, 2 * LANES), lambda b, i: (b, i, BLK_BIQ // 2)),
            pl.BlockSpec((1, seq, LANES), lambda b, i: (b, 0, BLK_IK)),
            pl.BlockSpec((1, t, LANES), lambda b, i: (b, i, BLK_IW)),
            pl.BlockSpec((DSA_HEADS, 3, t, t), lambda b, i: (1, 0, 0, 0)),
            pl.BlockSpec((1, DSA_LATENT), lambda b, i: (0, 0)),
            pl.BlockSpec((DSA_HEADS, n_out, DSA_LATENT), lambda b, i: (0, 0, 0)),
        ],
        out_specs=pl.BlockSpec((1, t, n_out), lambda b, i: (b, i, 0)),
        out_shape=jax.ShapeDtypeStruct((bsz, seq, n_out), BF16),
        scratch_shapes=[
            pltpu.VMEM((seq, DSA_LATENT), BF16),
            pltpu.VMEM((n_tiles, DSA_LATENT, t), BF16),
            pltpu.VMEM((n_tiles, t, t), jnp.int32),
            pltpu.VMEM((1, t), jnp.int32),
            pltpu.VMEM((1, t), jnp.int32),
            pltpu.VMEM((DSA_HEADS, 1, t), F32),
            pltpu.VMEM((DSA_HEADS, 1, t), F32),
            pltpu.VMEM((DSA_HEADS, DSA_LATENT, t), F32),
        ],
        compiler_params=pltpu.CompilerParams(
            dimension_semantics=("parallel", "arbitrary"), vmem_limit_bytes=VMEM_LIMIT),
        name="dsa_attn",
    )(p3, p3, p3, p3, p3, bias_tiles, kv_norm_g.reshape(1, DSA_LATENT), wuv_t_pad)


def _moba_kernel(q_ref, k_ref, v_ref, bias_ref, o_ref,
                 vt_ref, kmean_ref, selb_ref, m_ref, l_ref, acc_ref, *, n_tiles, topb):
    t = ATT_TILE
    i = pl.program_id(2)

    @pl.when(i == 0)
    def _():
        _transpose_tiles(v_ref, vt_ref, n_tiles, t)
        for n in range(n_tiles):
            kb = k_ref[0, n * t:(n + 1) * t, :].astype(F32)
            kmean_ref[n:n + 1, :] = jnp.mean(kb, axis=0, keepdims=True)

    scale = HEAD_DIM ** -0.5
    q = q_ref[0]
    kmean = kmean_ref[...].astype(BF16)
    blk = lax.broadcasted_iota(jnp.int32, (n_tiles, t), 0)
    own = jnp.full((1, t), i, jnp.int32)
    q_heads = []
    for hh in range(2):
        q_f = _lane_group_mask(q, hh, HEAD_DIM)
        q_heads.append((q_f * scale).astype(BF16))
        gate = _nt(kmean, q_f.astype(BF16))
        for n in range(n_tiles):
            gn = gate[n:n + 1, :]
            ahead = (gate > gn) | ((gate == gn) & (blk < n))
            ahead = ahead & (blk < i)
            rank = jnp.sum(jnp.where(ahead, 1.0, 0.0), axis=0, keepdims=True)
            chosen = ((rank < topb) & (own > n)) | (own == n)
            selb_ref[hh, n] = jnp.broadcast_to(jnp.where(chosen, 0.0, NEG), (8, t))
    _init_softmax_state(m_ref, l_ref, acc_ref)

    def body(j, carry):
        kblk = k_ref[0, pl.ds(pl.multiple_of(j * t, t), t), :]
        vt = vt_ref[j]
        tile_kind = jnp.minimum(i - j, 2)
        for hh in range(2):
            s = _nt(kblk, q_heads[hh]) + bias_ref[hh, tile_kind] + selb_ref[hh, j][0:1, :]
            _softmax_step(s, vt, m_ref, l_ref, acc_ref, hh)
        return carry

    lax.fori_loop(0, i + 1, body, 0)

    o_lo = acc_ref[0] / l_ref[0]
    o_hi = acc_ref[1] / l_ref[1]
    rows = lax.broadcasted_iota(jnp.int32, (LANES, t), 0)
    o = jnp.where(rows < HEAD_DIM, o_lo, o_hi)
    o_ref[0] = _nt(_eye(t), o.astype(BF16)).astype(o_ref.dtype)


def _moba_attention(p3, bias_tiles):
    bsz, seq, _ = p3.shape
    t = ATT_TILE
    n_tiles = seq // t
    topb = min(MOBA_TOPK_MAX, n_tiles)
    kern = functools.partial(_moba_kernel, n_tiles=n_tiles, topb=float(topb))
    pairs = MOBA_HEADS // 2
    first_pair = (DIFF_HEADS + DSA_HEADS) // 2
    return pl.pallas_call(
        kern,
        grid=(bsz, pairs, n_tiles),
        in_specs=[
            pl.BlockSpec((1, t, LANES), lambda b, g, i: (b, i, BLK_CQ + g)),
            pl.BlockSpec((1, seq, LANES), lambda b, g, i: (b, 0, BLK_CK + g)),
            pl.BlockSpec((1, seq, LANES), lambda b, g, i: (b, 0, BLK_CV + g)),
            pl.BlockSpec((2, 3, t, t), lambda b, g, i: (first_pair + g, 0, 0, 0)),
        ],
        out_specs=pl.BlockSpec((1, t, LANES), lambda b, g, i: (b, i, g)),
        out_shape=jax.ShapeDtypeStruct((bsz, seq, MOBA_HEADS * HEAD_DIM), BF16),
        scratch_shapes=[
            pltpu.VMEM((n_tiles, LANES, t), BF16),
            pltpu.VMEM((n_tiles, LANES), F32),
            pltpu.VMEM((2, n_tiles, 8, t), F32),
            pltpu.VMEM((2, 1, t), F32),
            pltpu.VMEM((2, 1, t), F32),
            pltpu.VMEM((2, LANES, t), F32),
        ],
        compiler_params=pltpu.CompilerParams(
            dimension_semantics=("parallel", "parallel", "arbitrary"),
            vmem_limit_bytes=VMEM_LIMIT),
        name="moba_attn",
    )(p3, p3, p3, bias_tiles)


def _merge_kernel(x_ref, g_ref, gb_ref, oa_ref, ob_ref, oc_ref, wa_ref, wb_ref, wc_ref, wo_ref,
                  mod_ref, o_ref):
    d = D_MODEL
    gates = jax.nn.sigmoid(g_ref[...].astype(F32) + gb_ref[...])
    merged = (gates[:, 0:d] * _nn(oa_ref[...], wa_ref[...])
              + gates[:, d:2 * d] * _nn(ob_ref[...], wb_ref[...])
              + gates[:, 2 * d:3 * d] * _nn(oc_ref[...], wc_ref[...]))
    z = _nn(merged.astype(BF16), wo_ref[...])
    o_ref[...] = x_ref[...] + mod_ref[0, 2:3, :] * z


def _merge(x2, p2, gate_b, oa, ob, oc, wa, wb, wc, wo, mod_l, seq):
    m, d = x2.shape
    tm = min(512, seq)
    full = lambda a: pl.BlockSpec(a.shape, lambda i: (0, 0))
    return pl.pallas_call(
        _merge_kernel,
        grid=(m // tm,),
        in_specs=[
            pl.BlockSpec((tm, d), lambda i: (i, 0)),
            pl.BlockSpec((tm, 3 * d), lambda i: (i, BLK_G)),
            pl.BlockSpec((1, 3 * d), lambda i: (0, 0)),
            pl.BlockSpec((tm, oa.shape[1]), lambda i: (i, 0)),
            pl.BlockSpec((tm, ob.shape[1]), lambda i: (i, 0)),
            pl.BlockSpec((tm, oc.shape[1]), lambda i: (i, 0)),
            full(wa), full(wb), full(wc), full(wo),
            pl.BlockSpec((1, 6, d), lambda i: ((i * tm) // seq, 0, 0)),
        ],
        out_specs=pl.BlockSpec((tm, d), lambda i: (i, 0)),
        out_shape=jax.ShapeDtypeStruct((m, d), F32),
        compiler_params=pltpu.CompilerParams(
            dimension_semantics=("parallel",), vmem_limit_bytes=VMEM_LIMIT),
        name="merge",
    )(x2, p2, gate_b.reshape(1, 3 * d), oa, ob, oc, wa, wb, wc, wo, mod_l)


def _mlp_kernel(x_ref, mod_ref, g_ref, w1_ref, w2_ref, gf_ref, o_ref, u_ref, acc_ref, *, final):
    f = pl.program_id(1)

    @pl.when(f == 0)
    def _():
        u = _norm_modulate(x_ref[...], g_ref[...], mod_ref[0, 3:4, :], mod_ref[0, 4:5, :])
        u_ref[...] = u.astype(BF16)
        acc_ref[...] = jnp.zeros(acc_ref.shape, F32)

    h = jnp.square(jnp.maximum(_nn(u_ref[...], w1_ref[...]), 0.0))
    acc_ref[...] += _nn(h.astype(BF16), w2_ref[...])

    @pl.when(f == pl.num_programs(1) - 1)
    def _():
        y = x_ref[...] + mod_ref[0, 5:6, :] * acc_ref[...]
        if final:
            ms = jnp.mean(y * y, axis=-1, keepdims=True)
            y = y * lax.rsqrt(ms + EPS) * gf_ref[...]
        o_ref[...] = y


def _mlp(x2, mod_l, g, w1, w2, g_final, seq, final):
    m, d = x2.shape
    dff = w1.shape[1]
    tm = min(1024, seq)
    tf = 1024
    return pl.pallas_call(
        functools.partial(_mlp_kernel, final=final),
        grid=(m // tm, dff // tf),
        in_specs=[
            pl.BlockSpec((tm, d), lambda i, f: (i, 0)),
            pl.BlockSpec((1, 6, d), lambda i, f: ((i * tm) // seq, 0, 0)),
            pl.BlockSpec((1, d), lambda i, f: (0, 0)),
            pl.BlockSpec((d, tf), lambda i, f: (0, f)),
            pl.BlockSpec((tf, d), lambda i, f: (f, 0)),
            pl.BlockSpec((1, d), lambda i, f: (0, 0)),
        ],
        out_specs=pl.BlockSpec((tm, d), lambda i, f: (i, 0)),
        out_shape=jax.ShapeDtypeStruct((m, d), F32),
        scratch_shapes=[pltpu.VMEM((tm, d), BF16), pltpu.VMEM((tm, d), F32)],
        compiler_params=pltpu.CompilerParams(
            dimension_semantics=("parallel", "arbitrary"), vmem_limit_bytes=VMEM_LIMIT),
        name="mlp",
    )(x2, mod_l, g.reshape(1, d), w1, w2, g_final.reshape(1, d))


def _t5_bucket(dist):
    max_exact = N_BUCKETS // 2
    n = jnp.maximum(dist, 0)
    nf = jnp.maximum(n, 1).astype(F32)
    large = max_exact + (jnp.log(nf / max_exact) / math.log(MAX_DISTANCE / max_exact)
                         * (N_BUCKETS - max_exact)).astype(jnp.int32)
    large = jnp.minimum(large, N_BUCKETS - 1)
    return jnp.where(n < max_exact, n, large)


def _bias_tiles(rel_bias):
    t = ATT_TILE
    kk = jnp.arange(t)[:, None]
    qq = jnp.arange(t)[None, :]
    dist = jnp.stack([qq - kk, t + qq - kk, jnp.full((t, t), 2 * t)])
    tab = rel_bias.astype(F32)[_t5_bucket(dist)]
    tab = jnp.where((dist >= 0)[..., None], tab, NEG)
    return jnp.transpose(tab, (3, 0, 1, 2))


def kernel(x, c, rel_bias, ada_w, ada_b, norm_mix, w_in, gate_b, diff_lambda, diff_subln,
           dsa_kv_norm, dsa_w_uv, w_br_a, w_br_b, w_br_c, w_o, norm_mlp, w_ff1, w_ff2,
           norm_final):
    bsz, seq, d = x.shape
    depth = w_in.shape[0]
    assert seq % ATT_TILE == 0 and 2 * ATT_TILE > MAX_DISTANCE

    src, keep = _packed_columns()
    w_in_p = (jnp.take(w_in, jnp.asarray(src, jnp.int32), axis=2)
              * jnp.asarray(keep, F32)).astype(BF16)
    n_out = DSA_HEADS * DSA_V_DIM
    wuv_t = jnp.transpose(dsa_w_uv, (0, 1, 3, 2))
    wuv_t_pad = jnp.zeros((depth, DSA_HEADS, n_out, DSA_LATENT), F32)
    for h in range(DSA_HEADS):
        wuv_t_pad = wuv_t_pad.at[:, h, h * DSA_V_DIM:(h + 1) * DSA_V_DIM, :].set(wuv_t[:, h])
    wuv_t_pad = wuv_t_pad.astype(BF16)
    wa, wb, wc, wo = (w.astype(BF16) for w in (w_br_a, w_br_b, w_br_c, w_o))
    w1, w2 = w_ff1.astype(BF16), w_ff2.astype(BF16)

    bias_tiles = _bias_tiles(rel_bias)
    mod = _ada(c, ada_w, ada_b).reshape(depth, bsz, 6, d)

    x2 = x.reshape(bsz * seq, d)
    for l in range(depth):
        lam_init = 0.8 - 0.6 * math.exp(-0.3 * l)
        p2 = _inproj(x2, mod[l], norm_mix[l], w_in_p[l], seq)
        p3 = p2.reshape(bsz, seq, PACKED_COLS)
        oa = _diff_attention(p3, bias_tiles, diff_lambda[l], diff_subln[l], lam_init)
        ob = _dsa_attention(p3, bias_tiles, dsa_kv_norm[l], wuv_t_pad[l])
        oc = _moba_attention(p3, bias_tiles)
        x2 = _merge(x2, p2, gate_b[l],
                    oa.reshape(bsz * seq, -1), ob.reshape(bsz * seq, -1), oc.reshape(bsz * seq, -1),
                    wa[l], wb[l], wc[l], wo[l], mod[l], seq)
        x2 = _mlp(x2, mod[l], norm_mlp[l], w1[l], w2[l], norm_final, seq, final=(l == depth - 1))
    return x2.reshape(bsz, seq, d)
```

```python
import functools
import math

import jax
import jax.numpy as jnp
from jax import lax
from jax.experimental import pallas as pl
from jax.experimental.pallas import tpu as pltpu

F32 = jnp.float32
BF16 = jnp.bfloat16

D_MODEL = 1024
HEAD_DIM = 64
DIFF_HEADS = 4
DIFF_V_DIM = 2 * HEAD_DIM
DSA_HEADS = 4
DSA_LATENT = 128
DSA_V_DIM = 64
IDX_HEADS = 8
IDX_DIM = 32
DSA_TOPK_MAX = 256
MOBA_HEADS = 4
MOBA_BLOCK = 256
MOBA_TOPK_MAX = 3
N_BUCKETS = 32
MAX_DISTANCE = 128
N_BIAS_HEADS = DIFF_HEADS + DSA_HEADS + MOBA_HEADS
D_FF = 4 * D_MODEL
EPS = 1e-6

LANES = 128
ATT_TILE = MOBA_BLOCK
NEG = -1e30
INT_MIN = -2 ** 31
VMEM_LIMIT = 52 * 1024 * 1024
DIFF_HEADS_PER_STEP = 2

_O_AQ, _O_AK, _O_AV, _O_BQ, _O_BKV, _O_BIQ, _O_BIK, _O_BIW, _O_CQ, _O_CK, _O_CV, _O_G = (
    0, 512, 1024, 1536, 2048, 2176, 2432, 2464, 2472, 2728, 2984, 3240)

BLK_G, BLK_AQ, BLK_AK, BLK_AV, BLK_BQ, BLK_BIQ, BLK_BKV, BLK_IK, BLK_CQ, BLK_CK, BLK_CV, BLK_IW = (
    0, 24, 28, 32, 36, 40, 42, 43, 44, 46, 48, 50)
N_BLKS = 51
PACKED_COLS = N_BLKS * LANES


def _pack_columns(w):
    seg = lambda off, n: w[..., off:off + n]
    ik = seg(_O_BIK, IDX_DIM)
    parts = [
        seg(_O_G, 3 * D_MODEL), seg(_O_AQ, 512), seg(_O_AK, 512), seg(_O_AV, 512),
        seg(_O_BQ, 512), seg(_O_BIQ, 256), seg(_O_BKV, 128),
        ik, ik, ik, ik,
        seg(_O_CQ, 256), seg(_O_CK, 256), seg(_O_CV, 256),
        seg(_O_BIW, IDX_HEADS),
        jnp.zeros(w.shape[:-1] + (LANES - IDX_HEADS,), w.dtype),
    ]
    out = jnp.concatenate(parts, axis=-1)
    assert out.shape[-1] == PACKED_COLS
    return out


def _nt(a, b):
    return lax.dot_general(a, b, (((1,), (1,)), ((), ())), preferred_element_type=F32)


def _nn(a, b):
    return jnp.dot(a, b, preferred_element_type=F32)


def _eye(n):
    r = lax.broadcasted_iota(jnp.int32, (n, n), 0)
    c = lax.broadcasted_iota(jnp.int32, (n, n), 1)
    return jnp.where(r == c, 1.0, 0.0).astype(BF16)


def _lane_group_mask(x_bf16, group, width):
    lane = lax.broadcasted_iota(jnp.int32, x_bf16.shape, 1)
    keep = (lane >= group * width) & (lane < (group + 1) * width)
    return jnp.where(keep, x_bf16.astype(F32), 0.0)


def _ada_kernel(c_ref, w_ref, b_ref, o_ref):
    c = c_ref[...]
    cond = c * jax.nn.sigmoid(c)
    o_ref[0] = _nn(cond, w_ref[0]) + b_ref[0]


def _ada(c, ada_w, ada_b):
    depth, d, n = ada_w.shape
    bsz = c.shape[0]
    tn = 1536
    return pl.pallas_call(
        _ada_kernel,
        grid=(depth, n // tn),
        in_specs=[
            pl.BlockSpec((bsz, d), lambda l, j: (0, 0)),
            pl.BlockSpec((1, d, tn), lambda l, j: (l, 0, j)),
            pl.BlockSpec((1, 1, tn), lambda l, j: (l, 0, j)),
        ],
        out_specs=pl.BlockSpec((1, bsz, tn), lambda l, j: (l, 0, j)),
        out_shape=jax.ShapeDtypeStruct((depth, bsz, n), F32),
        compiler_params=pltpu.CompilerParams(
            dimension_semantics=("arbitrary", "arbitrary"), vmem_limit_bytes=VMEM_LIMIT),
        name="ada_mod",
    )(c, ada_w, ada_b.reshape(depth, 1, n))


def _t5_bucket(dist):
    max_exact = N_BUCKETS // 2
    n = jnp.maximum(dist, 0)
    nf = jnp.maximum(n, 1).astype(F32)
    large = max_exact + (jnp.log(nf / max_exact) / math.log(MAX_DISTANCE / max_exact)
                         * (N_BUCKETS - max_exact)).astype(jnp.int32)
    large = jnp.minimum(large, N_BUCKETS - 1)
    return jnp.where(n < max_exact, n, large)


def _bias_kernel(rb_ref, o_ref):
    t = ATT_TILE
    h = pl.program_id(0)
    kk = lax.broadcasted_iota(jnp.int32, (t, t), 0)
    qq = lax.broadcasted_iota(jnp.int32, (t, t), 1)
    for kind in range(3):
        dist = jnp.full((t, t), 2 * t, jnp.int32) if kind == 2 else kind * t + qq - kk
        bucket = _t5_bucket(dist)
        tile = jnp.zeros((t, t), F32)
        for b in range(N_BUCKETS):
            tile = jnp.where(bucket == b, rb_ref[b, h], tile)
        o_ref[0, kind] = jnp.where(dist >= 0, tile, NEG)


def _bias_tiles(rel_bias):
    t = ATT_TILE
    heads = rel_bias.shape[1]
    return pl.pallas_call(
        _bias_kernel,
        grid=(heads,),
        in_specs=[pl.BlockSpec(memory_space=pltpu.SMEM)],
        out_specs=pl.BlockSpec((1, 3, t, t), lambda h: (h, 0, 0, 0)),
        out_shape=jax.ShapeDtypeStruct((heads, 3, t, t), F32),
        compiler_params=pltpu.CompilerParams(dimension_semantics=("arbitrary",)),
        name="bias_tiles",
    )(rel_bias.astype(F32))


def _norm_modulate(x, g, shift, scale):
    ms = jnp.mean(x * x, axis=-1, keepdims=True)
    y = x * lax.rsqrt(ms + EPS) * g
    return y * (1.0 + scale) + shift


def _inproj_kernel(x_ref, mod_ref, g_ref, w_ref, o_ref, u_ref):
    @pl.when(pl.program_id(1) == 0)
    def _():
        u = _norm_modulate(x_ref[...], g_ref[...], mod_ref[0, 0:1, :], mod_ref[0, 1:2, :])
        u_ref[...] = u.astype(BF16)

    o_ref[...] = _nn(u_ref[...], w_ref[...]).astype(o_ref.dtype)


def _inproj(x2, mod_l, g, w_packed, seq):
    m, d = x2.shape
    n = w_packed.shape[1]
    tm = min(1024, seq)
    tn = n // 3
    return pl.pallas_call(
        _inproj_kernel,
        grid=(m // tm, n // tn),
        in_specs=[
            pl.BlockSpec((tm, d), lambda i, j: (i, 0)),
            pl.BlockSpec((1, 6, d), lambda i, j: ((i * tm) // seq, 0, 0)),
            pl.BlockSpec((1, d), lambda i, j: (0, 0)),
            pl.BlockSpec((d, tn), lambda i, j: (0, j)),
        ],
        out_specs=pl.BlockSpec((tm, tn), lambda i, j: (i, j)),
        out_shape=jax.ShapeDtypeStruct((m, n), BF16),
        scratch_shapes=[pltpu.VMEM((tm, d), BF16)],
        compiler_params=pltpu.CompilerParams(
            dimension_semantics=("parallel", "arbitrary"), vmem_limit_bytes=VMEM_LIMIT),
        name="in_proj",
    )(x2, mod_l, g.reshape(1, d), w_packed)


def _attend(n_live, chains, tile_ctx, logits, values, s_ref, m_ref, a_ref, l_ref, acc_ref):
    m_ref[...] = jnp.full(m_ref.shape, NEG, F32)
    l_ref[...] = jnp.zeros(l_ref.shape, F32)
    acc_ref[...] = jnp.zeros(acc_ref.shape, F32)

    def start_tile(j):
        ctx = tile_ctx(j)
        for c in range(chains):
            s = logits(j, c, ctx)
            m_old = m_ref[c]
            m_new = jnp.maximum(m_old, jnp.max(s, axis=0, keepdims=True))
            s_ref[c] = s
            a_ref[c] = jnp.exp(m_old - m_new)
            m_ref[c] = m_new

    def finish_tile(j):
        for c in range(chains):
            p = jnp.exp(s_ref[c] - m_ref[c])
            alpha = a_ref[c]
            l_ref[c] = alpha * l_ref[c] + jnp.sum(p, axis=0, keepdims=True)
            acc_ref[c] = alpha * acc_ref[c] + _nn(values(j, c), p.astype(BF16))

    start_tile(0)

    def body(j, carry):
        finish_tile(j - 1)
        start_tile(j)
        return carry

    lax.fori_loop(1, n_live, body, 0)
    finish_tile(n_live - 1)


def _attend_scratch(chains, dv):
    t = ATT_TILE
    return [
        pltpu.VMEM((chains, t, t), F32),
        pltpu.VMEM((chains, 1, t), F32),
        pltpu.VMEM((chains, 1, t), F32),
        pltpu.VMEM((chains, 1, t), F32),
        pltpu.VMEM((chains, dv, t), F32),
    ]


def _key_tile(ref, j, cols=None):
    t = ATT_TILE
    rows = slice(j * t, (j + 1) * t) if isinstance(j, int) else pl.ds(pl.multiple_of(j * t, t), t)
    return ref[0, rows, :] if cols is None else ref[0, rows, cols]


def _transpose_tiles(src_ref, dst_ref, n_tiles, groups):
    t = ATT_TILE
    eye = _eye(LANES)
    for g in range(groups):
        for j in range(n_tiles):
            blk = src_ref[0, j * t:(j + 1) * t, g * LANES:(g + 1) * LANES]
            dst_ref[g, j] = _nt(eye, blk).astype(BF16)


def _diff_kernel(q_ref, k_ref, v_ref, bias_ref, dl_ref, g_ref, o_ref,
                 vt_ref, qs_ref, s_ref, m_ref, a_ref, l_ref, acc_ref, *, lam_init, n_tiles, hp):
    t = ATT_TILE
    i = pl.program_id(2)

    @pl.when(i == 0)
    def _():
        _transpose_tiles(v_ref, vt_ref, n_tiles, hp)

    scale = HEAD_DIM ** -0.5
    for h in range(hp):
        q = q_ref[0, :, h * LANES:(h + 1) * LANES]
        for half in range(2):
            qs_ref[2 * h + half] = (_lane_group_mask(q, half, HEAD_DIM) * scale).astype(BF16)

    def logits(j, c, ctx):
        h = c // 2
        kblk = _key_tile(k_ref, j, slice(h * LANES, (h + 1) * LANES))
        return _nt(kblk, qs_ref[c]) + bias_ref[h, jnp.minimum(i - j, 2)]

    _attend(i + 1, 2 * hp, lambda j: None, logits, lambda j, c: vt_ref[c // 2, j],
            s_ref, m_ref, a_ref, l_ref, acc_ref)

    dl = dl_ref[...]
    lam = (jnp.exp(jnp.sum(dl[0:1] * dl[1:2], keepdims=True))
           - jnp.exp(jnp.sum(dl[2:3] * dl[3:4], keepdims=True)) + lam_init)
    eye = _eye(t)
    for h in range(hp):
        o = acc_ref[2 * h] / l_ref[2 * h] - lam * (acc_ref[2 * h + 1] / l_ref[2 * h + 1])
        ms = jnp.mean(o * o, axis=0, keepdims=True)
        o = o * lax.rsqrt(ms + EPS) * g_ref[...] * (1.0 - lam_init)
        o_ref[0, :, h * LANES:(h + 1) * LANES] = _nt(eye, o.astype(BF16)).astype(o_ref.dtype)


def _diff_attention(p3, bias_tiles, diff_lambda_l, subln_g, lam_init):
    bsz, seq, _ = p3.shape
    t = ATT_TILE
    n_tiles = seq // t
    hp = DIFF_HEADS_PER_STEP
    w = hp * LANES
    kern = functools.partial(_diff_kernel, lam_init=lam_init, n_tiles=n_tiles, hp=hp)
    return pl.pallas_call(
        kern,
        grid=(bsz, DIFF_HEADS // hp, n_tiles),
        in_specs=[
            pl.BlockSpec((1, t, w), lambda b, h, i: (b, i, BLK_AQ // hp + h)),
            pl.BlockSpec((1, seq, w), lambda b, h, i: (b, 0, BLK_AK // hp + h)),
            pl.BlockSpec((1, seq, w), lambda b, h, i: (b, 0, BLK_AV // hp + h)),
            pl.BlockSpec((hp, 3, t, t), lambda b, h, i: (h, 0, 0, 0)),
            pl.BlockSpec((4, HEAD_DIM), lambda b, h, i: (0, 0)),
            pl.BlockSpec((DIFF_V_DIM, 1), lambda b, h, i: (0, 0)),
        ],
        out_specs=pl.BlockSpec((1, t, w), lambda b, h, i: (b, i, h)),
        out_shape=jax.ShapeDtypeStruct((bsz, seq, DIFF_HEADS * DIFF_V_DIM), BF16),
        scratch_shapes=[
            pltpu.VMEM((hp, n_tiles, LANES, t), BF16),
            pltpu.VMEM((2 * hp, t, LANES), BF16),
        ] + _attend_scratch(2 * hp, DIFF_V_DIM),
        compiler_params=pltpu.CompilerParams(
            dimension_semantics=("parallel", "parallel", "arbitrary"),
            vmem_limit_bytes=VMEM_LIMIT),
        name="diff_attn",
    )(p3, p3, p3, bias_tiles, diff_lambda_l, subln_g.reshape(DIFF_V_DIM, 1))


def _dsa_kernel(q_ref, kv_ref, iq_ref, ik_ref, iw_ref, bias_ref, g_ref, wuv_ref, o_ref,
                kvn_ref, kvt_ref, keys_ref, thr_ref, cut_ref, iqh_ref,
                s_ref, m_ref, a_ref, l_ref, acc_ref, *, n_tiles, topk):
    t = ATT_TILE
    i = pl.program_id(1)
    n_live = i + 1

    @pl.when(i == 0)
    def _():
        kv = kv_ref[0].astype(F32)
        ms = jnp.mean(kv * kv, axis=-1, keepdims=True)
        kvn_ref[0] = (kv * lax.rsqrt(ms + EPS) * g_ref[...]).astype(BF16)
        _transpose_tiles(kvn_ref, kvt_ref, n_tiles, 1)

    groups = LANES // IDX_DIM
    for h in range(IDX_HEADS):
        blk = iq_ref[0, :, (h // groups) * LANES:(h // groups + 1) * LANES]
        iqh_ref[h] = _lane_group_mask(blk, h % groups, IDX_DIM).astype(BF16)
    sel_rows = lax.broadcasted_iota(jnp.int32, (IDX_HEADS, LANES), 0)
    sel_cols = lax.broadcasted_iota(jnp.int32, (IDX_HEADS, LANES), 1)
    pick = jnp.where(sel_rows == sel_cols, 1.0, 0.0).astype(BF16)
    iw_t = _nt(pick, iw_ref[0])
    idx_scale = (IDX_HEADS ** -0.5) * (IDX_DIM ** -0.5)
    row = lax.broadcasted_iota(jnp.int32, (t, t), 0)
    col = lax.broadcasted_iota(jnp.int32, (t, t), 1)

    def score_body(j, carry):
        ik = _key_tile(ik_ref, j)
        sc = jnp.zeros((t, t), F32)
        for h in range(IDX_HEADS):
            sc = sc + jnp.maximum(_nt(ik, iqh_ref[h]), 0.0) * iw_t[h:h + 1, :]
        sc = sc * idx_scale
        bits = pltpu.bitcast(sc, jnp.int32)
        key = bits ^ ((bits >> 31) & 0x7FFFFFFF)
        valid = (i * t + col) >= (j * t + row)
        keys_ref[j] = jnp.where(valid, key, INT_MIN)
        return carry

    lax.fori_loop(0, n_live, score_body, 0)

    def count(pred):
        def cb(j, acc):
            hit = pred(keys_ref[j], j)
            return acc + jnp.sum(jnp.where(hit, 1.0, 0.0), axis=0, keepdims=True)
        return lax.fori_loop(0, n_live, cb, jnp.zeros((1, t), F32))

    def bit_body(it, thr):
        cand = thr ^ lax.shift_left(jnp.int32(1), 31 - it)
        cnt = count(lambda k, j: k >= cand)
        return jnp.where(cnt >= topk, cand, thr)

    thr = lax.fori_loop(0, 32, bit_body, jnp.full((1, t), INT_MIN, jnp.int32))
    n_gt = count(lambda k, j: k > thr)
    n_ge = count(lambda k, j: k >= thr)
    need = topk - n_gt
    has_rank = thr > INT_MIN
    excess = jnp.where(has_rank & (n_ge > topk), 1.0, 0.0)
    thr_ref[...] = thr
    cut_ref[...] = jnp.where(has_rank, jnp.int32(2 * n_tiles * t), jnp.int32(0))

    @pl.when(jnp.max(excess) > 0.0)
    def _():
        n_bits = (2 * n_tiles * t - 1).bit_length()

        def cut_body(it, cut):
            cand = cut | lax.shift_left(jnp.int32(1), n_bits - 1 - it)
            cnt = count(lambda k, j: (k == thr) & ((j * t + row) < cand))
            return jnp.where(cnt <= need, cand, cut)

        cut = lax.fori_loop(0, n_bits, cut_body, jnp.zeros((1, t), jnp.int32))
        cut_ref[...] = jnp.where(has_rank, cut, jnp.int32(0))

    scale = DSA_LATENT ** -0.5

    def tile_ctx(j):
        key = keys_ref[j]
        thr_v = thr_ref[...]
        sel = (key > thr_v) | ((key == thr_v) & ((j * t + row) < cut_ref[...]))
        return sel, _key_tile(kvn_ref, j), jnp.minimum(i - j, 2)

    def logits(j, h, ctx):
        sel, kvb, kind = ctx
        q_h = q_ref[0, :, h * LANES:(h + 1) * LANES]
        return jnp.where(sel, _nt(kvb, q_h) * scale + bias_ref[h, kind], NEG)

    _attend(n_live, DSA_HEADS, tile_ctx, logits, lambda j, h: kvt_ref[0, j],
            s_ref, m_ref, a_ref, l_ref, acc_ref)

    y_t = jnp.zeros((DSA_HEADS * DSA_V_DIM, t), F32)
    for h in range(DSA_HEADS):
        o_h = (acc_ref[h] / l_ref[h]).astype(BF16)
        y_t = y_t + _nn(wuv_ref[h], o_h)
    o_ref[0] = _nt(_eye(t), y_t.astype(BF16)).astype(o_ref.dtype)


def _dsa_attention(p3, bias_tiles, kv_norm_g, wuv_t_pad):
    bsz, seq, _ = p3.shape
    t = ATT_TILE
    n_tiles = seq // t
    topk = min(DSA_TOPK_MAX, seq // 4)
    kern = functools.partial(_dsa_kernel, n_tiles=n_tiles, topk=float(topk))
    n_out = DSA_HEADS * DSA_V_DIM
    return pl.pallas_call(
        kern,
        grid=(bsz, n_tiles),
        in_specs=[
            pl.BlockSpec((1, t, 4 * LANES), lambda b, i: (b, i, BLK_BQ // 4)),
            pl.BlockSpec((1, seq, LANES), lambda b, i: (b, 0, BLK_BKV)),
            pl.BlockSpec((1, t, 2 * LANES), lambda b, i: (b, i, BLK_BIQ // 2)),
            pl.BlockSpec((1, seq, LANES), lambda b, i: (b, 0, BLK_IK)),
            pl.BlockSpec((1, t, LANES), lambda b, i: (b, i, BLK_IW)),
            pl.BlockSpec((DSA_HEADS, 3, t, t), lambda b, i: (DIFF_HEADS // DSA_HEADS, 0, 0, 0)),
            pl.BlockSpec((1, DSA_LATENT), lambda b, i: (0, 0)),
            pl.BlockSpec((DSA_HEADS, n_out, DSA_LATENT), lambda b, i: (0, 0, 0)),
        ],
        out_specs=pl.BlockSpec((1, t, n_out), lambda b, i: (b, i, 0)),
        out_shape=jax.ShapeDtypeStruct((bsz, seq, n_out), BF16),
        scratch_shapes=[
            pltpu.VMEM((1, seq, DSA_LATENT), BF16),
            pltpu.VMEM((1, n_tiles, DSA_LATENT, t), BF16),
            pltpu.VMEM((n_tiles, t, t), jnp.int32),
            pltpu.VMEM((1, t), jnp.int32),
            pltpu.VMEM((1, t), jnp.int32),
            pltpu.VMEM((IDX_HEADS, t, LANES), BF16),
        ] + _attend_scratch(DSA_HEADS, DSA_LATENT),
        compiler_params=pltpu.CompilerParams(
            dimension_semantics=("parallel", "arbitrary"), vmem_limit_bytes=VMEM_LIMIT),
        name="dsa_attn",
    )(p3, p3, p3, p3, p3, bias_tiles, kv_norm_g.reshape(1, DSA_LATENT), wuv_t_pad)


def _moba_kernel(q_ref, k_ref, v_ref, bias_ref, o_ref,
                 vt_ref, kmean_ref, selb_ref, qs_ref,
                 s_ref, m_ref, a_ref, l_ref, acc_ref, *, n_tiles, topb):
    t = ATT_TILE
    i = pl.program_id(1)
    pairs = MOBA_HEADS // 2

    @pl.when(i == 0)
    def _():
        _transpose_tiles(v_ref, vt_ref, n_tiles, pairs)
        for n in range(n_tiles):
            kb = k_ref[0, n * t:(n + 1) * t, :].astype(F32)
            kmean_ref[n:n + 1, :] = jnp.mean(kb, axis=0, keepdims=True)

    scale = HEAD_DIM ** -0.5
    blk = lax.broadcasted_iota(jnp.int32, (n_tiles, t), 0)
    own = jnp.full((1, t), i, jnp.int32)
    for hd in range(MOBA_HEADS):
        g, half = hd // 2, hd % 2
        q_f = _lane_group_mask(q_ref[0, :, g * LANES:(g + 1) * LANES], half, HEAD_DIM)
        qs_ref[hd] = (q_f * scale).astype(BF16)
        kmean = kmean_ref[:, g * LANES:(g + 1) * LANES].astype(BF16)
        gate = _nt(kmean, q_f.astype(BF16))
        for n in range(n_tiles):
            gn = gate[n:n + 1, :]
            ahead = (gate > gn) | ((gate == gn) & (blk < n))
            ahead = ahead & (blk < i)
            rank = jnp.sum(jnp.where(ahead, 1.0, 0.0), axis=0, keepdims=True)
            chosen = ((rank < topb) & (own > n)) | (own == n)
            selb_ref[hd, n] = jnp.broadcast_to(jnp.where(chosen, 0.0, NEG), (8, t))

    def logits(j, hd, ctx):
        g = hd // 2
        kblk = _key_tile(k_ref, j, slice(g * LANES, (g + 1) * LANES))
        return (_nt(kblk, qs_ref[hd]) + bias_ref[hd, jnp.minimum(i - j, 2)]
                + selb_ref[hd, j][0:1, :])

    _attend(i + 1, MOBA_HEADS, lambda j: None, logits, lambda j, hd: vt_ref[hd // 2, j],
            s_ref, m_ref, a_ref, l_ref, acc_ref)

    rows = lax.broadcasted_iota(jnp.int32, (LANES, t), 0)
    eye = _eye(t)
    for g in range(pairs):
        o_lo = acc_ref[2 * g] / l_ref[2 * g]
        o_hi = acc_ref[2 * g + 1] / l_ref[2 * g + 1]
        o = jnp.where(rows < HEAD_DIM, o_lo, o_hi)
        o_ref[0, :, g * LANES:(g + 1) * LANES] = _nt(eye, o.astype(BF16)).astype(o_ref.dtype)


def _moba_attention(p3, bias_tiles):
    bsz, seq, _ = p3.shape
    t = ATT_TILE
    n_tiles = seq // t
    topb = min(MOBA_TOPK_MAX, n_tiles)
    kern = functools.partial(_moba_kernel, n_tiles=n_tiles, topb=float(topb))
    w = MOBA_HEADS * HEAD_DIM
    blocks = w // LANES
    first = (DIFF_HEADS + DSA_HEADS) // MOBA_HEADS
    return pl.pallas_call(
        kern,
        grid=(bsz, n_tiles),
        in_specs=[
            pl.BlockSpec((1, t, w), lambda b, i: (b, i, BLK_CQ // blocks)),
            pl.BlockSpec((1, seq, w), lambda b, i: (b, 0, BLK_CK // blocks)),
            pl.BlockSpec((1, seq, w), lambda b, i: (b, 0, BLK_CV // blocks)),
            pl.BlockSpec((MOBA_HEADS, 3, t, t), lambda b, i: (first, 0, 0, 0)),
        ],
        out_specs=pl.BlockSpec((1, t, w), lambda b, i: (b, i, 0)),
        out_shape=jax.ShapeDtypeStruct((bsz, seq, w), BF16),
        scratch_shapes=[
            pltpu.VMEM((blocks, n_tiles, LANES, t), BF16),
            pltpu.VMEM((n_tiles, w), F32),
            pltpu.VMEM((MOBA_HEADS, n_tiles, 8, t), F32),
            pltpu.VMEM((MOBA_HEADS, t, LANES), BF16),
        ] + _attend_scratch(MOBA_HEADS, LANES),
        compiler_params=pltpu.CompilerParams(
            dimension_semantics=("parallel", "arbitrary"), vmem_limit_bytes=VMEM_LIMIT),
        name="moba_attn",
    )(p3, p3, p3, bias_tiles)


def _merge_kernel(x_ref, g_ref, gb_ref, oa_ref, ob_ref, oc_ref, wa_ref, wb_ref, wc_ref, wo_ref,
                  mod_ref, o_ref):
    d = D_MODEL
    gates = jax.nn.sigmoid(g_ref[...].astype(F32) + gb_ref[...])
    merged = (gates[:, 0:d] * _nn(oa_ref[...], wa_ref[...])
              + gates[:, d:2 * d] * _nn(ob_ref[...], wb_ref[...])
              + gates[:, 2 * d:3 * d] * _nn(oc_ref[...], wc_ref[...]))
    z = _nn(merged.astype(BF16), wo_ref[...])
    o_ref[...] = x_ref[...] + mod_ref[0, 2:3, :] * z


def _merge(x2, p2, gate_b, oa, ob, oc, wa, wb, wc, wo, mod_l, seq):
    m, d = x2.shape
    tm = min(512, seq)
    full = lambda a: pl.BlockSpec(a.shape, lambda i: (0, 0))
    return pl.pallas_call(
        _merge_kernel,
        grid=(m // tm,),
        in_specs=[
            pl.BlockSpec((tm, d), lambda i: (i, 0)),
            pl.BlockSpec((tm, 3 * d), lambda i: (i, BLK_G)),
            pl.BlockSpec((1, 3 * d), lambda i: (0, 0)),
            pl.BlockSpec((tm, oa.shape[1]), lambda i: (i, 0)),
            pl.BlockSpec((tm, ob.shape[1]), lambda i: (i, 0)),
            pl.BlockSpec((tm, oc.shape[1]), lambda i: (i, 0)),
            full(wa), full(wb), full(wc), full(wo),
            pl.BlockSpec((1, 6, d), lambda i: ((i * tm) // seq, 0, 0)),
        ],
        out_specs=pl.BlockSpec((tm, d), lambda i: (i, 0)),
        out_shape=jax.ShapeDtypeStruct((m, d), F32),
        compiler_params=pltpu.CompilerParams(
            dimension_semantics=("parallel",), vmem_limit_bytes=VMEM_LIMIT),
        name="merge",
    )(x2, p2, gate_b.reshape(1, 3 * d), oa, ob, oc, wa, wb, wc, wo, mod_l)


def _mlp_kernel(x_ref, mod_ref, g_ref, w1_ref, w2_ref, gf_ref, o_ref, u_ref, acc_ref, *, final):
    f = pl.program_id(1)

    @pl.when(f == 0)
    def _():
        u = _norm_modulate(x_ref[...], g_ref[...], mod_ref[0, 3:4, :], mod_ref[0, 4:5, :])
        u_ref[...] = u.astype(BF16)
        acc_ref[...] = jnp.zeros(acc_ref.shape, F32)

    h = jnp.square(jnp.maximum(_nn(u_ref[...], w1_ref[...]), 0.0))
    acc_ref[...] += _nn(h.astype(BF16), w2_ref[...])

    @pl.when(f == pl.num_programs(1) - 1)
    def _():
        y = x_ref[...] + mod_ref[0, 5:6, :] * acc_ref[...]
        if final:
            ms = jnp.mean(y * y, axis=-1, keepdims=True)
            y = y * lax.rsqrt(ms + EPS) * gf_ref[...]
        o_ref[...] = y


def _mlp(x2, mod_l, g, w1, w2, g_final, seq, final):
    m, d = x2.shape
    dff = w1.shape[1]
    tm = min(1024, seq)
    tf = 1024
    return pl.pallas_call(
        functools.partial(_mlp_kernel, final=final),
        grid=(m // tm, dff // tf),
        in_specs=[
            pl.BlockSpec((tm, d), lambda i, f: (i, 0)),
            pl.BlockSpec((1, 6, d), lambda i, f: ((i * tm) // seq, 0, 0)),
            pl.BlockSpec((1, d), lambda i, f: (0, 0)),
            pl.BlockSpec((d, tf), lambda i, f: (0, f)),
            pl.BlockSpec((tf, d), lambda i, f: (f, 0)),
            pl.BlockSpec((1, d), lambda i, f: (0, 0)),
        ],
        out_specs=pl.BlockSpec((tm, d), lambda i, f: (i, 0)),
        out_shape=jax.ShapeDtypeStruct((m, d), F32),
        scratch_shapes=[pltpu.VMEM((tm, d), BF16), pltpu.VMEM((tm, d), F32)],
        compiler_params=pltpu.CompilerParams(
            dimension_semantics=("parallel", "arbitrary"), vmem_limit_bytes=VMEM_LIMIT),
        name="mlp",
    )(x2, mod_l, g.reshape(1, d), w1, w2, g_final.reshape(1, d))


def kernel(x, c, rel_bias, ada_w, ada_b, norm_mix, w_in, gate_b, diff_lambda, diff_subln,
           dsa_kv_norm, dsa_w_uv, w_br_a, w_br_b, w_br_c, w_o, norm_mlp, w_ff1, w_ff2,
           norm_final):
    bsz, seq, d = x.shape
    depth = w_in.shape[0]
    assert seq % ATT_TILE == 0 and 2 * ATT_TILE > MAX_DISTANCE

    w_in_p = _pack_columns(w_in).astype(BF16)
    n_out = DSA_HEADS * DSA_V_DIM
    wuv_t = jnp.transpose(dsa_w_uv, (0, 1, 3, 2))
    wuv_t_pad = jnp.zeros((depth, DSA_HEADS, n_out, DSA_LATENT), F32)
    for h in range(DSA_HEADS):
        wuv_t_pad = wuv_t_pad.at[:, h, h * DSA_V_DIM:(h + 1) * DSA_V_DIM, :].set(wuv_t[:, h])
    wuv_t_pad = wuv_t_pad.astype(BF16)
    wa, wb, wc, wo = (w.astype(BF16) for w in (w_br_a, w_br_b, w_br_c, w_o))
    w1, w2 = w_ff1.astype(BF16), w_ff2.astype(BF16)

    bias_tiles = _bias_tiles(rel_bias)
    mod = _ada(c, ada_w, ada_b).reshape(depth, bsz, 6, d)

    x2 = x.reshape(bsz * seq, d)
    for l in range(depth):
        lam_init = 0.8 - 0.6 * math.exp(-0.3 * l)
        p2 = _inproj(x2, mod[l], norm_mix[l], w_in_p[l], seq)
        p3 = p2.reshape(bsz, seq, PACKED_COLS)
        oa = _diff_attention(p3, bias_tiles, diff_lambda[l], diff_subln[l], lam_init)
        ob = _dsa_attention(p3, bias_tiles, dsa_kv_norm[l], wuv_t_pad[l])
        oc = _moba_attention(p3, bias_tiles)
        x2 = _merge(x2, p2, gate_b[l],
                    oa.reshape(bsz * seq, -1), ob.reshape(bsz * seq, -1), oc.reshape(bsz * seq, -1),
                    wa[l], wb[l], wc[l], wo[l], mod[l], seq)
        x2 = _mlp(x2, mod[l], norm_mlp[l], w1[l], w2[l], norm_final, seq, final=(l == depth - 1))
    return x2.reshape(bsz, seq, d)
```

```python
import functools
import math

import jax
import jax.numpy as jnp
from jax import lax
from jax.experimental import pallas as pl
from jax.experimental.pallas import tpu as pltpu

F32 = jnp.float32
BF16 = jnp.bfloat16

D_MODEL = 1024
HEAD_DIM = 64
DIFF_HEADS = 4
DIFF_V_DIM = 2 * HEAD_DIM
DSA_HEADS = 4
DSA_LATENT = 128
DSA_V_DIM = 64
IDX_HEADS = 8
IDX_DIM = 32
DSA_TOPK_MAX = 256
MOBA_HEADS = 4
MOBA_BLOCK = 256
MOBA_TOPK_MAX = 3
N_BUCKETS = 32
MAX_DISTANCE = 128
N_BIAS_HEADS = DIFF_HEADS + DSA_HEADS + MOBA_HEADS
D_FF = 4 * D_MODEL
EPS = 1e-6

LANES = 128
ATT_TILE = MOBA_BLOCK
NEG = -1e30
INT_MIN = -2 ** 31
VMEM_LIMIT = 52 * 1024 * 1024
DIFF_HEADS_PER_STEP = 2
SUM_ROWS = 16
LOG2E = math.log2(math.e)

_O_AQ, _O_AK, _O_AV, _O_BQ, _O_BKV, _O_BIQ, _O_BIK, _O_BIW, _O_CQ, _O_CK, _O_CV, _O_G = (
    0, 512, 1024, 1536, 2048, 2176, 2432, 2464, 2472, 2728, 2984, 3240)

BLK_G, BLK_AQ, BLK_AK, BLK_AV, BLK_BQ, BLK_BIQ, BLK_BKV, BLK_IK, BLK_CQ, BLK_CK, BLK_CV, BLK_IW = (
    0, 24, 28, 32, 36, 40, 42, 43, 44, 46, 48, 50)
N_BLKS = 51
PACKED_COLS = N_BLKS * LANES


def _pack_columns(w):
    seg = lambda off, n: w[..., off:off + n]
    ik = seg(_O_BIK, IDX_DIM)
    parts = [
        seg(_O_G, 3 * D_MODEL), seg(_O_AQ, 512), seg(_O_AK, 512), seg(_O_AV, 512),
        seg(_O_BQ, 512), seg(_O_BIQ, 256), seg(_O_BKV, 128),
        ik, ik, ik, ik,
        seg(_O_CQ, 256), seg(_O_CK, 256), seg(_O_CV, 256),
        seg(_O_BIW, IDX_HEADS),
        jnp.zeros(w.shape[:-1] + (LANES - IDX_HEADS,), w.dtype),
    ]
    out = jnp.concatenate(parts, axis=-1)
    assert out.shape[-1] == PACKED_COLS
    return out


def _nt(a, b):
    return lax.dot_general(a, b, (((1,), (1,)), ((), ())), preferred_element_type=F32)


def _nn(a, b):
    return jnp.dot(a, b, preferred_element_type=F32)


def _eye(n):
    r = lax.broadcasted_iota(jnp.int32, (n, n), 0)
    c = lax.broadcasted_iota(jnp.int32, (n, n), 1)
    return jnp.where(r == c, 1.0, 0.0).astype(BF16)


def _lane_group_mask(x_bf16, group, width):
    lane = lax.broadcasted_iota(jnp.int32, x_bf16.shape, 1)
    keep = (lane >= group * width) & (lane < (group + 1) * width)
    return jnp.where(keep, x_bf16.astype(F32), 0.0)


def _ada_kernel(c_ref, w_ref, b_ref, o_ref):
    c = c_ref[...]
    cond = c * jax.nn.sigmoid(c)
    o_ref[0] = _nn(cond, w_ref[0]) + b_ref[0]


def _ada(c, ada_w, ada_b):
    depth, d, n = ada_w.shape
    bsz = c.shape[0]
    tn = 1536
    return pl.pallas_call(
        _ada_kernel,
        grid=(depth, n // tn),
        in_specs=[
            pl.BlockSpec((bsz, d), lambda l, j: (0, 0)),
            pl.BlockSpec((1, d, tn), lambda l, j: (l, 0, j)),
            pl.BlockSpec((1, 1, tn), lambda l, j: (l, 0, j)),
        ],
        out_specs=pl.BlockSpec((1, bsz, tn), lambda l, j: (l, 0, j)),
        out_shape=jax.ShapeDtypeStruct((depth, bsz, n), F32),
        compiler_params=pltpu.CompilerParams(
            dimension_semantics=("arbitrary", "arbitrary"), vmem_limit_bytes=VMEM_LIMIT),
        name="ada_mod",
    )(c, ada_w, ada_b.reshape(depth, 1, n))


def _t5_bucket(dist):
    max_exact = N_BUCKETS // 2
    n = jnp.maximum(dist, 0)
    nf = jnp.maximum(n, 1).astype(F32)
    large = max_exact + (jnp.log(nf / max_exact) / math.log(MAX_DISTANCE / max_exact)
                         * (N_BUCKETS - max_exact)).astype(jnp.int32)
    large = jnp.minimum(large, N_BUCKETS - 1)
    return jnp.where(n < max_exact, n, large)


def _bias_kernel(rb_ref, o_ref):
    t = ATT_TILE
    h = pl.program_id(0)
    kk = lax.broadcasted_iota(jnp.int32, (t, t), 0)
    qq = lax.broadcasted_iota(jnp.int32, (t, t), 1)
    for kind in range(3):
        dist = jnp.full((t, t), 2 * t, jnp.int32) if kind == 2 else kind * t + qq - kk
        bucket = _t5_bucket(dist)
        tile = jnp.zeros((t, t), F32)
        for b in range(N_BUCKETS):
            tile = jnp.where(bucket == b, rb_ref[b, h], tile)
        o_ref[0, kind] = jnp.where(dist >= 0, tile * LOG2E, NEG)


def _bias_tiles(rel_bias):
    t = ATT_TILE
    heads = rel_bias.shape[1]
    return pl.pallas_call(
        _bias_kernel,
        grid=(heads,),
        in_specs=[pl.BlockSpec(memory_space=pltpu.SMEM)],
        out_specs=pl.BlockSpec((1, 3, t, t), lambda h: (h, 0, 0, 0)),
        out_shape=jax.ShapeDtypeStruct((heads, 3, t, t), F32),
        compiler_params=pltpu.CompilerParams(dimension_semantics=("arbitrary",)),
        name="bias_tiles",
    )(rel_bias.astype(F32))


def _norm_modulate(x, g, shift, scale):
    ms = jnp.mean(x * x, axis=-1, keepdims=True)
    y = x * lax.rsqrt(ms + EPS) * g
    return y * (1.0 + scale) + shift


def _inproj_kernel(x_ref, mod_ref, g_ref, w_ref, o_ref, u_ref):
    @pl.when(pl.program_id(1) == 0)
    def _():
        u = _norm_modulate(x_ref[...], g_ref[...], mod_ref[0, 0:1, :], mod_ref[0, 1:2, :])
        u_ref[...] = u.astype(BF16)

    o_ref[...] = _nn(u_ref[...], w_ref[...]).astype(o_ref.dtype)


def _inproj(x2, mod_l, g, w_packed, seq):
    m, d = x2.shape
    n = w_packed.shape[1]
    tm = min(1024, seq)
    tn = n // 3
    return pl.pallas_call(
        _inproj_kernel,
        grid=(m // tm, n // tn),
        in_specs=[
            pl.BlockSpec((tm, d), lambda i, j: (i, 0)),
            pl.BlockSpec((1, 6, d), lambda i, j: ((i * tm) // seq, 0, 0)),
            pl.BlockSpec((1, d), lambda i, j: (0, 0)),
            pl.BlockSpec((d, tn), lambda i, j: (0, j)),
        ],
        out_specs=pl.BlockSpec((tm, tn), lambda i, j: (i, j)),
        out_shape=jax.ShapeDtypeStruct((m, n), BF16),
        scratch_shapes=[pltpu.VMEM((tm, d), BF16)],
        compiler_params=pltpu.CompilerParams(
            dimension_semantics=("parallel", "arbitrary"), vmem_limit_bytes=VMEM_LIMIT),
        name="in_proj",
    )(x2, mod_l, g.reshape(1, d), w_packed)


def _attend(n_live, chains, tile_ctx, logits, values, s_ref, m_ref, a_ref, acc_ref):
    m_ref[...] = jnp.full(m_ref.shape, NEG, F32)
    acc_ref[...] = jnp.zeros(acc_ref.shape, F32)

    def start_tile(j):
        ctx = tile_ctx(j)
        for c in range(chains):
            s = logits(j, c, ctx)
            m_old = m_ref[c]
            m_new = jnp.maximum(m_old, jnp.max(s, axis=0, keepdims=True))
            s_ref[c] = s
            a_ref[c] = jnp.exp2(m_old - m_new)
            m_ref[c] = m_new

    def finish_tile(j):
        for c in range(chains):
            p = jnp.exp2(s_ref[c] - m_ref[c])
            acc_ref[c] = a_ref[c] * acc_ref[c] + _nn(values(j, c), p.astype(BF16))

    start_tile(0)

    def body(j, carry):
        finish_tile(j - 1)
        start_tile(j)
        return carry

    lax.fori_loop(1, n_live, body, 0)
    finish_tile(n_live - 1)


def _attend_scratch(chains, dv):
    t = ATT_TILE
    return [
        pltpu.VMEM((chains, t, t), F32),
        pltpu.VMEM((chains, 1, t), F32),
        pltpu.VMEM((chains, 1, t), F32),
        pltpu.VMEM((chains, dv + SUM_ROWS, t), F32),
    ]


def _attend_result(acc_ref, c, dv):
    return acc_ref[c, 0:dv, :] / acc_ref[c, dv:dv + 1, :]


def _key_tile(ref, j, cols=None):
    t = ATT_TILE
    rows = slice(j * t, (j + 1) * t) if isinstance(j, int) else pl.ds(pl.multiple_of(j * t, t), t)
    return ref[0, rows, :] if cols is None else ref[0, rows, cols]


def _transpose_tiles(src_ref, dst_ref, n_tiles, groups):
    t = ATT_TILE
    eye = _eye(LANES)
    for g in range(groups):
        for j in range(n_tiles):
            blk = src_ref[0, j * t:(j + 1) * t, g * LANES:(g + 1) * LANES]
            dst_ref[g, j, 0:LANES, :] = _nt(eye, blk).astype(BF16)
            dst_ref[g, j, LANES:LANES + SUM_ROWS, :] = jnp.ones((SUM_ROWS, t), BF16)


def _diff_kernel(q_ref, k_ref, v_ref, bias_ref, dl_ref, g_ref, o_ref,
                 vt_ref, qs_ref, s_ref, m_ref, a_ref, acc_ref, *, lam_init, n_tiles, hp):
    t = ATT_TILE
    i = pl.program_id(2)

    @pl.when(i == 0)
    def _():
        _transpose_tiles(v_ref, vt_ref, n_tiles, hp)

    scale = HEAD_DIM ** -0.5 * LOG2E
    for h in range(hp):
        q = q_ref[0, :, h * LANES:(h + 1) * LANES]
        for half in range(2):
            qs_ref[2 * h + half] = (_lane_group_mask(q, half, HEAD_DIM) * scale).astype(BF16)

    def logits(j, c, ctx):
        h = c // 2
        kblk = _key_tile(k_ref, j, slice(h * LANES, (h + 1) * LANES))
        return _nt(kblk, qs_ref[c]) + bias_ref[h, jnp.minimum(i - j, 2)]

    _attend(i + 1, 2 * hp, lambda j: None, logits, lambda j, c: vt_ref[c // 2, j],
            s_ref, m_ref, a_ref, acc_ref)

    dl = dl_ref[...]
    lam = (jnp.exp(jnp.sum(dl[0:1] * dl[1:2], keepdims=True))
           - jnp.exp(jnp.sum(dl[2:3] * dl[3:4], keepdims=True)) + lam_init)
    eye = _eye(t)
    for h in range(hp):
        o = (_attend_result(acc_ref, 2 * h, DIFF_V_DIM)
             - lam * _attend_result(acc_ref, 2 * h + 1, DIFF_V_DIM))
        ms = jnp.mean(o * o, axis=0, keepdims=True)
        o = o * lax.rsqrt(ms + EPS) * g_ref[...] * (1.0 - lam_init)
        o_ref[0, :, h * LANES:(h + 1) * LANES] = _nt(eye, o.astype(BF16)).astype(o_ref.dtype)


def _diff_attention(p3, bias_tiles, diff_lambda_l, subln_g, lam_init):
    bsz, seq, _ = p3.shape
    t = ATT_TILE
    n_tiles = seq // t
    hp = DIFF_HEADS_PER_STEP
    w = hp * LANES
    kern = functools.partial(_diff_kernel, lam_init=lam_init, n_tiles=n_tiles, hp=hp)
    return pl.pallas_call(
        kern,
        grid=(bsz, DIFF_HEADS // hp, n_tiles),
        in_specs=[
            pl.BlockSpec((1, t, w), lambda b, h, i: (b, i, BLK_AQ // hp + h)),
            pl.BlockSpec((1, seq, w), lambda b, h, i: (b, 0, BLK_AK // hp + h)),
            pl.BlockSpec((1, seq, w), lambda b, h, i: (b, 0, BLK_AV // hp + h)),
            pl.BlockSpec((hp, 3, t, t), lambda b, h, i: (h, 0, 0, 0)),
            pl.BlockSpec((4, HEAD_DIM), lambda b, h, i: (0, 0)),
            pl.BlockSpec((DIFF_V_DIM, 1), lambda b, h, i: (0, 0)),
        ],
        out_specs=pl.BlockSpec((1, t, w), lambda b, h, i: (b, i, h)),
        out_shape=jax.ShapeDtypeStruct((bsz, seq, DIFF_HEADS * DIFF_V_DIM), BF16),
        scratch_shapes=[
            pltpu.VMEM((hp, n_tiles, LANES + SUM_ROWS, t), BF16),
            pltpu.VMEM((2 * hp, t, LANES), BF16),
        ] + _attend_scratch(2 * hp, DIFF_V_DIM),
        compiler_params=pltpu.CompilerParams(
            dimension_semantics=("parallel", "parallel", "arbitrary"),
            vmem_limit_bytes=VMEM_LIMIT),
        name="diff_attn",
    )(p3, p3, p3, bias_tiles, diff_lambda_l, subln_g.reshape(DIFF_V_DIM, 1))


def _dsa_kernel(q_ref, kv_ref, iq_ref, ik_ref, iw_ref, bias_ref, g_ref, wuv_ref, o_ref,
                kvn_ref, kvt_ref, keys_ref, byte_ref, cand_ref, thr_ref, cut_ref, iqh_ref,
                s_ref, m_ref, a_ref, acc_ref, *, n_tiles, topk):
    t = ATT_TILE
    i = pl.program_id(1)
    n_live = i + 1

    @pl.when(i == 0)
    def _():
        kv = kv_ref[0].astype(F32)
        ms = jnp.mean(kv * kv, axis=-1, keepdims=True)
        kvn_ref[0] = (kv * lax.rsqrt(ms + EPS) * g_ref[...]).astype(BF16)
        _transpose_tiles(kvn_ref, kvt_ref, n_tiles, 1)

    groups = LANES // IDX_DIM
    for h in range(IDX_HEADS):
        blk = iq_ref[0, :, (h // groups) * LANES:(h // groups + 1) * LANES]
        iqh_ref[h] = _lane_group_mask(blk, h % groups, IDX_DIM).astype(BF16)
    sel_rows = lax.broadcasted_iota(jnp.int32, (IDX_HEADS, LANES), 0)
    sel_cols = lax.broadcasted_iota(jnp.int32, (IDX_HEADS, LANES), 1)
    pick = jnp.where(sel_rows == sel_cols, 1.0, 0.0).astype(BF16)
    iw_t = _nt(pick, iw_ref[0])
    idx_scale = (IDX_HEADS ** -0.5) * (IDX_DIM ** -0.5)
    row = lax.broadcasted_iota(jnp.int32, (t, t), 0)
    col = lax.broadcasted_iota(jnp.int32, (t, t), 1)

    def score_body(j, carry):
        ik = _key_tile(ik_ref, j)
        sc = jnp.zeros((t, t), F32)
        for h in range(IDX_HEADS):
            sc = sc + jnp.maximum(_nt(ik, iqh_ref[h]), 0.0) * iw_t[h:h + 1, :]
        sc = sc * idx_scale
        bits = pltpu.bitcast(sc, jnp.int32)
        key = bits ^ ((bits >> 31) & 0x7FFFFFFF)
        valid = (i * t + col) >= (j * t + row)
        keys_ref[j] = jnp.where(valid, key, INT_MIN)
        top = jnp.where(valid, (key >> 24) + 128, -1)
        cand_ref[j] = top.astype(F32).astype(BF16)
        for b in range(3):
            byte_ref[b, j] = ((key >> (16 - 8 * b)) & 0xFF).astype(F32).astype(BF16)
        return carry

    lax.fori_loop(0, n_live, score_body, 0)

    one_b, zero_b = jnp.ones((), BF16), jnp.zeros((), BF16)
    packed_rows = 16

    def count(hit_fn):
        def cb(j, acc):
            hit = jnp.where(hit_fn(cand_ref[j]), one_b, zero_b)
            part = hit[0:packed_rows]
            for r in range(1, t // packed_rows):
                part = part + hit[packed_rows * r:packed_rows * (r + 1)]
            return acc + jnp.sum(part.astype(F32), axis=0, keepdims=True)
        return lax.fori_loop(0, n_live, cb, jnp.zeros((1, t), F32))

    need = jnp.full((1, t), topk, F32)
    thr = jnp.zeros((1, t), jnp.int32)
    n_tied = None
    for b in range(4):
        def bit_body(it, val, need=need):
            cand = val | lax.shift_left(jnp.int32(1), 7 - it)
            cand_b = cand.astype(F32).astype(BF16)
            return jnp.where(count(lambda x: x >= cand_b) >= need, cand, val)

        val = lax.fori_loop(0, 8, bit_body, jnp.zeros((1, t), jnp.int32))
        val_b = val.astype(F32).astype(BF16)
        need = need - count(lambda x: x > val_b)
        if b < 3:
            def narrow(j, carry, b=b, val_b=val_b):
                cand_ref[j] = jnp.where(cand_ref[j] == val_b, byte_ref[b, j], -one_b)
                return carry
            lax.fori_loop(0, n_live, narrow, 0)
        else:
            n_tied = count(lambda x: x == val_b)
        piece = val - (128 if b == 0 else 0)
        thr = thr | lax.shift_left(piece, 24 - 8 * b)
    has_rank = thr > INT_MIN
    excess = jnp.where(has_rank & (n_tied > need), 1.0, 0.0)
    thr_ref[...] = thr
    cut_ref[...] = jnp.where(has_rank, jnp.int32(2 * n_tiles * t), jnp.int32(0))

    @pl.when(jnp.max(excess) > 0.0)
    def _():
        n_bits = (2 * n_tiles * t - 1).bit_length()

        def count_ties_before(cand):
            def cb(j, acc):
                hit = (keys_ref[j] == thr) & ((j * t + row) < cand)
                return acc + jnp.sum(jnp.where(hit, 1.0, 0.0), axis=0, keepdims=True)
            return lax.fori_loop(0, n_live, cb, jnp.zeros((1, t), F32))

        def cut_body(it, cut):
            cand = cut | lax.shift_left(jnp.int32(1), n_bits - 1 - it)
            return jnp.where(count_ties_before(cand) <= need, cand, cut)

        cut = lax.fori_loop(0, n_bits, cut_body, jnp.zeros((1, t), jnp.int32))
        cut_ref[...] = jnp.where(has_rank, cut, jnp.int32(0))

    scale = DSA_LATENT ** -0.5 * LOG2E

    def tile_ctx(j):
        key = keys_ref[j]
        thr_v = thr_ref[...]
        sel = (key > thr_v) | ((key == thr_v) & ((j * t + row) < cut_ref[...]))
        return sel, _key_tile(kvn_ref, j), jnp.minimum(i - j, 2)

    def logits(j, h, ctx):
        sel, kvb, kind = ctx
        q_h = q_ref[0, :, h * LANES:(h + 1) * LANES]
        return jnp.where(sel, _nt(kvb, q_h) * scale + bias_ref[h, kind], NEG)

    _attend(n_live, DSA_HEADS, tile_ctx, logits, lambda j, h: kvt_ref[0, j],
            s_ref, m_ref, a_ref, acc_ref)

    y_t = jnp.zeros((DSA_HEADS * DSA_V_DIM, t), F32)
    for h in range(DSA_HEADS):
        o_h = _attend_result(acc_ref, h, DSA_LATENT).astype(BF16)
        y_t = y_t + _nn(wuv_ref[h], o_h)
    o_ref[0] = _nt(_eye(t), y_t.astype(BF16)).astype(o_ref.dtype)


def _dsa_attention(p3, bias_tiles, kv_norm_g, wuv_t_pad):
    bsz, seq, _ = p3.shape
    t = ATT_TILE
    n_tiles = seq // t
    topk = min(DSA_TOPK_MAX, seq // 4)
    kern = functools.partial(_dsa_kernel, n_tiles=n_tiles, topk=float(topk))
    n_out = DSA_HEADS * DSA_V_DIM
    return pl.pallas_call(
        kern,
        grid=(bsz, n_tiles),
        in_specs=[
            pl.BlockSpec((1, t, 4 * LANES), lambda b, i: (b, i, BLK_BQ // 4)),
            pl.BlockSpec((1, seq, LANES), lambda b, i: (b, 0, BLK_BKV)),
            pl.BlockSpec((1, t, 2 * LANES), lambda b, i: (b, i, BLK_BIQ // 2)),
            pl.BlockSpec((1, seq, LANES), lambda b, i: (b, 0, BLK_IK)),
            pl.BlockSpec((1, t, LANES), lambda b, i: (b, i, BLK_IW)),
            pl.BlockSpec((DSA_HEADS, 3, t, t), lambda b, i: (DIFF_HEADS // DSA_HEADS, 0, 0, 0)),
            pl.BlockSpec((1, DSA_LATENT), lambda b, i: (0, 0)),
            pl.BlockSpec((DSA_HEADS, n_out, DSA_LATENT), lambda b, i: (0, 0, 0)),
        ],
        out_specs=pl.BlockSpec((1, t, n_out), lambda b, i: (b, i, 0)),
        out_shape=jax.ShapeDtypeStruct((bsz, seq, n_out), BF16),
        scratch_shapes=[
            pltpu.VMEM((1, seq, DSA_LATENT), BF16),
            pltpu.VMEM((1, n_tiles, DSA_LATENT + SUM_ROWS, t), BF16),
            pltpu.VMEM((n_tiles, t, t), jnp.int32),
            pltpu.VMEM((3, n_tiles, t, t), BF16),
            pltpu.VMEM((n_tiles, t, t), BF16),
            pltpu.VMEM((1, t), jnp.int32),
            pltpu.VMEM((1, t), jnp.int32),
            pltpu.VMEM((IDX_HEADS, t, LANES), BF16),
        ] + _attend_scratch(DSA_HEADS, DSA_LATENT),
        compiler_params=pltpu.CompilerParams(
            dimension_semantics=("parallel", "arbitrary"), vmem_limit_bytes=VMEM_LIMIT),
        name="dsa_attn",
    )(p3, p3, p3, p3, p3, bias_tiles, kv_norm_g.reshape(1, DSA_LATENT), wuv_t_pad)


def _moba_kernel(q_ref, k_ref, v_ref, bias_ref, o_ref,
                 vt_ref, kmean_ref, selb_ref, qs_ref,
                 s_ref, m_ref, a_ref, acc_ref, *, n_tiles, topb):
    t = ATT_TILE
    i = pl.program_id(1)
    pairs = MOBA_HEADS // 2

    @pl.when(i == 0)
    def _():
        _transpose_tiles(v_ref, vt_ref, n_tiles, pairs)
        for n in range(n_tiles):
            kb = k_ref[0, n * t:(n + 1) * t, :].astype(F32)
            kmean_ref[n:n + 1, :] = jnp.mean(kb, axis=0, keepdims=True)

    scale = HEAD_DIM ** -0.5
    blk = lax.broadcasted_iota(jnp.int32, (n_tiles, t), 0)
    own = jnp.full((1, t), i, jnp.int32)
    for hd in range(MOBA_HEADS):
        g, half = hd // 2, hd % 2
        q_f = _lane_group_mask(q_ref[0, :, g * LANES:(g + 1) * LANES], half, HEAD_DIM)
        qs_ref[hd] = (q_f * (scale * LOG2E)).astype(BF16)
        kmean = kmean_ref[:, g * LANES:(g + 1) * LANES].astype(BF16)
        gate = _nt(kmean, q_f.astype(BF16))
        for n in range(n_tiles):
            gn = gate[n:n + 1, :]
            ahead = (gate > gn) | ((gate == gn) & (blk < n))
            ahead = ahead & (blk < i)
            rank = jnp.sum(jnp.where(ahead, 1.0, 0.0), axis=0, keepdims=True)
            chosen = ((rank < topb) & (own > n)) | (own == n)
            selb_ref[hd, n] = jnp.broadcast_to(jnp.where(chosen, 0.0, NEG), (8, t))

    def logits(j, hd, ctx):
        g = hd // 2
        kblk = _key_tile(k_ref, j, slice(g * LANES, (g + 1) * LANES))
        return (_nt(kblk, qs_ref[hd]) + bias_ref[hd, jnp.minimum(i - j, 2)]
                + selb_ref[hd, j][0:1, :])

    _attend(i + 1, MOBA_HEADS, lambda j: None, logits, lambda j, hd: vt_ref[hd // 2, j],
            s_ref, m_ref, a_ref, acc_ref)

    rows = lax.broadcasted_iota(jnp.int32, (LANES, t), 0)
    eye = _eye(t)
    for g in range(pairs):
        o_lo = _attend_result(acc_ref, 2 * g, LANES)
        o_hi = _attend_result(acc_ref, 2 * g + 1, LANES)
        o = jnp.where(rows < HEAD_DIM, o_lo, o_hi)
        o_ref[0, :, g * LANES:(g + 1) * LANES] = _nt(eye, o.astype(BF16)).astype(o_ref.dtype)


def _moba_attention(p3, bias_tiles):
    bsz, seq, _ = p3.shape
    t = ATT_TILE
    n_tiles = seq // t
    topb = min(MOBA_TOPK_MAX, n_tiles)
    kern = functools.partial(_moba_kernel, n_tiles=n_tiles, topb=float(topb))
    w = MOBA_HEADS * HEAD_DIM
    blocks = w // LANES
    first = (DIFF_HEADS + DSA_HEADS) // MOBA_HEADS
    return pl.pallas_call(
        kern,
        grid=(bsz, n_tiles),
        in_specs=[
            pl.BlockSpec((1, t, w), lambda b, i: (b, i, BLK_CQ // blocks)),
            pl.BlockSpec((1, seq, w), lambda b, i: (b, 0, BLK_CK // blocks)),
            pl.BlockSpec((1, seq, w), lambda b, i: (b, 0, BLK_CV // blocks)),
            pl.BlockSpec((MOBA_HEADS, 3, t, t), lambda b, i: (first, 0, 0, 0)),
        ],
        out_specs=pl.BlockSpec((1, t, w), lambda b, i: (b, i, 0)),
        out_shape=jax.ShapeDtypeStruct((bsz, seq, w), BF16),
        scratch_shapes=[
            pltpu.VMEM((blocks, n_tiles, LANES + SUM_ROWS, t), BF16),
            pltpu.VMEM((n_tiles, w), F32),
            pltpu.VMEM((MOBA_HEADS, n_tiles, 8, t), F32),
            pltpu.VMEM((MOBA_HEADS, t, LANES), BF16),
        ] + _attend_scratch(MOBA_HEADS, LANES),
        compiler_params=pltpu.CompilerParams(
            dimension_semantics=("parallel", "arbitrary"), vmem_limit_bytes=VMEM_LIMIT),
        name="moba_attn",
    )(p3, p3, p3, bias_tiles)


def _merge_kernel(x_ref, g_ref, gb_ref, oa_ref, ob_ref, oc_ref, wa_ref, wb_ref, wc_ref, wo_ref,
                  mod_ref, o_ref):
    d = D_MODEL
    gates = jax.nn.sigmoid(g_ref[...].astype(F32) + gb_ref[...])
    merged = (gates[:, 0:d] * _nn(oa_ref[...], wa_ref[...])
              + gates[:, d:2 * d] * _nn(ob_ref[...], wb_ref[...])
              + gates[:, 2 * d:3 * d] * _nn(oc_ref[...], wc_ref[...]))
    z = _nn(merged.astype(BF16), wo_ref[...])
    o_ref[...] = x_ref[...] + mod_ref[0, 2:3, :] * z


def _merge(x2, p2, gate_b, oa, ob, oc, wa, wb, wc, wo, mod_l, seq):
    m, d = x2.shape
    tm = min(512, seq)
    full = lambda a: pl.BlockSpec(a.shape, lambda i: (0, 0))
    return pl.pallas_call(
        _merge_kernel,
        grid=(m // tm,),
        in_specs=[
            pl.BlockSpec((tm, d), lambda i: (i, 0)),
            pl.BlockSpec((tm, 3 * d), lambda i: (i, BLK_G)),
            pl.BlockSpec((1, 3 * d), lambda i: (0, 0)),
            pl.BlockSpec((tm, oa.shape[1]), lambda i: (i, 0)),
            pl.BlockSpec((tm, ob.shape[1]), lambda i: (i, 0)),
            pl.BlockSpec((tm, oc.shape[1]), lambda i: (i, 0)),
            full(wa), full(wb), full(wc), full(wo),
            pl.BlockSpec((1, 6, d), lambda i: ((i * tm) // seq, 0, 0)),
        ],
        out_specs=pl.BlockSpec((tm, d), lambda i: (i, 0)),
        out_shape=jax.ShapeDtypeStruct((m, d), F32),
        compiler_params=pltpu.CompilerParams(
            dimension_semantics=("parallel",), vmem_limit_bytes=VMEM_LIMIT),
        name="merge",
    )(x2, p2, gate_b.reshape(1, 3 * d), oa, ob, oc, wa, wb, wc, wo, mod_l)


def _mlp_kernel(x_ref, mod_ref, g_ref, w1_ref, w2_ref, gf_ref, o_ref, u_ref, acc_ref, *, final):
    f = pl.program_id(1)

    @pl.when(f == 0)
    def _():
        u = _norm_modulate(x_ref[...], g_ref[...], mod_ref[0, 3:4, :], mod_ref[0, 4:5, :])
        u_ref[...] = u.astype(BF16)
        acc_ref[...] = jnp.zeros(acc_ref.shape, F32)

    h = jnp.square(jnp.maximum(_nn(u_ref[...], w1_ref[...]), 0.0))
    acc_ref[...] += _nn(h.astype(BF16), w2_ref[...])

    @pl.when(f == pl.num_programs(1) - 1)
    def _():
        y = x_ref[...] + mod_ref[0, 5:6, :] * acc_ref[...]
        if final:
            ms = jnp.mean(y * y, axis=-1, keepdims=True)
            y = y * lax.rsqrt(ms + EPS) * gf_ref[...]
        o_ref[...] = y


def _mlp(x2, mod_l, g, w1, w2, g_final, seq, final):
    m, d = x2.shape
    dff = w1.shape[1]
    tm = min(1024, seq)
    tf = 1024
    return pl.pallas_call(
        functools.partial(_mlp_kernel, final=final),
        grid=(m // tm, dff // tf),
        in_specs=[
            pl.BlockSpec((tm, d), lambda i, f: (i, 0)),
            pl.BlockSpec((1, 6, d), lambda i, f: ((i * tm) // seq, 0, 0)),
            pl.BlockSpec((1, d), lambda i, f: (0, 0)),
            pl.BlockSpec((d, tf), lambda i, f: (0, f)),
            pl.BlockSpec((tf, d), lambda i, f: (f, 0)),
            pl.BlockSpec((1, d), lambda i, f: (0, 0)),
        ],
        out_specs=pl.BlockSpec((tm, d), lambda i, f: (i, 0)),
        out_shape=jax.ShapeDtypeStruct((m, d), F32),
        scratch_shapes=[pltpu.VMEM((tm, d), BF16), pltpu.VMEM((tm, d), F32)],
        compiler_params=pltpu.CompilerParams(
            dimension_semantics=("parallel", "arbitrary"), vmem_limit_bytes=VMEM_LIMIT),
        name="mlp",
    )(x2, mod_l, g.reshape(1, d), w1, w2, g_final.reshape(1, d))


def kernel(x, c, rel_bias, ada_w, ada_b, norm_mix, w_in, gate_b, diff_lambda, diff_subln,
           dsa_kv_norm, dsa_w_uv, w_br_a, w_br_b, w_br_c, w_o, norm_mlp, w_ff1, w_ff2,
           norm_final):
    bsz, seq, d = x.shape
    depth = w_in.shape[0]
    assert seq % ATT_TILE == 0 and 2 * ATT_TILE > MAX_DISTANCE

    w_in_p = _pack_columns(w_in).astype(BF16)
    n_out = DSA_HEADS * DSA_V_DIM
    wuv_t = jnp.transpose(dsa_w_uv, (0, 1, 3, 2))
    wuv_t_pad = jnp.zeros((depth, DSA_HEADS, n_out, DSA_LATENT), F32)
    for h in range(DSA_HEADS):
        wuv_t_pad = wuv_t_pad.at[:, h, h * DSA_V_DIM:(h + 1) * DSA_V_DIM, :].set(wuv_t[:, h])
    wuv_t_pad = wuv_t_pad.astype(BF16)
    wa, wb, wc, wo = (w.astype(BF16) for w in (w_br_a, w_br_b, w_br_c, w_o))
    w1, w2 = w_ff1.astype(BF16), w_ff2.astype(BF16)

    bias_tiles = _bias_tiles(rel_bias)
    mod = _ada(c, ada_w, ada_b).reshape(depth, bsz, 6, d)

    x2 = x.reshape(bsz * seq, d)
    for l in range(depth):
        lam_init = 0.8 - 0.6 * math.exp(-0.3 * l)
        p2 = _inproj(x2, mod[l], norm_mix[l], w_in_p[l], seq)
        p3 = p2.reshape(bsz, seq, PACKED_COLS)
        oa = _diff_attention(p3, bias_tiles, diff_lambda[l], diff_subln[l], lam_init)
        ob = _dsa_attention(p3, bias_tiles, dsa_kv_norm[l], wuv_t_pad[l])
        oc = _moba_attention(p3, bias_tiles)
        x2 = _merge(x2, p2, gate_b[l],
                    oa.reshape(bsz * seq, -1), ob.reshape(bsz * seq, -1), oc.reshape(bsz * seq, -1),
                    wa[l], wb[l], wc[l], wo[l], mod[l], seq)
        x2 = _mlp(x2, mod[l], norm_mlp[l], w1[l], w2[l], norm_final, seq, final=(l == depth - 1))
    return x2.reshape(bsz, seq, d)
```

```python
import functools
import math

import jax
import jax.numpy as jnp
from jax import lax
from jax.experimental import pallas as pl
from jax.experimental.pallas import tpu as pltpu

F32 = jnp.float32
BF16 = jnp.bfloat16

D_MODEL = 1024
HEAD_DIM = 64
DIFF_HEADS = 4
DIFF_V_DIM = 2 * HEAD_DIM
DSA_HEADS = 4
DSA_LATENT = 128
DSA_V_DIM = 64
IDX_HEADS = 8
IDX_DIM = 32
DSA_TOPK_MAX = 256
MOBA_HEADS = 4
MOBA_BLOCK = 256
MOBA_TOPK_MAX = 3
N_BUCKETS = 32
MAX_DISTANCE = 128
N_BIAS_HEADS = DIFF_HEADS + DSA_HEADS + MOBA_HEADS
D_FF = 4 * D_MODEL
EPS = 1e-6

LANES = 128
ATT_TILE = MOBA_BLOCK
NEG = -1e30
INT_MIN = -2 ** 31
VMEM_LIMIT = 52 * 1024 * 1024
DIFF_HEADS_PER_STEP = 4
SUM_ROWS = 16
LOG2E = math.log2(math.e)

_O_AQ, _O_AK, _O_AV, _O_BQ, _O_BKV, _O_BIQ, _O_BIK, _O_BIW, _O_CQ, _O_CK, _O_CV, _O_G = (
    0, 512, 1024, 1536, 2048, 2176, 2432, 2464, 2472, 2728, 2984, 3240)

BLK_G, BLK_AQ, BLK_AK, BLK_AV, BLK_BQ, BLK_BIQ, BLK_BKV, BLK_IK, BLK_CQ, BLK_CK, BLK_CV, BLK_IW = (
    0, 24, 28, 32, 36, 40, 42, 43, 44, 46, 48, 50)
N_BLKS = 51
PACKED_COLS = N_BLKS * LANES


def _pack_columns(w):
    seg = lambda off, n: w[..., off:off + n]
    ik = seg(_O_BIK, IDX_DIM)
    parts = [
        seg(_O_G, 3 * D_MODEL), seg(_O_AQ, 512), seg(_O_AK, 512), seg(_O_AV, 512),
        seg(_O_BQ, 512), seg(_O_BIQ, 256), seg(_O_BKV, 128),
        ik, ik, ik, ik,
        seg(_O_CQ, 256), seg(_O_CK, 256), seg(_O_CV, 256),
        seg(_O_BIW, IDX_HEADS),
        jnp.zeros(w.shape[:-1] + (LANES - IDX_HEADS,), w.dtype),
    ]
    out = jnp.concatenate(parts, axis=-1)
    assert out.shape[-1] == PACKED_COLS
    return out


def _nt(a, b):
    return lax.dot_general(a, b, (((1,), (1,)), ((), ())), preferred_element_type=F32)


def _nn(a, b):
    return jnp.dot(a, b, preferred_element_type=F32)


def _eye(n):
    r = lax.broadcasted_iota(jnp.int32, (n, n), 0)
    c = lax.broadcasted_iota(jnp.int32, (n, n), 1)
    return jnp.where(r == c, 1.0, 0.0).astype(BF16)


def _transposed(x_f32, eye):
    return _nt(eye, x_f32.astype(BF16)).astype(BF16)


def _lane_group_mask(x_bf16, group, width):
    lane = lax.broadcasted_iota(jnp.int32, x_bf16.shape, 1)
    keep = (lane >= group * width) & (lane < (group + 1) * width)
    return jnp.where(keep, x_bf16.astype(F32), 0.0)


def _ada_kernel(c_ref, w_ref, b_ref, o_ref):
    c = c_ref[...]
    cond = c * jax.nn.sigmoid(c)
    o_ref[0] = _nn(cond, w_ref[0]) + b_ref[0]


def _ada(c, ada_w, ada_b):
    depth, d, n = ada_w.shape
    bsz = c.shape[0]
    tn = 1536
    return pl.pallas_call(
        _ada_kernel,
        grid=(depth, n // tn),
        in_specs=[
            pl.BlockSpec((bsz, d), lambda l, j: (0, 0)),
            pl.BlockSpec((1, d, tn), lambda l, j: (l, 0, j)),
            pl.BlockSpec((1, 1, tn), lambda l, j: (l, 0, j)),
        ],
        out_specs=pl.BlockSpec((1, bsz, tn), lambda l, j: (l, 0, j)),
        out_shape=jax.ShapeDtypeStruct((depth, bsz, n), F32),
        compiler_params=pltpu.CompilerParams(
            dimension_semantics=("arbitrary", "arbitrary"), vmem_limit_bytes=VMEM_LIMIT),
        name="ada_mod",
    )(c, ada_w, ada_b.reshape(depth, 1, n))


def _t5_bucket(dist):
    max_exact = N_BUCKETS // 2
    n = jnp.maximum(dist, 0)
    nf = jnp.maximum(n, 1).astype(F32)
    large = max_exact + (jnp.log(nf / max_exact) / math.log(MAX_DISTANCE / max_exact)
                         * (N_BUCKETS - max_exact)).astype(jnp.int32)
    large = jnp.minimum(large, N_BUCKETS - 1)
    return jnp.where(n < max_exact, n, large)


def _bias_kernel(rb_ref, o_ref):
    t = ATT_TILE
    h = pl.program_id(0)
    kk = lax.broadcasted_iota(jnp.int32, (t, t), 0)
    qq = lax.broadcasted_iota(jnp.int32, (t, t), 1)
    for kind in range(3):
        dist = jnp.full((t, t), 2 * t, jnp.int32) if kind == 2 else kind * t + qq - kk
        bucket = _t5_bucket(dist)
        tile = jnp.zeros((t, t), F32)
        for b in range(N_BUCKETS):
            tile = jnp.where(bucket == b, rb_ref[b, h], tile)
        o_ref[0, kind] = jnp.where(dist >= 0, tile * LOG2E, NEG)


def _bias_tiles(rel_bias):
    t = ATT_TILE
    heads = rel_bias.shape[1]
    return pl.pallas_call(
        _bias_kernel,
        grid=(heads,),
        in_specs=[pl.BlockSpec(memory_space=pltpu.SMEM)],
        out_specs=pl.BlockSpec((1, 3, t, t), lambda h: (h, 0, 0, 0)),
        out_shape=jax.ShapeDtypeStruct((heads, 3, t, t), F32),
        compiler_params=pltpu.CompilerParams(dimension_semantics=("arbitrary",)),
        name="bias_tiles",
    )(rel_bias.astype(F32))


def _norm_modulate(x, g, shift, scale):
    ms = jnp.mean(x * x, axis=-1, keepdims=True)
    y = x * lax.rsqrt(ms + EPS) * g
    return y * (1.0 + scale) + shift


def _inproj_kernel(x_ref, mod_ref, g_ref, w_ref, o_ref, u_ref):
    @pl.when(pl.program_id(1) == 0)
    def _():
        u = _norm_modulate(x_ref[...], g_ref[...], mod_ref[0, 0:1, :], mod_ref[0, 1:2, :])
        u_ref[...] = u.astype(BF16)

    o_ref[...] = _nn(u_ref[...], w_ref[...]).astype(o_ref.dtype)


def _inproj(x2, mod_l, g, w_packed, seq):
    m, d = x2.shape
    n = w_packed.shape[1]
    tm = min(1024, seq)
    tn = n // 3
    return pl.pallas_call(
        _inproj_kernel,
        grid=(m // tm, n // tn),
        in_specs=[
            pl.BlockSpec((tm, d), lambda i, j: (i, 0)),
            pl.BlockSpec((1, 6, d), lambda i, j: ((i * tm) // seq, 0, 0)),
            pl.BlockSpec((1, d), lambda i, j: (0, 0)),
            pl.BlockSpec((d, tn), lambda i, j: (0, j)),
        ],
        out_specs=pl.BlockSpec((tm, tn), lambda i, j: (i, j)),
        out_shape=jax.ShapeDtypeStruct((m, n), BF16),
        scratch_shapes=[pltpu.VMEM((tm, d), BF16)],
        compiler_params=pltpu.CompilerParams(
            dimension_semantics=("parallel", "arbitrary"), vmem_limit_bytes=VMEM_LIMIT),
        name="in_proj",
    )(x2, mod_l, g.reshape(1, d), w_packed)


def _attend(n_live, chains, tile_ctx, logits, values, s_ref, m_ref, a_ref, acc_ref):
    m_ref[...] = jnp.full(m_ref.shape, NEG, F32)
    acc_ref[...] = jnp.zeros(acc_ref.shape, F32)

    def start_tile(j):
        ctx = tile_ctx(j)
        for c in range(chains):
            s = logits(j, c, ctx)
            m_old = m_ref[c]
            m_new = jnp.maximum(m_old, jnp.max(s, axis=0, keepdims=True))
            s_ref[c] = s
            a_ref[c] = jnp.exp2(m_old - m_new)
            m_ref[c] = m_new

    def finish_tile(j):
        for c in range(chains):
            p = jnp.exp2(s_ref[c] - m_ref[c])
            acc_ref[c] = a_ref[c] * acc_ref[c] + _nn(values(j, c), p.astype(BF16))

    start_tile(0)

    def body(j, carry):
        finish_tile(j - 1)
        start_tile(j)
        return carry

    lax.fori_loop(1, n_live, body, 0)
    finish_tile(n_live - 1)


def _attend_scratch(chains, dv):
    t = ATT_TILE
    return [
        pltpu.VMEM((chains, t, t), F32),
        pltpu.VMEM((chains, 1, t), F32),
        pltpu.VMEM((chains, 1, t), F32),
        pltpu.VMEM((chains, dv + SUM_ROWS, t), F32),
    ]


def _attend_result(acc_ref, c, dv):
    return acc_ref[c, 0:dv, :] / acc_ref[c, dv:dv + 1, :]


def _key_tile(ref, j, cols=None):
    t = ATT_TILE
    rows = slice(j * t, (j + 1) * t) if isinstance(j, int) else pl.ds(pl.multiple_of(j * t, t), t)
    return ref[0, rows, :] if cols is None else ref[0, rows, cols]


def _transpose_tiles(src_ref, dst_ref, n_tiles, groups):
    t = ATT_TILE
    eye = _eye(LANES)
    for g in range(groups):
        for j in range(n_tiles):
            blk = src_ref[0, j * t:(j + 1) * t, g * LANES:(g + 1) * LANES]
            dst_ref[g, j, 0:LANES, :] = _nt(eye, blk).astype(BF16)
            dst_ref[g, j, LANES:LANES + SUM_ROWS, :] = jnp.ones((SUM_ROWS, t), BF16)


def _diff_kernel(q_ref, k_ref, v_ref, bias_ref, dl_ref, g_ref, o_ref,
                 vt_ref, qs_ref, s_ref, m_ref, a_ref, acc_ref, *, lam_init, n_tiles, hp):
    t = ATT_TILE
    i = pl.program_id(2)

    @pl.when(i == 0)
    def _():
        _transpose_tiles(v_ref, vt_ref, n_tiles, hp)

    scale = HEAD_DIM ** -0.5 * LOG2E
    eye_l = _eye(LANES)
    for h in range(hp):
        q = q_ref[0, :, h * LANES:(h + 1) * LANES]
        for half in range(2):
            qs_ref[2 * h + half] = _transposed(_lane_group_mask(q, half, HEAD_DIM) * scale, eye_l)

    def logits(j, c, ctx):
        h = c // 2
        kblk = _key_tile(k_ref, j, slice(h * LANES, (h + 1) * LANES))
        return _nn(kblk, qs_ref[c]) + bias_ref[h, jnp.minimum(i - j, 2)]

    _attend(i + 1, 2 * hp, lambda j: None, logits, lambda j, c: vt_ref[c // 2, j],
            s_ref, m_ref, a_ref, acc_ref)

    dl = dl_ref[...]
    lam = (jnp.exp(jnp.sum(dl[0:1] * dl[1:2], keepdims=True))
           - jnp.exp(jnp.sum(dl[2:3] * dl[3:4], keepdims=True)) + lam_init)
    eye = _eye(t)
    for h in range(hp):
        o = (_attend_result(acc_ref, 2 * h, DIFF_V_DIM)
             - lam * _attend_result(acc_ref, 2 * h + 1, DIFF_V_DIM))
        ms = jnp.mean(o * o, axis=0, keepdims=True)
        o = o * lax.rsqrt(ms + EPS) * g_ref[...] * (1.0 - lam_init)
        o_ref[0, :, h * LANES:(h + 1) * LANES] = _nt(eye, o.astype(BF16)).astype(o_ref.dtype)


def _diff_attention(p3, bias_tiles, diff_lambda_l, subln_g, lam_init):
    bsz, seq, _ = p3.shape
    t = ATT_TILE
    n_tiles = seq // t
    hp = DIFF_HEADS_PER_STEP
    w = hp * LANES
    kern = functools.partial(_diff_kernel, lam_init=lam_init, n_tiles=n_tiles, hp=hp)
    return pl.pallas_call(
        kern,
        grid=(bsz, DIFF_HEADS // hp, n_tiles),
        in_specs=[
            pl.BlockSpec((1, t, w), lambda b, h, i: (b, i, BLK_AQ // hp + h)),
            pl.BlockSpec((1, seq, w), lambda b, h, i: (b, 0, BLK_AK // hp + h)),
            pl.BlockSpec((1, seq, w), lambda b, h, i: (b, 0, BLK_AV // hp + h)),
            pl.BlockSpec((hp, 3, t, t), lambda b, h, i: (h, 0, 0, 0)),
            pl.BlockSpec((4, HEAD_DIM), lambda b, h, i: (0, 0)),
            pl.BlockSpec((DIFF_V_DIM, 1), lambda b, h, i: (0, 0)),
        ],
        out_specs=pl.BlockSpec((1, t, w), lambda b, h, i: (b, i, h)),
        out_shape=jax.ShapeDtypeStruct((bsz, seq, DIFF_HEADS * DIFF_V_DIM), BF16),
        scratch_shapes=[
            pltpu.VMEM((hp, n_tiles, LANES + SUM_ROWS, t), BF16),
            pltpu.VMEM((2 * hp, LANES, t), BF16),
        ] + _attend_scratch(2 * hp, DIFF_V_DIM),
        compiler_params=pltpu.CompilerParams(
            dimension_semantics=("parallel", "parallel", "arbitrary"),
            vmem_limit_bytes=VMEM_LIMIT),
        name="diff_attn",
    )(p3, p3, p3, bias_tiles, diff_lambda_l, subln_g.reshape(DIFF_V_DIM, 1))


def _dsa_kernel(q_ref, kv_ref, iq_ref, ik_ref, iw_ref, bias_ref, g_ref, wuv_ref, o_ref,
                kvn_ref, kvt_ref, keys_ref, byte_ref, cand_ref, thr_ref, cut_ref, iqh_ref, qt_ref,
                s_ref, m_ref, a_ref, acc_ref, *, n_tiles, topk):
    t = ATT_TILE
    i = pl.program_id(1)
    n_live = i + 1

    @pl.when(i == 0)
    def _():
        kv = kv_ref[0].astype(F32)
        ms = jnp.mean(kv * kv, axis=-1, keepdims=True)
        kvn_ref[0] = (kv * lax.rsqrt(ms + EPS) * g_ref[...]).astype(BF16)
        _transpose_tiles(kvn_ref, kvt_ref, n_tiles, 1)

    groups = LANES // IDX_DIM
    eye_l = _eye(LANES)
    for h in range(IDX_HEADS):
        blk = iq_ref[0, :, (h // groups) * LANES:(h // groups + 1) * LANES]
        iqh_ref[h] = _transposed(_lane_group_mask(blk, h % groups, IDX_DIM), eye_l)
    for h in range(DSA_HEADS):
        qt_ref[h] = _nt(eye_l, q_ref[0, :, h * LANES:(h + 1) * LANES]).astype(BF16)
    sel_rows = lax.broadcasted_iota(jnp.int32, (IDX_HEADS, LANES), 0)
    sel_cols = lax.broadcasted_iota(jnp.int32, (IDX_HEADS, LANES), 1)
    pick = jnp.where(sel_rows == sel_cols, 1.0, 0.0).astype(BF16)
    iw_t = _nt(pick, iw_ref[0])
    idx_scale = (IDX_HEADS ** -0.5) * (IDX_DIM ** -0.5)
    row = lax.broadcasted_iota(jnp.int32, (t, t), 0)
    col = lax.broadcasted_iota(jnp.int32, (t, t), 1)

    def score_body(j, carry):
        ik = _key_tile(ik_ref, j)
        sc = jnp.zeros((t, t), F32)
        for h in range(IDX_HEADS):
            sc = sc + jnp.maximum(_nn(ik, iqh_ref[h]), 0.0) * iw_t[h:h + 1, :]
        sc = sc * idx_scale
        bits = pltpu.bitcast(sc, jnp.int32)
        key = bits ^ ((bits >> 31) & 0x7FFFFFFF)
        valid = (i * t + col) >= (j * t + row)
        keys_ref[j] = jnp.where(valid, key, INT_MIN)
        top = jnp.where(valid, (key >> 24) + 128, -1)
        cand_ref[j] = top.astype(F32).astype(BF16)
        for b in range(3):
            byte_ref[b, j] = ((key >> (16 - 8 * b)) & 0xFF).astype(F32).astype(BF16)
        return carry

    lax.fori_loop(0, n_live, score_body, 0)

    one_b, zero_b = jnp.ones((), BF16), jnp.zeros((), BF16)
    packed_rows = 16

    def count(hit_fn):
        def cb(j, acc):
            hit = jnp.where(hit_fn(cand_ref[j]), one_b, zero_b)
            part = hit[0:packed_rows]
            for r in range(1, t // packed_rows):
                part = part + hit[packed_rows * r:packed_rows * (r + 1)]
            return acc + jnp.sum(part.astype(F32), axis=0, keepdims=True)
        return lax.fori_loop(0, n_live, cb, jnp.zeros((1, t), F32))

    need = jnp.full((1, t), topk, F32)
    thr = jnp.zeros((1, t), jnp.int32)
    n_tied = None
    for b in range(4):
        def bit_body(it, val, need=need):
            cand = val | lax.shift_left(jnp.int32(1), 7 - it)
            cand_b = cand.astype(F32).astype(BF16)
            return jnp.where(count(lambda x: x >= cand_b) >= need, cand, val)

        val = lax.fori_loop(0, 8, bit_body, jnp.zeros((1, t), jnp.int32))
        val_b = val.astype(F32).astype(BF16)
        need = need - count(lambda x: x > val_b)
        if b < 3:
            def narrow(j, carry, b=b, val_b=val_b):
                cand_ref[j] = jnp.where(cand_ref[j] == val_b, byte_ref[b, j], -one_b)
                return carry
            lax.fori_loop(0, n_live, narrow, 0)
        else:
            n_tied = count(lambda x: x == val_b)
        piece = val - (128 if b == 0 else 0)
        thr = thr | lax.shift_left(piece, 24 - 8 * b)
    has_rank = thr > INT_MIN
    excess = jnp.where(has_rank & (n_tied > need), 1.0, 0.0)
    thr_ref[...] = thr
    cut_ref[...] = jnp.where(has_rank, jnp.int32(2 * n_tiles * t), jnp.int32(0))

    @pl.when(jnp.max(excess) > 0.0)
    def _():
        n_bits = (2 * n_tiles * t - 1).bit_length()

        def count_ties_before(cand):
            def cb(j, acc):
                hit = (keys_ref[j] == thr) & ((j * t + row) < cand)
                return acc + jnp.sum(jnp.where(hit, 1.0, 0.0), axis=0, keepdims=True)
            return lax.fori_loop(0, n_live, cb, jnp.zeros((1, t), F32))

        def cut_body(it, cut):
            cand = cut | lax.shift_left(jnp.int32(1), n_bits - 1 - it)
            return jnp.where(count_ties_before(cand) <= need, cand, cut)

        cut = lax.fori_loop(0, n_bits, cut_body, jnp.zeros((1, t), jnp.int32))
        cut_ref[...] = jnp.where(has_rank, cut, jnp.int32(0))

    scale = DSA_LATENT ** -0.5 * LOG2E

    def tile_ctx(j):
        key = keys_ref[j]
        thr_v = thr_ref[...]
        sel = (key > thr_v) | ((key == thr_v) & ((j * t + row) < cut_ref[...]))
        return sel, _key_tile(kvn_ref, j), jnp.minimum(i - j, 2)

    def logits(j, h, ctx):
        sel, kvb, kind = ctx
        return jnp.where(sel, _nn(kvb, qt_ref[h]) * scale + bias_ref[h, kind], NEG)

    _attend(n_live, DSA_HEADS, tile_ctx, logits, lambda j, h: kvt_ref[0, j],
            s_ref, m_ref, a_ref, acc_ref)

    y_t = jnp.zeros((DSA_HEADS * DSA_V_DIM, t), F32)
    for h in range(DSA_HEADS):
        o_h = _attend_result(acc_ref, h, DSA_LATENT).astype(BF16)
        y_t = y_t + _nn(wuv_ref[h], o_h)
    o_ref[0] = _nt(_eye(t), y_t.astype(BF16)).astype(o_ref.dtype)


def _dsa_attention(p3, bias_tiles, kv_norm_g, wuv_t_pad):
    bsz, seq, _ = p3.shape
    t = ATT_TILE
    n_tiles = seq // t
    topk = min(DSA_TOPK_MAX, seq // 4)
    kern = functools.partial(_dsa_kernel, n_tiles=n_tiles, topk=float(topk))
    n_out = DSA_HEADS * DSA_V_DIM
    return pl.pallas_call(
        kern,
        grid=(bsz, n_tiles),
        in_specs=[
            pl.BlockSpec((1, t, 4 * LANES), lambda b, i: (b, i, BLK_BQ // 4)),
            pl.BlockSpec((1, seq, LANES), lambda b, i: (b, 0, BLK_BKV)),
            pl.BlockSpec((1, t, 2 * LANES), lambda b, i: (b, i, BLK_BIQ // 2)),
            pl.BlockSpec((1, seq, LANES), lambda b, i: (b, 0, BLK_IK)),
            pl.BlockSpec((1, t, LANES), lambda b, i: (b, i, BLK_IW)),
            pl.BlockSpec((DSA_HEADS, 3, t, t), lambda b, i: (DIFF_HEADS // DSA_HEADS, 0, 0, 0)),
            pl.BlockSpec((1, DSA_LATENT), lambda b, i: (0, 0)),
            pl.BlockSpec((DSA_HEADS, n_out, DSA_LATENT), lambda b, i: (0, 0, 0)),
        ],
        out_specs=pl.BlockSpec((1, t, n_out), lambda b, i: (b, i, 0)),
        out_shape=jax.ShapeDtypeStruct((bsz, seq, n_out), BF16),
        scratch_shapes=[
            pltpu.VMEM((1, seq, DSA_LATENT), BF16),
            pltpu.VMEM((1, n_tiles, DSA_LATENT + SUM_ROWS, t), BF16),
            pltpu.VMEM((n_tiles, t, t), jnp.int32),
            pltpu.VMEM((3, n_tiles, t, t), BF16),
            pltpu.VMEM((n_tiles, t, t), BF16),
            pltpu.VMEM((1, t), jnp.int32),
            pltpu.VMEM((1, t), jnp.int32),
            pltpu.VMEM((IDX_HEADS, LANES, t), BF16),
            pltpu.VMEM((DSA_HEADS, LANES, t), BF16),
        ] + _attend_scratch(DSA_HEADS, DSA_LATENT),
        compiler_params=pltpu.CompilerParams(
            dimension_semantics=("parallel", "arbitrary"), vmem_limit_bytes=VMEM_LIMIT),
        name="dsa_attn",
    )(p3, p3, p3, p3, p3, bias_tiles, kv_norm_g.reshape(1, DSA_LATENT), wuv_t_pad)


def _moba_kernel(q_ref, k_ref, v_ref, bias_ref, o_ref,
                 vt_ref, kmean_ref, selb_ref, qs_ref,
                 s_ref, m_ref, a_ref, acc_ref, *, n_tiles, topb):
    t = ATT_TILE
    i = pl.program_id(1)
    pairs = MOBA_HEADS // 2

    @pl.when(i == 0)
    def _():
        _transpose_tiles(v_ref, vt_ref, n_tiles, pairs)
        for n in range(n_tiles):
            kb = k_ref[0, n * t:(n + 1) * t, :].astype(F32)
            kmean_ref[n:n + 1, :] = jnp.mean(kb, axis=0, keepdims=True)

    scale = HEAD_DIM ** -0.5
    eye_l = _eye(LANES)
    blk = lax.broadcasted_iota(jnp.int32, (n_tiles, t), 0)
    own = jnp.full((1, t), i, jnp.int32)
    for hd in range(MOBA_HEADS):
        g, half = hd // 2, hd % 2
        q_f = _lane_group_mask(q_ref[0, :, g * LANES:(g + 1) * LANES], half, HEAD_DIM)
        qs_ref[hd] = _transposed(q_f * (scale * LOG2E), eye_l)
        kmean = kmean_ref[:, g * LANES:(g + 1) * LANES].astype(BF16)
        gate = _nt(kmean, q_f.astype(BF16))
        for n in range(n_tiles):
            gn = gate[n:n + 1, :]
            ahead = (gate > gn) | ((gate == gn) & (blk < n))
            ahead = ahead & (blk < i)
            rank = jnp.sum(jnp.where(ahead, 1.0, 0.0), axis=0, keepdims=True)
            chosen = ((rank < topb) & (own > n)) | (own == n)
            selb_ref[hd, n] = jnp.broadcast_to(jnp.where(chosen, 0.0, NEG), (8, t))

    def logits(j, hd, ctx):
        g = hd // 2
        kblk = _key_tile(k_ref, j, slice(g * LANES, (g + 1) * LANES))
        return (_nn(kblk, qs_ref[hd]) + bias_ref[hd, jnp.minimum(i - j, 2)]
                + selb_ref[hd, j][0:1, :])

    _attend(i + 1, MOBA_HEADS, lambda j: None, logits, lambda j, hd: vt_ref[hd // 2, j],
            s_ref, m_ref, a_ref, acc_ref)

    rows = lax.broadcasted_iota(jnp.int32, (LANES, t), 0)
    eye = _eye(t)
    for g in range(pairs):
        o_lo = _attend_result(acc_ref, 2 * g, LANES)
        o_hi = _attend_result(acc_ref, 2 * g + 1, LANES)
        o = jnp.where(rows < HEAD_DIM, o_lo, o_hi)
        o_ref[0, :, g * LANES:(g + 1) * LANES] = _nt(eye, o.astype(BF16)).astype(o_ref.dtype)


def _moba_attention(p3, bias_tiles):
    bsz, seq, _ = p3.shape
    t = ATT_TILE
    n_tiles = seq // t
    topb = min(MOBA_TOPK_MAX, n_tiles)
    kern = functools.partial(_moba_kernel, n_tiles=n_tiles, topb=float(topb))
    w = MOBA_HEADS * HEAD_DIM
    blocks = w // LANES
    first = (DIFF_HEADS + DSA_HEADS) // MOBA_HEADS
    return pl.pallas_call(
        kern,
        grid=(bsz, n_tiles),
        in_specs=[
            pl.BlockSpec((1, t, w), lambda b, i: (b, i, BLK_CQ // blocks)),
            pl.BlockSpec((1, seq, w), lambda b, i: (b, 0, BLK_CK // blocks)),
            pl.BlockSpec((1, seq, w), lambda b, i: (b, 0, BLK_CV // blocks)),
            pl.BlockSpec((MOBA_HEADS, 3, t, t), lambda b, i: (first, 0, 0, 0)),
        ],
        out_specs=pl.BlockSpec((1, t, w), lambda b, i: (b, i, 0)),
        out_shape=jax.ShapeDtypeStruct((bsz, seq, w), BF16),
        scratch_shapes=[
            pltpu.VMEM((blocks, n_tiles, LANES + SUM_ROWS, t), BF16),
            pltpu.VMEM((n_tiles, w), F32),
            pltpu.VMEM((MOBA_HEADS, n_tiles, 8, t), F32),
            pltpu.VMEM((MOBA_HEADS, LANES, t), BF16),
        ] + _attend_scratch(MOBA_HEADS, LANES),
        compiler_params=pltpu.CompilerParams(
            dimension_semantics=("parallel", "arbitrary"), vmem_limit_bytes=VMEM_LIMIT),
        name="moba_attn",
    )(p3, p3, p3, bias_tiles)


def _merge_kernel(x_ref, g_ref, gb_ref, oa_ref, ob_ref, oc_ref, wa_ref, wb_ref, wc_ref, wo_ref,
                  mod_ref, o_ref):
    d = D_MODEL
    gates = jax.nn.sigmoid(g_ref[...].astype(F32) + gb_ref[...])
    merged = (gates[:, 0:d] * _nn(oa_ref[...], wa_ref[...])
              + gates[:, d:2 * d] * _nn(ob_ref[...], wb_ref[...])
              + gates[:, 2 * d:3 * d] * _nn(oc_ref[...], wc_ref[...]))
    z = _nn(merged.astype(BF16), wo_ref[...])
    o_ref[...] = x_ref[...] + mod_ref[0, 2:3, :] * z


def _merge(x2, p2, gate_b, oa, ob, oc, wa, wb, wc, wo, mod_l, seq):
    m, d = x2.shape
    tm = min(512, seq)
    full = lambda a: pl.BlockSpec(a.shape, lambda i: (0, 0))
    return pl.pallas_call(
        _merge_kernel,
        grid=(m // tm,),
        in_specs=[
            pl.BlockSpec((tm, d), lambda i: (i, 0)),
            pl.BlockSpec((tm, 3 * d), lambda i: (i, BLK_G)),
            pl.BlockSpec((1, 3 * d), lambda i: (0, 0)),
            pl.BlockSpec((tm, oa.shape[1]), lambda i: (i, 0)),
            pl.BlockSpec((tm, ob.shape[1]), lambda i: (i, 0)),
            pl.BlockSpec((tm, oc.shape[1]), lambda i: (i, 0)),
            full(wa), full(wb), full(wc), full(wo),
            pl.BlockSpec((1, 6, d), lambda i: ((i * tm) // seq, 0, 0)),
        ],
        out_specs=pl.BlockSpec((tm, d), lambda i: (i, 0)),
        out_shape=jax.ShapeDtypeStruct((m, d), F32),
        compiler_params=pltpu.CompilerParams(
            dimension_semantics=("parallel",), vmem_limit_bytes=VMEM_LIMIT),
        name="merge",
    )(x2, p2, gate_b.reshape(1, 3 * d), oa, ob, oc, wa, wb, wc, wo, mod_l)


def _mlp_kernel(x_ref, mod_ref, g_ref, w1_ref, w2_ref, gf_ref, o_ref, u_ref, acc_ref, *, final):
    f = pl.program_id(1)

    @pl.when(f == 0)
    def _():
        u = _norm_modulate(x_ref[...], g_ref[...], mod_ref[0, 3:4, :], mod_ref[0, 4:5, :])
        u_ref[...] = u.astype(BF16)
        acc_ref[...] = jnp.zeros(acc_ref.shape, F32)

    h = jnp.square(jnp.maximum(_nn(u_ref[...], w1_ref[...]), 0.0))
    acc_ref[...] += _nn(h.astype(BF16), w2_ref[...])

    @pl.when(f == pl.num_programs(1) - 1)
    def _():
        y = x_ref[...] + mod_ref[0, 5:6, :] * acc_ref[...]
        if final:
            ms = jnp.mean(y * y, axis=-1, keepdims=True)
            y = y * lax.rsqrt(ms + EPS) * gf_ref[...]
        o_ref[...] = y


def _mlp(x2, mod_l, g, w1, w2, g_final, seq, final):
    m, d = x2.shape
    dff = w1.shape[1]
    tm = min(1024, seq)
    tf = 1024
    return pl.pallas_call(
        functools.partial(_mlp_kernel, final=final),
        grid=(m // tm, dff // tf),
        in_specs=[
            pl.BlockSpec((tm, d), lambda i, f: (i, 0)),
            pl.BlockSpec((1, 6, d), lambda i, f: ((i * tm) // seq, 0, 0)),
            pl.BlockSpec((1, d), lambda i, f: (0, 0)),
            pl.BlockSpec((d, tf), lambda i, f: (0, f)),
            pl.BlockSpec((tf, d), lambda i, f: (f, 0)),
            pl.BlockSpec((1, d), lambda i, f: (0, 0)),
        ],
        out_specs=pl.BlockSpec((tm, d), lambda i, f: (i, 0)),
        out_shape=jax.ShapeDtypeStruct((m, d), F32),
        scratch_shapes=[pltpu.VMEM((tm, d), BF16), pltpu.VMEM((tm, d), F32)],
        compiler_params=pltpu.CompilerParams(
            dimension_semantics=("parallel", "arbitrary"), vmem_limit_bytes=VMEM_LIMIT),
        name="mlp",
    )(x2, mod_l, g.reshape(1, d), w1, w2, g_final.reshape(1, d))


def kernel(x, c, rel_bias, ada_w, ada_b, norm_mix, w_in, gate_b, diff_lambda, diff_subln,
           dsa_kv_norm, dsa_w_uv, w_br_a, w_br_b, w_br_c, w_o, norm_mlp, w_ff1, w_ff2,
           norm_final):
    bsz, seq, d = x.shape
    depth = w_in.shape[0]
    assert seq % ATT_TILE == 0 and 2 * ATT_TILE > MAX_DISTANCE

    w_in_p = _pack_columns(w_in).astype(BF16)
    n_out = DSA_HEADS * DSA_V_DIM
    wuv_t = jnp.transpose(dsa_w_uv, (0, 1, 3, 2))
    wuv_t_pad = jnp.zeros((depth, DSA_HEADS, n_out, DSA_LATENT), F32)
    for h in range(DSA_HEADS):
        wuv_t_pad = wuv_t_pad.at[:, h, h * DSA_V_DIM:(h + 1) * DSA_V_DIM, :].set(wuv_t[:, h])
    wuv_t_pad = wuv_t_pad.astype(BF16)
    wa, wb, wc, wo = (w.astype(BF16) for w in (w_br_a, w_br_b, w_br_c, w_o))
    w1, w2 = w_ff1.astype(BF16), w_ff2.astype(BF16)

    bias_tiles = _bias_tiles(rel_bias)
    mod = _ada(c, ada_w, ada_b).reshape(depth, bsz, 6, d)

    x2 = x.reshape(bsz * seq, d)
    for l in range(depth):
        lam_init = 0.8 - 0.6 * math.exp(-0.3 * l)
        p2 = _inproj(x2, mod[l], norm_mix[l], w_in_p[l], seq)
        p3 = p2.reshape(bsz, seq, PACKED_COLS)
        oa = _diff_attention(p3, bias_tiles, diff_lambda[l], diff_subln[l], lam_init)
        ob = _dsa_attention(p3, bias_tiles, dsa_kv_norm[l], wuv_t_pad[l])
        oc = _moba_attention(p3, bias_tiles)
        x2 = _merge(x2, p2, gate_b[l],
                    oa.reshape(bsz * seq, -1), ob.reshape(bsz * seq, -1), oc.reshape(bsz * seq, -1),
                    wa[l], wb[l], wc[l], wo[l], mod[l], seq)
        x2 = _mlp(x2, mod[l], norm_mlp[l], w1[l], w2[l], norm_final, seq, final=(l == depth - 1))
    return x2.reshape(bsz, seq, d)
```

```python
import functools
import math

import jax
import jax.numpy as jnp
from jax import lax
from jax.experimental import pallas as pl
from jax.experimental.pallas import tpu as pltpu

F32 = jnp.float32
BF16 = jnp.bfloat16

D_MODEL = 1024
HEAD_DIM = 64
DIFF_HEADS = 4
DIFF_V_DIM = 2 * HEAD_DIM
DSA_HEADS = 4
DSA_LATENT = 128
DSA_V_DIM = 64
IDX_HEADS = 8
IDX_DIM = 32
DSA_TOPK_MAX = 256
MOBA_HEADS = 4
MOBA_BLOCK = 256
MOBA_TOPK_MAX = 3
N_BUCKETS = 32
MAX_DISTANCE = 128
N_BIAS_HEADS = DIFF_HEADS + DSA_HEADS + MOBA_HEADS
D_FF = 4 * D_MODEL
EPS = 1e-6

LANES = 128
ATT_TILE = MOBA_BLOCK
NEG = -1e30
INT_MIN = -2 ** 31
VMEM_LIMIT = 52 * 1024 * 1024
DIFF_HEADS_PER_STEP = 4
SUM_ROWS = 16
LOG2E = math.log2(math.e)

_O_AQ, _O_AK, _O_AV, _O_BQ, _O_BKV, _O_BIQ, _O_BIK, _O_BIW, _O_CQ, _O_CK, _O_CV, _O_G = (
    0, 512, 1024, 1536, 2048, 2176, 2432, 2464, 2472, 2728, 2984, 3240)

BLK_G, BLK_AQ, BLK_AK, BLK_AV, BLK_BQ, BLK_BIQ, BLK_BKV, BLK_IK, BLK_CQ, BLK_CK, BLK_CV, BLK_IW = (
    0, 24, 28, 32, 36, 40, 42, 43, 44, 46, 48, 50)
N_BLKS = 51
PACKED_COLS = N_BLKS * LANES


def _pack_columns(w):
    seg = lambda off, n: w[..., off:off + n]
    ik = seg(_O_BIK, IDX_DIM)
    parts = [
        seg(_O_G, 3 * D_MODEL), seg(_O_AQ, 512), seg(_O_AK, 512), seg(_O_AV, 512),
        seg(_O_BQ, 512), seg(_O_BIQ, 256), seg(_O_BKV, 128),
        ik, ik, ik, ik,
        seg(_O_CQ, 256), seg(_O_CK, 256), seg(_O_CV, 256),
        seg(_O_BIW, IDX_HEADS),
        jnp.zeros(w.shape[:-1] + (LANES - IDX_HEADS,), w.dtype),
    ]
    out = jnp.concatenate(parts, axis=-1)
    assert out.shape[-1] == PACKED_COLS
    return out


def _nt(a, b):
    return lax.dot_general(a, b, (((1,), (1,)), ((), ())), preferred_element_type=F32)


def _nn(a, b):
    return jnp.dot(a, b, preferred_element_type=F32)


def _eye(n):
    r = lax.broadcasted_iota(jnp.int32, (n, n), 0)
    c = lax.broadcasted_iota(jnp.int32, (n, n), 1)
    return jnp.where(r == c, 1.0, 0.0).astype(BF16)


def _transposed(x_f32, eye):
    return _nt(eye, x_f32.astype(BF16)).astype(BF16)


def _lane_group_mask(x_bf16, group, width):
    lane = lax.broadcasted_iota(jnp.int32, x_bf16.shape, 1)
    keep = (lane >= group * width) & (lane < (group + 1) * width)
    return jnp.where(keep, x_bf16.astype(F32), 0.0)


def _ada_kernel(c_ref, w_ref, b_ref, o_ref):
    c = c_ref[...]
    cond = c * jax.nn.sigmoid(c)
    o_ref[0] = _nn(cond, w_ref[0]) + b_ref[0]


def _ada(c, ada_w, ada_b):
    depth, d, n = ada_w.shape
    bsz = c.shape[0]
    tn = 1536
    return pl.pallas_call(
        _ada_kernel,
        grid=(depth, n // tn),
        in_specs=[
            pl.BlockSpec((bsz, d), lambda l, j: (0, 0)),
            pl.BlockSpec((1, d, tn), lambda l, j: (l, 0, j)),
            pl.BlockSpec((1, 1, tn), lambda l, j: (l, 0, j)),
        ],
        out_specs=pl.BlockSpec((1, bsz, tn), lambda l, j: (l, 0, j)),
        out_shape=jax.ShapeDtypeStruct((depth, bsz, n), F32),
        compiler_params=pltpu.CompilerParams(
            dimension_semantics=("arbitrary", "arbitrary"), vmem_limit_bytes=VMEM_LIMIT),
        name="ada_mod",
    )(c, ada_w, ada_b.reshape(depth, 1, n))


def _t5_bucket(dist):
    max_exact = N_BUCKETS // 2
    n = jnp.maximum(dist, 0)
    nf = jnp.maximum(n, 1).astype(F32)
    large = max_exact + (jnp.log(nf / max_exact) / math.log(MAX_DISTANCE / max_exact)
                         * (N_BUCKETS - max_exact)).astype(jnp.int32)
    large = jnp.minimum(large, N_BUCKETS - 1)
    return jnp.where(n < max_exact, n, large)


def _bias_kernel(rb_ref, o_ref):
    t = ATT_TILE
    h = pl.program_id(0)
    kk = lax.broadcasted_iota(jnp.int32, (t, t), 0)
    qq = lax.broadcasted_iota(jnp.int32, (t, t), 1)
    for kind in range(3):
        dist = jnp.full((t, t), 2 * t, jnp.int32) if kind == 2 else kind * t + qq - kk
        bucket = _t5_bucket(dist)
        tile = jnp.zeros((t, t), F32)
        for b in range(N_BUCKETS):
            tile = jnp.where(bucket == b, rb_ref[b, h], tile)
        o_ref[0, kind] = jnp.where(dist >= 0, tile * LOG2E, NEG)


def _bias_tiles(rel_bias):
    t = ATT_TILE
    heads = rel_bias.shape[1]
    return pl.pallas_call(
        _bias_kernel,
        grid=(heads,),
        in_specs=[pl.BlockSpec(memory_space=pltpu.SMEM)],
        out_specs=pl.BlockSpec((1, 3, t, t), lambda h: (h, 0, 0, 0)),
        out_shape=jax.ShapeDtypeStruct((heads, 3, t, t), F32),
        compiler_params=pltpu.CompilerParams(dimension_semantics=("arbitrary",)),
        name="bias_tiles",
    )(rel_bias.astype(F32))


def _norm_modulate(x, g, shift, scale):
    ms = jnp.mean(x * x, axis=-1, keepdims=True)
    y = x * lax.rsqrt(ms + EPS) * g
    return y * (1.0 + scale) + shift


def _inproj_kernel(x_ref, mod_ref, g_ref, w_ref, o_ref, u_ref):
    @pl.when(pl.program_id(1) == 0)
    def _():
        u = _norm_modulate(x_ref[...], g_ref[...], mod_ref[0, 0:1, :], mod_ref[0, 1:2, :])
        u_ref[...] = u.astype(BF16)

    o_ref[...] = _nn(u_ref[...], w_ref[...]).astype(o_ref.dtype)


def _inproj(x2, mod_l, g, w_packed, seq):
    m, d = x2.shape
    n = w_packed.shape[1]
    tm = min(1024, seq)
    tn = n // 3
    return pl.pallas_call(
        _inproj_kernel,
        grid=(m // tm, n // tn),
        in_specs=[
            pl.BlockSpec((tm, d), lambda i, j: (i, 0)),
            pl.BlockSpec((1, 6, d), lambda i, j: ((i * tm) // seq, 0, 0)),
            pl.BlockSpec((1, d), lambda i, j: (0, 0)),
            pl.BlockSpec((d, tn), lambda i, j: (0, j)),
        ],
        out_specs=pl.BlockSpec((tm, tn), lambda i, j: (i, j)),
        out_shape=jax.ShapeDtypeStruct((m, n), BF16),
        scratch_shapes=[pltpu.VMEM((tm, d), BF16)],
        compiler_params=pltpu.CompilerParams(
            dimension_semantics=("parallel", "arbitrary"), vmem_limit_bytes=VMEM_LIMIT),
        name="in_proj",
    )(x2, mod_l, g.reshape(1, d), w_packed)


def _attend(n_live, chains, tile_ctx, logits, values, s_ref, m_ref, a_ref, acc_ref):
    m_ref[...] = jnp.full(m_ref.shape, NEG, F32)
    acc_ref[...] = jnp.zeros(acc_ref.shape, F32)

    def start_tile(j):
        ctx = tile_ctx(j)
        for c in range(chains):
            s = logits(j, c, ctx)
            m_old = m_ref[c]
            m_new = jnp.maximum(m_old, jnp.max(s, axis=0, keepdims=True))
            s_ref[c] = s
            a_ref[c] = jnp.exp2(m_old - m_new)
            m_ref[c] = m_new

    def finish_tile(j):
        for c in range(chains):
            p = jnp.exp2(s_ref[c] - m_ref[c])
            acc_ref[c] = a_ref[c] * acc_ref[c] + _nn(values(j, c), p.astype(BF16))

    start_tile(0)

    def body(j, carry):
        finish_tile(j - 1)
        start_tile(j)
        return carry

    lax.fori_loop(1, n_live, body, 0)
    finish_tile(n_live - 1)


def _attend_scratch(chains, dv):
    t = ATT_TILE
    return [
        pltpu.VMEM((chains, t, t), F32),
        pltpu.VMEM((chains, 1, t), F32),
        pltpu.VMEM((chains, 1, t), F32),
        pltpu.VMEM((chains, dv + SUM_ROWS, t), F32),
    ]


def _attend_result(acc_ref, c, dv):
    return acc_ref[c, 0:dv, :] / acc_ref[c, dv:dv + 1, :]


def _key_tile(ref, j, cols=None):
    t = ATT_TILE
    rows = slice(j * t, (j + 1) * t) if isinstance(j, int) else pl.ds(pl.multiple_of(j * t, t), t)
    return ref[0, rows, :] if cols is None else ref[0, rows, cols]


def _transpose_tiles(src_ref, dst_ref, n_tiles, groups):
    t = ATT_TILE
    eye = _eye(LANES)
    for g in range(groups):
        for j in range(n_tiles):
            blk = src_ref[0, j * t:(j + 1) * t, g * LANES:(g + 1) * LANES]
            dst_ref[g, j, 0:LANES, :] = _nt(eye, blk).astype(BF16)
            dst_ref[g, j, LANES:LANES + SUM_ROWS, :] = jnp.ones((SUM_ROWS, t), BF16)


def _diff_kernel(q_ref, k_ref, v_ref, bias_ref, dl_ref, g_ref, o_ref,
                 vt_ref, qs_ref, s_ref, m_ref, a_ref, acc_ref, *, lam_init, n_tiles, hp):
    t = ATT_TILE
    i = pl.program_id(2)

    @pl.when(i == 0)
    def _():
        _transpose_tiles(v_ref, vt_ref, n_tiles, hp)

    scale = HEAD_DIM ** -0.5 * LOG2E
    eye_l = _eye(LANES)
    for h in range(hp):
        q = q_ref[0, :, h * LANES:(h + 1) * LANES]
        for half in range(2):
            qs_ref[2 * h + half] = _transposed(_lane_group_mask(q, half, HEAD_DIM) * scale, eye_l)

    def logits(j, c, ctx):
        h = c // 2
        kblk = _key_tile(k_ref, j, slice(h * LANES, (h + 1) * LANES))
        return _nn(kblk, qs_ref[c]) + bias_ref[h, jnp.minimum(i - j, 2)]

    _attend(i + 1, 2 * hp, lambda j: None, logits, lambda j, c: vt_ref[c // 2, j],
            s_ref, m_ref, a_ref, acc_ref)

    dl = dl_ref[...]
    lam = (jnp.exp(jnp.sum(dl[0:1] * dl[1:2], keepdims=True))
           - jnp.exp(jnp.sum(dl[2:3] * dl[3:4], keepdims=True)) + lam_init)
    eye = _eye(t)
    for h in range(hp):
        o = (_attend_result(acc_ref, 2 * h, DIFF_V_DIM)
             - lam * _attend_result(acc_ref, 2 * h + 1, DIFF_V_DIM))
        ms = jnp.mean(o * o, axis=0, keepdims=True)
        o = o * lax.rsqrt(ms + EPS) * g_ref[...] * (1.0 - lam_init)
        o_ref[0, :, h * LANES:(h + 1) * LANES] = _nt(eye, o.astype(BF16)).astype(o_ref.dtype)


def _diff_attention(p3, bias_tiles, diff_lambda_l, subln_g, lam_init):
    bsz, seq, _ = p3.shape
    t = ATT_TILE
    n_tiles = seq // t
    hp = DIFF_HEADS_PER_STEP
    w = hp * LANES
    kern = functools.partial(_diff_kernel, lam_init=lam_init, n_tiles=n_tiles, hp=hp)
    return pl.pallas_call(
        kern,
        grid=(bsz, DIFF_HEADS // hp, n_tiles),
        in_specs=[
            pl.BlockSpec((1, t, w), lambda b, h, i: (b, i, BLK_AQ // hp + h)),
            pl.BlockSpec((1, seq, w), lambda b, h, i: (b, 0, BLK_AK // hp + h)),
            pl.BlockSpec((1, seq, w), lambda b, h, i: (b, 0, BLK_AV // hp + h)),
            pl.BlockSpec((hp, 3, t, t), lambda b, h, i: (h, 0, 0, 0)),
            pl.BlockSpec((4, HEAD_DIM), lambda b, h, i: (0, 0)),
            pl.BlockSpec((DIFF_V_DIM, 1), lambda b, h, i: (0, 0)),
        ],
        out_specs=pl.BlockSpec((1, t, w), lambda b, h, i: (b, i, h)),
        out_shape=jax.ShapeDtypeStruct((bsz, seq, DIFF_HEADS * DIFF_V_DIM), BF16),
        scratch_shapes=[
            pltpu.VMEM((hp, n_tiles, LANES + SUM_ROWS, t), BF16),
            pltpu.VMEM((2 * hp, LANES, t), BF16),
        ] + _attend_scratch(2 * hp, DIFF_V_DIM),
        compiler_params=pltpu.CompilerParams(
            dimension_semantics=("parallel", "parallel", "arbitrary"),
            vmem_limit_bytes=VMEM_LIMIT),
        name="diff_attn",
    )(p3, p3, p3, bias_tiles, diff_lambda_l, subln_g.reshape(DIFF_V_DIM, 1))


def _dsa_kernel(q_ref, kv_ref, iq_ref, ik_ref, iw_ref, bias_ref, g_ref, wuv_ref, o_ref,
                kvn_ref, kvt_ref, keys_ref, byte_ref, cand_ref, thr_ref, cut_ref, need_ref, tied_ref,
                iqh_ref, qt_ref,
                s_ref, m_ref, a_ref, acc_ref, *, n_tiles, topk):
    t = ATT_TILE
    i = pl.program_id(1)
    n_live = i + 1

    @pl.when(i == 0)
    def _():
        kv = kv_ref[0].astype(F32)
        ms = jnp.mean(kv * kv, axis=-1, keepdims=True)
        kvn_ref[0] = (kv * lax.rsqrt(ms + EPS) * g_ref[...]).astype(BF16)
        _transpose_tiles(kvn_ref, kvt_ref, n_tiles, 1)

    groups = LANES // IDX_DIM
    eye_l = _eye(LANES)
    for h in range(IDX_HEADS):
        blk = iq_ref[0, :, (h // groups) * LANES:(h // groups + 1) * LANES]
        iqh_ref[h] = _transposed(_lane_group_mask(blk, h % groups, IDX_DIM), eye_l)
    for h in range(DSA_HEADS):
        qt_ref[h] = _nt(eye_l, q_ref[0, :, h * LANES:(h + 1) * LANES]).astype(BF16)
    sel_rows = lax.broadcasted_iota(jnp.int32, (IDX_HEADS, LANES), 0)
    sel_cols = lax.broadcasted_iota(jnp.int32, (IDX_HEADS, LANES), 1)
    pick = jnp.where(sel_rows == sel_cols, 1.0, 0.0).astype(BF16)
    iw_t = _nt(pick, iw_ref[0])
    idx_scale = (IDX_HEADS ** -0.5) * (IDX_DIM ** -0.5)
    row = lax.broadcasted_iota(jnp.int32, (t, t), 0)
    col = lax.broadcasted_iota(jnp.int32, (t, t), 1)

    def score_body(j, carry):
        ik = _key_tile(ik_ref, j)
        sc = jnp.zeros((t, t), F32)
        for h in range(IDX_HEADS):
            sc = sc + jnp.maximum(_nn(ik, iqh_ref[h]), 0.0) * iw_t[h:h + 1, :]
        sc = sc * idx_scale
        bits = pltpu.bitcast(sc, jnp.int32)
        key = bits ^ ((bits >> 31) & 0x7FFFFFFF)
        valid = (i * t + col) >= (j * t + row)
        keys_ref[j] = jnp.where(valid, key, INT_MIN)
        top = jnp.where(valid, (key >> 24) + 128, -1)
        cand_ref[j] = top.astype(F32).astype(BF16)
        for b in range(3):
            byte_ref[b, j] = ((key >> (16 - 8 * b)) & 0xFF).astype(F32).astype(BF16)
        return carry

    lax.fori_loop(0, n_live, score_body, 0)

    one_b, zero_b = jnp.ones((), BF16), jnp.zeros((), BF16)
    packed_rows = 16

    def radix_select(live):
        def count(hit_fn):
            part = None
            for j in range(live):
                hit = jnp.where(hit_fn(cand_ref[j]), one_b, zero_b)
                for r in range(t // packed_rows):
                    rows = hit[packed_rows * r:packed_rows * (r + 1)]
                    part = rows if part is None else part + rows
            return jnp.sum(part.astype(F32), axis=0, keepdims=True)

        need = jnp.full((1, t), topk, F32)
        thr = jnp.zeros((1, t), jnp.int32)
        for b in range(4):
            def bit_body(it, val, need=need):
                cand = val | lax.shift_left(jnp.int32(1), 7 - it)
                cand_b = cand.astype(F32).astype(BF16)
                return jnp.where(count(lambda x: x >= cand_b) >= need, cand, val)

            val = lax.fori_loop(0, 8, bit_body, jnp.zeros((1, t), jnp.int32))
            val_b = val.astype(F32).astype(BF16)
            need = need - count(lambda x: x > val_b)
            if b < 3:
                for j in range(live):
                    cand_ref[j] = jnp.where(cand_ref[j] == val_b, byte_ref[b, j], -one_b)
            else:
                tied_ref[...] = count(lambda x: x == val_b)
            piece = val - (128 if b == 0 else 0)
            thr = thr | lax.shift_left(piece, 24 - 8 * b)
        thr_ref[...] = thr
        need_ref[...] = need

    assert (t // packed_rows) * n_tiles <= 256
    for live in range(1, n_tiles + 1):
        pl.when(n_live == live)(functools.partial(radix_select, live))

    thr = thr_ref[...]
    need = need_ref[...]
    has_rank = thr > INT_MIN
    excess = jnp.where(has_rank & (tied_ref[...] > need), 1.0, 0.0)
    cut_ref[...] = jnp.where(has_rank, jnp.int32(2 * n_tiles * t), jnp.int32(0))

    @pl.when(jnp.max(excess) > 0.0)
    def _():
        n_bits = (2 * n_tiles * t - 1).bit_length()

        def count_ties_before(cand):
            def cb(j, acc):
                hit = (keys_ref[j] == thr) & ((j * t + row) < cand)
                return acc + jnp.sum(jnp.where(hit, 1.0, 0.0), axis=0, keepdims=True)
            return lax.fori_loop(0, n_live, cb, jnp.zeros((1, t), F32))

        def cut_body(it, cut):
            cand = cut | lax.shift_left(jnp.int32(1), n_bits - 1 - it)
            return jnp.where(count_ties_before(cand) <= need, cand, cut)

        cut = lax.fori_loop(0, n_bits, cut_body, jnp.zeros((1, t), jnp.int32))
        cut_ref[...] = jnp.where(has_rank, cut, jnp.int32(0))

    scale = DSA_LATENT ** -0.5 * LOG2E

    def tile_ctx(j):
        key = keys_ref[j]
        thr_v = thr_ref[...]
        sel = (key > thr_v) | ((key == thr_v) & ((j * t + row) < cut_ref[...]))
        return sel, _key_tile(kvn_ref, j), jnp.minimum(i - j, 2)

    def logits(j, h, ctx):
        sel, kvb, kind = ctx
        return jnp.where(sel, _nn(kvb, qt_ref[h]) * scale + bias_ref[h, kind], NEG)

    _attend(n_live, DSA_HEADS, tile_ctx, logits, lambda j, h: kvt_ref[0, j],
            s_ref, m_ref, a_ref, acc_ref)

    y_t = jnp.zeros((DSA_HEADS * DSA_V_DIM, t), F32)
    for h in range(DSA_HEADS):
        o_h = _attend_result(acc_ref, h, DSA_LATENT).astype(BF16)
        y_t = y_t + _nn(wuv_ref[h], o_h)
    o_ref[0] = _nt(_eye(t), y_t.astype(BF16)).astype(o_ref.dtype)


def _dsa_attention(p3, bias_tiles, kv_norm_g, wuv_t_pad):
    bsz, seq, _ = p3.shape
    t = ATT_TILE
    n_tiles = seq // t
    topk = min(DSA_TOPK_MAX, seq // 4)
    kern = functools.partial(_dsa_kernel, n_tiles=n_tiles, topk=float(topk))
    n_out = DSA_HEADS * DSA_V_DIM
    return pl.pallas_call(
        kern,
        grid=(bsz, n_tiles),
        in_specs=[
            pl.BlockSpec((1, t, 4 * LANES), lambda b, i: (b, i, BLK_BQ // 4)),
            pl.BlockSpec((1, seq, LANES), lambda b, i: (b, 0, BLK_BKV)),
            pl.BlockSpec((1, t, 2 * LANES), lambda b, i: (b, i, BLK_BIQ // 2)),
            pl.BlockSpec((1, seq, LANES), lambda b, i: (b, 0, BLK_IK)),
            pl.BlockSpec((1, t, LANES), lambda b, i: (b, i, BLK_IW)),
            pl.BlockSpec((DSA_HEADS, 3, t, t), lambda b, i: (DIFF_HEADS // DSA_HEADS, 0, 0, 0)),
            pl.BlockSpec((1, DSA_LATENT), lambda b, i: (0, 0)),
            pl.BlockSpec((DSA_HEADS, n_out, DSA_LATENT), lambda b, i: (0, 0, 0)),
        ],
        out_specs=pl.BlockSpec((1, t, n_out), lambda b, i: (b, i, 0)),
        out_shape=jax.ShapeDtypeStruct((bsz, seq, n_out), BF16),
        scratch_shapes=[
            pltpu.VMEM((1, seq, DSA_LATENT), BF16),
            pltpu.VMEM((1, n_tiles, DSA_LATENT + SUM_ROWS, t), BF16),
            pltpu.VMEM((n_tiles, t, t), jnp.int32),
            pltpu.VMEM((3, n_tiles, t, t), BF16),
            pltpu.VMEM((n_tiles, t, t), BF16),
            pltpu.VMEM((1, t), jnp.int32),
            pltpu.VMEM((1, t), jnp.int32),
            pltpu.VMEM((1, t), F32),
            pltpu.VMEM((1, t), F32),
            pltpu.VMEM((IDX_HEADS, LANES, t), BF16),
            pltpu.VMEM((DSA_HEADS, LANES, t), BF16),
        ] + _attend_scratch(DSA_HEADS, DSA_LATENT),
        compiler_params=pltpu.CompilerParams(
            dimension_semantics=("parallel", "arbitrary"), vmem_limit_bytes=VMEM_LIMIT),
        name="dsa_attn",
    )(p3, p3, p3, p3, p3, bias_tiles, kv_norm_g.reshape(1, DSA_LATENT), wuv_t_pad)


def _moba_kernel(q_ref, k_ref, v_ref, bias_ref, o_ref,
                 vt_ref, kmean_ref, selb_ref, qs_ref,
                 s_ref, m_ref, a_ref, acc_ref, *, n_tiles, topb):
    t = ATT_TILE
    i = pl.program_id(1)
    pairs = MOBA_HEADS // 2

    @pl.when(i == 0)
    def _():
        _transpose_tiles(v_ref, vt_ref, n_tiles, pairs)
        for n in range(n_tiles):
            kb = k_ref[0, n * t:(n + 1) * t, :].astype(F32)
            kmean_ref[n:n + 1, :] = jnp.mean(kb, axis=0, keepdims=True)

    scale = HEAD_DIM ** -0.5
    eye_l = _eye(LANES)
    blk = lax.broadcasted_iota(jnp.int32, (n_tiles, t), 0)
    own = jnp.full((1, t), i, jnp.int32)
    for hd in range(MOBA_HEADS):
        g, half = hd // 2, hd % 2
        q_f = _lane_group_mask(q_ref[0, :, g * LANES:(g + 1) * LANES], half, HEAD_DIM)
        qs_ref[hd] = _transposed(q_f * (scale * LOG2E), eye_l)
        kmean = kmean_ref[:, g * LANES:(g + 1) * LANES].astype(BF16)
        gate = _nt(kmean, q_f.astype(BF16))
        for n in range(n_tiles):
            gn = gate[n:n + 1, :]
            ahead = (gate > gn) | ((gate == gn) & (blk < n))
            ahead = ahead & (blk < i)
            rank = jnp.sum(jnp.where(ahead, 1.0, 0.0), axis=0, keepdims=True)
            chosen = ((rank < topb) & (own > n)) | (own == n)
            selb_ref[hd, n] = jnp.broadcast_to(jnp.where(chosen, 0.0, NEG), (8, t))

    def logits(j, hd, ctx):
        g = hd // 2
        kblk = _key_tile(k_ref, j, slice(g * LANES, (g + 1) * LANES))
        return (_nn(kblk, qs_ref[hd]) + bias_ref[hd, jnp.minimum(i - j, 2)]
                + selb_ref[hd, j][0:1, :])

    _attend(i + 1, MOBA_HEADS, lambda j: None, logits, lambda j, hd: vt_ref[hd // 2, j],
            s_ref, m_ref, a_ref, acc_ref)

    rows = lax.broadcasted_iota(jnp.int32, (LANES, t), 0)
    eye = _eye(t)
    for g in range(pairs):
        o_lo = _attend_result(acc_ref, 2 * g, LANES)
        o_hi = _attend_result(acc_ref, 2 * g + 1, LANES)
        o = jnp.where(rows < HEAD_DIM, o_lo, o_hi)
        o_ref[0, :, g * LANES:(g + 1) * LANES] = _nt(eye, o.astype(BF16)).astype(o_ref.dtype)


def _moba_attention(p3, bias_tiles):
    bsz, seq, _ = p3.shape
    t = ATT_TILE
    n_tiles = seq // t
    topb = min(MOBA_TOPK_MAX, n_tiles)
    kern = functools.partial(_moba_kernel, n_tiles=n_tiles, topb=float(topb))
    w = MOBA_HEADS * HEAD_DIM
    blocks = w // LANES
    first = (DIFF_HEADS + DSA_HEADS) // MOBA_HEADS
    return pl.pallas_call(
        kern,
        grid=(bsz, n_tiles),
        in_specs=[
            pl.BlockSpec((1, t, w), lambda b, i: (b, i, BLK_CQ // blocks)),
            pl.BlockSpec((1, seq, w), lambda b, i: (b, 0, BLK_CK // blocks)),
            pl.BlockSpec((1, seq, w), lambda b, i: (b, 0, BLK_CV // blocks)),
            pl.BlockSpec((MOBA_HEADS, 3, t, t), lambda b, i: (first, 0, 0, 0)),
        ],
        out_specs=pl.BlockSpec((1, t, w), lambda b, i: (b, i, 0)),
        out_shape=jax.ShapeDtypeStruct((bsz, seq, w), BF16),
        scratch_shapes=[
            pltpu.VMEM((blocks, n_tiles, LANES + SUM_ROWS, t), BF16),
            pltpu.VMEM((n_tiles, w), F32),
            pltpu.VMEM((MOBA_HEADS, n_tiles, 8, t), F32),
            pltpu.VMEM((MOBA_HEADS, LANES, t), BF16),
        ] + _attend_scratch(MOBA_HEADS, LANES),
        compiler_params=pltpu.CompilerParams(
            dimension_semantics=("parallel", "arbitrary"), vmem_limit_bytes=VMEM_LIMIT),
        name="moba_attn",
    )(p3, p3, p3, bias_tiles)


def _merge_kernel(x_ref, g_ref, gb_ref, oa_ref, ob_ref, oc_ref, wa_ref, wb_ref, wc_ref, wo_ref,
                  mod_ref, o_ref):
    d = D_MODEL
    gates = jax.nn.sigmoid(g_ref[...].astype(F32) + gb_ref[...])
    merged = (gates[:, 0:d] * _nn(oa_ref[...], wa_ref[...])
              + gates[:, d:2 * d] * _nn(ob_ref[...], wb_ref[...])
              + gates[:, 2 * d:3 * d] * _nn(oc_ref[...], wc_ref[...]))
    z = _nn(merged.astype(BF16), wo_ref[...])
    o_ref[...] = x_ref[...] + mod_ref[0, 2:3, :] * z


def _merge(x2, p2, gate_b, oa, ob, oc, wa, wb, wc, wo, mod_l, seq):
    m, d = x2.shape
    tm = min(512, seq)
    full = lambda a: pl.BlockSpec(a.shape, lambda i: (0, 0))
    return pl.pallas_call(
        _merge_kernel,
        grid=(m // tm,),
        in_specs=[
            pl.BlockSpec((tm, d), lambda i: (i, 0)),
            pl.BlockSpec((tm, 3 * d), lambda i: (i, BLK_G)),
            pl.BlockSpec((1, 3 * d), lambda i: (0, 0)),
            pl.BlockSpec((tm, oa.shape[1]), lambda i: (i, 0)),
            pl.BlockSpec((tm, ob.shape[1]), lambda i: (i, 0)),
            pl.BlockSpec((tm, oc.shape[1]), lambda i: (i, 0)),
            full(wa), full(wb), full(wc), full(wo),
            pl.BlockSpec((1, 6, d), lambda i: ((i * tm) // seq, 0, 0)),
        ],
        out_specs=pl.BlockSpec((tm, d), lambda i: (i, 0)),
        out_shape=jax.ShapeDtypeStruct((m, d), F32),
        compiler_params=pltpu.CompilerParams(
            dimension_semantics=("parallel",), vmem_limit_bytes=VMEM_LIMIT),
        name="merge",
    )(x2, p2, gate_b.reshape(1, 3 * d), oa, ob, oc, wa, wb, wc, wo, mod_l)


def _mlp_kernel(x_ref, mod_ref, g_ref, w1_ref, w2_ref, gf_ref, o_ref, u_ref, acc_ref, *, final):
    f = pl.program_id(1)

    @pl.when(f == 0)
    def _():
        u = _norm_modulate(x_ref[...], g_ref[...], mod_ref[0, 3:4, :], mod_ref[0, 4:5, :])
        u_ref[...] = u.astype(BF16)
        acc_ref[...] = jnp.zeros(acc_ref.shape, F32)

    h = jnp.square(jnp.maximum(_nn(u_ref[...], w1_ref[...]), 0.0))
    acc_ref[...] += _nn(h.astype(BF16), w2_ref[...])

    @pl.when(f == pl.num_programs(1) - 1)
    def _():
        y = x_ref[...] + mod_ref[0, 5:6, :] * acc_ref[...]
        if final:
            ms = jnp.mean(y * y, axis=-1, keepdims=True)
            y = y * lax.rsqrt(ms + EPS) * gf_ref[...]
        o_ref[...] = y


def _mlp(x2, mod_l, g, w1, w2, g_final, seq, final):
    m, d = x2.shape
    dff = w1.shape[1]
    tm = min(1024, seq)
    tf = 1024
    return pl.pallas_call(
        functools.partial(_mlp_kernel, final=final),
        grid=(m // tm, dff // tf),
        in_specs=[
            pl.BlockSpec((tm, d), lambda i, f: (i, 0)),
            pl.BlockSpec((1, 6, d), lambda i, f: ((i * tm) // seq, 0, 0)),
            pl.BlockSpec((1, d), lambda i, f: (0, 0)),
            pl.BlockSpec((d, tf), lambda i, f: (0, f)),
            pl.BlockSpec((tf, d), lambda i, f: (f, 0)),
            pl.BlockSpec((1, d), lambda i, f: (0, 0)),
        ],
        out_specs=pl.BlockSpec((tm, d), lambda i, f: (i, 0)),
        out_shape=jax.ShapeDtypeStruct((m, d), F32),
        scratch_shapes=[pltpu.VMEM((tm, d), BF16), pltpu.VMEM((tm, d), F32)],
        compiler_params=pltpu.CompilerParams(
            dimension_semantics=("parallel", "arbitrary"), vmem_limit_bytes=VMEM_LIMIT),
        name="mlp",
    )(x2, mod_l, g.reshape(1, d), w1, w2, g_final.reshape(1, d))


def kernel(x, c, rel_bias, ada_w, ada_b, norm_mix, w_in, gate_b, diff_lambda, diff_subln,
           dsa_kv_norm, dsa_w_uv, w_br_a, w_br_b, w_br_c, w_o, norm_mlp, w_ff1, w_ff2,
           norm_final):
    bsz, seq, d = x.shape
    depth = w_in.shape[0]
    assert seq % ATT_TILE == 0 and 2 * ATT_TILE > MAX_DISTANCE

    w_in_p = _pack_columns(w_in).astype(BF16)
    n_out = DSA_HEADS * DSA_V_DIM
    wuv_t = jnp.transpose(dsa_w_uv, (0, 1, 3, 2))
    wuv_t_pad = jnp.zeros((depth, DSA_HEADS, n_out, DSA_LATENT), F32)
    for h in range(DSA_HEADS):
        wuv_t_pad = wuv_t_pad.at[:, h, h * DSA_V_DIM:(h + 1) * DSA_V_DIM, :].set(wuv_t[:, h])
    wuv_t_pad = wuv_t_pad.astype(BF16)
    wa, wb, wc, wo = (w.astype(BF16) for w in (w_br_a, w_br_b, w_br_c, w_o))
    w1, w2 = w_ff1.astype(BF16), w_ff2.astype(BF16)

    bias_tiles = _bias_tiles(rel_bias)
    mod = _ada(c, ada_w, ada_b).reshape(depth, bsz, 6, d)

    x2 = x.reshape(bsz * seq, d)
    for l in range(depth):
        lam_init = 0.8 - 0.6 * math.exp(-0.3 * l)
        p2 = _inproj(x2, mod[l], norm_mix[l], w_in_p[l], seq)
        p3 = p2.reshape(bsz, seq, PACKED_COLS)
        oa = _diff_attention(p3, bias_tiles, diff_lambda[l], diff_subln[l], lam_init)
        ob = _dsa_attention(p3, bias_tiles, dsa_kv_norm[l], wuv_t_pad[l])
        oc = _moba_attention(p3, bias_tiles)
        x2 = _merge(x2, p2, gate_b[l],
                    oa.reshape(bsz * seq, -1), ob.reshape(bsz * seq, -1), oc.reshape(bsz * seq, -1),
                    wa[l], wb[l], wc[l], wo[l], mod[l], seq)
        x2 = _mlp(x2, mod[l], norm_mlp[l], w1[l], w2[l], norm_final, seq, final=(l == depth - 1))
    return x2.reshape(bsz, seq, d)
```

```python
import functools
import math

import jax
import jax.numpy as jnp
from jax import lax
from jax.experimental import pallas as pl
from jax.experimental.pallas import tpu as pltpu

F32 = jnp.float32
BF16 = jnp.bfloat16

D_MODEL = 1024
HEAD_DIM = 64
DIFF_HEADS = 4
DIFF_V_DIM = 2 * HEAD_DIM
DSA_HEADS = 4
DSA_LATENT = 128
DSA_V_DIM = 64
IDX_HEADS = 8
IDX_DIM = 32
DSA_TOPK_MAX = 256
MOBA_HEADS = 4
MOBA_BLOCK = 256
MOBA_TOPK_MAX = 3
N_BUCKETS = 32
MAX_DISTANCE = 128
N_BIAS_HEADS = DIFF_HEADS + DSA_HEADS + MOBA_HEADS
D_FF = 4 * D_MODEL
EPS = 1e-6

LANES = 128
ATT_TILE = MOBA_BLOCK
NEG = -1e30
INT_MIN = -2 ** 31
VMEM_LIMIT = 52 * 1024 * 1024
DIFF_HEADS_PER_STEP = 4
SUM_ROWS = 16
LOG2E = math.log2(math.e)

_O_AQ, _O_AK, _O_AV, _O_BQ, _O_BKV, _O_BIQ, _O_BIK, _O_BIW, _O_CQ, _O_CK, _O_CV, _O_G = (
    0, 512, 1024, 1536, 2048, 2176, 2432, 2464, 2472, 2728, 2984, 3240)

BLK_G, BLK_AQ, BLK_AK, BLK_AV, BLK_BQ, BLK_BIQ, BLK_BKV, BLK_IK, BLK_CQ, BLK_CK, BLK_CV, BLK_IW = (
    0, 24, 28, 32, 36, 40, 42, 43, 44, 46, 48, 50)
N_BLKS = 51
PACKED_COLS = N_BLKS * LANES


def _pack_columns(w):
    seg = lambda off, n: w[..., off:off + n]
    ik = seg(_O_BIK, IDX_DIM)
    parts = [
        seg(_O_G, 3 * D_MODEL), seg(_O_AQ, 512), seg(_O_AK, 512), seg(_O_AV, 512),
        seg(_O_BQ, 512), seg(_O_BIQ, 256), seg(_O_BKV, 128),
        ik, ik, ik, ik,
        seg(_O_CQ, 256), seg(_O_CK, 256), seg(_O_CV, 256),
        seg(_O_BIW, IDX_HEADS),
        jnp.zeros(w.shape[:-1] + (LANES - IDX_HEADS,), w.dtype),
    ]
    out = jnp.concatenate(parts, axis=-1)
    assert out.shape[-1] == PACKED_COLS
    return out


def _nt(a, b):
    return lax.dot_general(a, b, (((1,), (1,)), ((), ())), preferred_element_type=F32)


def _nn(a, b):
    return jnp.dot(a, b, preferred_element_type=F32)


def _eye(n):
    r = lax.broadcasted_iota(jnp.int32, (n, n), 0)
    c = lax.broadcasted_iota(jnp.int32, (n, n), 1)
    return jnp.where(r == c, 1.0, 0.0).astype(BF16)


def _transposed(x_f32, eye):
    return _nt(eye, x_f32.astype(BF16)).astype(BF16)


def _lane_group_mask(x_bf16, group, width):
    lane = lax.broadcasted_iota(jnp.int32, x_bf16.shape, 1)
    keep = (lane >= group * width) & (lane < (group + 1) * width)
    return jnp.where(keep, x_bf16.astype(F32), 0.0)


def _ada_kernel(c_ref, w_ref, b_ref, o_ref):
    c = c_ref[...]
    cond = c * jax.nn.sigmoid(c)
    o_ref[0] = _nn(cond, w_ref[0]) + b_ref[0]


def _ada(c, ada_w, ada_b):
    depth, d, n = ada_w.shape
    bsz = c.shape[0]
    tn = 1536
    return pl.pallas_call(
        _ada_kernel,
        grid=(depth, n // tn),
        in_specs=[
            pl.BlockSpec((bsz, d), lambda l, j: (0, 0)),
            pl.BlockSpec((1, d, tn), lambda l, j: (l, 0, j)),
            pl.BlockSpec((1, 1, tn), lambda l, j: (l, 0, j)),
        ],
        out_specs=pl.BlockSpec((1, bsz, tn), lambda l, j: (l, 0, j)),
        out_shape=jax.ShapeDtypeStruct((depth, bsz, n), F32),
        compiler_params=pltpu.CompilerParams(
            dimension_semantics=("arbitrary", "arbitrary"), vmem_limit_bytes=VMEM_LIMIT),
        name="ada_mod",
    )(c, ada_w, ada_b.reshape(depth, 1, n))


def _t5_bucket(dist):
    max_exact = N_BUCKETS // 2
    n = jnp.maximum(dist, 0)
    nf = jnp.maximum(n, 1).astype(F32)
    large = max_exact + (jnp.log(nf / max_exact) / math.log(MAX_DISTANCE / max_exact)
                         * (N_BUCKETS - max_exact)).astype(jnp.int32)
    large = jnp.minimum(large, N_BUCKETS - 1)
    return jnp.where(n < max_exact, n, large)


def _bias_kernel(rb_ref, o_ref):
    t = ATT_TILE
    h = pl.program_id(0)
    kk = lax.broadcasted_iota(jnp.int32, (t, t), 0)
    qq = lax.broadcasted_iota(jnp.int32, (t, t), 1)
    for kind in range(3):
        dist = jnp.full((t, t), 2 * t, jnp.int32) if kind == 2 else kind * t + qq - kk
        bucket = _t5_bucket(dist)
        tile = jnp.zeros((t, t), F32)
        for b in range(N_BUCKETS):
            tile = jnp.where(bucket == b, rb_ref[b, h], tile)
        o_ref[0, kind] = jnp.where(dist >= 0, tile * LOG2E, NEG)


def _bias_tiles(rel_bias):
    t = ATT_TILE
    heads = rel_bias.shape[1]
    return pl.pallas_call(
        _bias_kernel,
        grid=(heads,),
        in_specs=[pl.BlockSpec(memory_space=pltpu.SMEM)],
        out_specs=pl.BlockSpec((1, 3, t, t), lambda h: (h, 0, 0, 0)),
        out_shape=jax.ShapeDtypeStruct((heads, 3, t, t), F32),
        compiler_params=pltpu.CompilerParams(dimension_semantics=("arbitrary",)),
        name="bias_tiles",
    )(rel_bias.astype(F32))


def _norm_modulate(x, g, shift, scale):
    ms = jnp.mean(x * x, axis=-1, keepdims=True)
    y = x * lax.rsqrt(ms + EPS) * g
    return y * (1.0 + scale) + shift


def _inproj_kernel(x_ref, mod_ref, g_ref, w_ref, o_ref, u_ref):
    @pl.when(pl.program_id(1) == 0)
    def _():
        u = _norm_modulate(x_ref[...], g_ref[...], mod_ref[0, 0:1, :], mod_ref[0, 1:2, :])
        u_ref[...] = u.astype(BF16)

    o_ref[...] = _nn(u_ref[...], w_ref[...]).astype(o_ref.dtype)


def _inproj(x2, mod_l, g, w_packed, seq):
    m, d = x2.shape
    n = w_packed.shape[1]
    tm = min(1024, seq)
    tn = n // 3
    return pl.pallas_call(
        _inproj_kernel,
        grid=(m // tm, n // tn),
        in_specs=[
            pl.BlockSpec((tm, d), lambda i, j: (i, 0)),
            pl.BlockSpec((1, 6, d), lambda i, j: ((i * tm) // seq, 0, 0)),
            pl.BlockSpec((1, d), lambda i, j: (0, 0)),
            pl.BlockSpec((d, tn), lambda i, j: (0, j)),
        ],
        out_specs=pl.BlockSpec((tm, tn), lambda i, j: (i, j)),
        out_shape=jax.ShapeDtypeStruct((m, n), BF16),
        scratch_shapes=[pltpu.VMEM((tm, d), BF16)],
        compiler_params=pltpu.CompilerParams(
            dimension_semantics=("parallel", "arbitrary"), vmem_limit_bytes=VMEM_LIMIT),
        name="in_proj",
    )(x2, mod_l, g.reshape(1, d), w_packed)


def _attend(n_live, chains, tile_ctx, logits, values, s_ref, m_ref, a_ref, acc_ref):
    m_ref[...] = jnp.full(m_ref.shape, NEG, F32)
    acc_ref[...] = jnp.zeros(acc_ref.shape, F32)

    def start_chain(j, slot, c, ctx):
        s = logits(j, c, ctx)
        m_old = m_ref[1 - slot, c]
        m_new = jnp.maximum(m_old, jnp.max(s, axis=0, keepdims=True))
        s_ref[slot, c] = s
        a_ref[slot, c] = jnp.exp2(m_old - m_new)
        m_ref[slot, c] = m_new

    def finish_chain(j, slot, c):
        p = jnp.exp2(s_ref[slot, c] - m_ref[slot, c])
        acc_ref[c] = a_ref[slot, c] * acc_ref[c] + _nn(values(j, c), p.astype(BF16))

    ctx = tile_ctx(0)
    for c in range(chains):
        start_chain(0, 0, c, ctx)

    def body(j, carry):
        for slot in range(2):
            @pl.when(j % 2 == slot)
            def _(slot=slot):
                ctx = tile_ctx(j)
                for c in range(chains):
                    start_chain(j, slot, c, ctx)
                    finish_chain(j - 1, 1 - slot, c)
        return carry

    lax.fori_loop(1, n_live, body, 0)
    for slot in range(2):
        @pl.when((n_live - 1) % 2 == slot)
        def _(slot=slot):
            for c in range(chains):
                finish_chain(n_live - 1, slot, c)


def _attend_scratch(chains, dv):
    t = ATT_TILE
    return [
        pltpu.VMEM((2, chains, t, t), F32),
        pltpu.VMEM((2, chains, 1, t), F32),
        pltpu.VMEM((2, chains, 1, t), F32),
        pltpu.VMEM((chains, dv + SUM_ROWS, t), F32),
    ]


def _attend_result(acc_ref, c, dv):
    return acc_ref[c, 0:dv, :] / acc_ref[c, dv:dv + 1, :]


def _key_tile(ref, j, cols=None):
    t = ATT_TILE
    rows = slice(j * t, (j + 1) * t) if isinstance(j, int) else pl.ds(pl.multiple_of(j * t, t), t)
    return ref[0, rows, :] if cols is None else ref[0, rows, cols]


def _transpose_tiles(src_ref, dst_ref, n_tiles, groups):
    t = ATT_TILE
    eye = _eye(LANES)
    for g in range(groups):
        for j in range(n_tiles):
            blk = src_ref[0, j * t:(j + 1) * t, g * LANES:(g + 1) * LANES]
            dst_ref[g, j, 0:LANES, :] = _nt(eye, blk).astype(BF16)
            dst_ref[g, j, LANES:LANES + SUM_ROWS, :] = jnp.ones((SUM_ROWS, t), BF16)


def _diff_kernel(q_ref, k_ref, v_ref, bias_ref, dl_ref, g_ref, o_ref,
                 vt_ref, qs_ref, s_ref, m_ref, a_ref, acc_ref, *, lam_init, n_tiles, hp):
    t = ATT_TILE
    i = pl.program_id(2)

    @pl.when(i == 0)
    def _():
        _transpose_tiles(v_ref, vt_ref, n_tiles, hp)

    scale = HEAD_DIM ** -0.5 * LOG2E
    eye_l = _eye(LANES)
    for h in range(hp):
        q = q_ref[0, :, h * LANES:(h + 1) * LANES]
        for half in range(2):
            qs_ref[2 * h + half] = _transposed(_lane_group_mask(q, half, HEAD_DIM) * scale, eye_l)

    def logits(j, c, ctx):
        h = c // 2
        kblk = _key_tile(k_ref, j, slice(h * LANES, (h + 1) * LANES))
        return _nn(kblk, qs_ref[c]) + bias_ref[h, jnp.minimum(i - j, 2)]

    _attend(i + 1, 2 * hp, lambda j: None, logits, lambda j, c: vt_ref[c // 2, j],
            s_ref, m_ref, a_ref, acc_ref)

    dl = dl_ref[...]
    lam = (jnp.exp(jnp.sum(dl[0:1] * dl[1:2], keepdims=True))
           - jnp.exp(jnp.sum(dl[2:3] * dl[3:4], keepdims=True)) + lam_init)
    eye = _eye(t)
    for h in range(hp):
        o = (_attend_result(acc_ref, 2 * h, DIFF_V_DIM)
             - lam * _attend_result(acc_ref, 2 * h + 1, DIFF_V_DIM))
        ms = jnp.mean(o * o, axis=0, keepdims=True)
        o = o * lax.rsqrt(ms + EPS) * g_ref[...] * (1.0 - lam_init)
        o_ref[0, :, h * LANES:(h + 1) * LANES] = _nt(eye, o.astype(BF16)).astype(o_ref.dtype)


def _diff_attention(p3, bias_tiles, diff_lambda_l, subln_g, lam_init):
    bsz, seq, _ = p3.shape
    t = ATT_TILE
    n_tiles = seq // t
    hp = DIFF_HEADS_PER_STEP
    w = hp * LANES
    kern = functools.partial(_diff_kernel, lam_init=lam_init, n_tiles=n_tiles, hp=hp)
    return pl.pallas_call(
        kern,
        grid=(bsz, DIFF_HEADS // hp, n_tiles),
        in_specs=[
            pl.BlockSpec((1, t, w), lambda b, h, i: (b, i, BLK_AQ // hp + h)),
            pl.BlockSpec((1, seq, w), lambda b, h, i: (b, 0, BLK_AK // hp + h)),
            pl.BlockSpec((1, seq, w), lambda b, h, i: (b, 0, BLK_AV // hp + h)),
            pl.BlockSpec((hp, 3, t, t), lambda b, h, i: (h, 0, 0, 0)),
            pl.BlockSpec((4, HEAD_DIM), lambda b, h, i: (0, 0)),
            pl.BlockSpec((DIFF_V_DIM, 1), lambda b, h, i: (0, 0)),
        ],
        out_specs=pl.BlockSpec((1, t, w), lambda b, h, i: (b, i, h)),
        out_shape=jax.ShapeDtypeStruct((bsz, seq, DIFF_HEADS * DIFF_V_DIM), BF16),
        scratch_shapes=[
            pltpu.VMEM((hp, n_tiles, LANES + SUM_ROWS, t), BF16),
            pltpu.VMEM((2 * hp, LANES, t), BF16),
        ] + _attend_scratch(2 * hp, DIFF_V_DIM),
        compiler_params=pltpu.CompilerParams(
            dimension_semantics=("parallel", "parallel", "arbitrary"),
            vmem_limit_bytes=VMEM_LIMIT),
        name="diff_attn",
    )(p3, p3, p3, bias_tiles, diff_lambda_l, subln_g.reshape(DIFF_V_DIM, 1))


def _dsa_kernel(q_ref, kv_ref, iq_ref, ik_ref, iw_ref, bias_ref, g_ref, wuv_ref, o_ref,
                kvn_ref, kvt_ref, keys_ref, byte_ref, cand_ref, thr_ref, cut_ref, need_ref, tied_ref,
                iqh_ref, qt_ref,
                s_ref, m_ref, a_ref, acc_ref, *, n_tiles, topk):
    t = ATT_TILE
    i = pl.program_id(1)
    n_live = i + 1

    @pl.when(i == 0)
    def _():
        kv = kv_ref[0].astype(F32)
        ms = jnp.mean(kv * kv, axis=-1, keepdims=True)
        kvn_ref[0] = (kv * lax.rsqrt(ms + EPS) * g_ref[...]).astype(BF16)
        _transpose_tiles(kvn_ref, kvt_ref, n_tiles, 1)

    groups = LANES // IDX_DIM
    eye_l = _eye(LANES)
    for h in range(IDX_HEADS):
        blk = iq_ref[0, :, (h // groups) * LANES:(h // groups + 1) * LANES]
        iqh_ref[h] = _transposed(_lane_group_mask(blk, h % groups, IDX_DIM), eye_l)
    for h in range(DSA_HEADS):
        qt_ref[h] = _nt(eye_l, q_ref[0, :, h * LANES:(h + 1) * LANES]).astype(BF16)
    sel_rows = lax.broadcasted_iota(jnp.int32, (IDX_HEADS, LANES), 0)
    sel_cols = lax.broadcasted_iota(jnp.int32, (IDX_HEADS, LANES), 1)
    pick = jnp.where(sel_rows == sel_cols, 1.0, 0.0).astype(BF16)
    iw_t = _nt(pick, iw_ref[0])
    idx_scale = (IDX_HEADS ** -0.5) * (IDX_DIM ** -0.5)
    row = lax.broadcasted_iota(jnp.int32, (t, t), 0)
    col = lax.broadcasted_iota(jnp.int32, (t, t), 1)

    def score_body(j, carry):
        ik = _key_tile(ik_ref, j)
        sc = jnp.zeros((t, t), F32)
        for h in range(IDX_HEADS):
            sc = sc + jnp.maximum(_nn(ik, iqh_ref[h]), 0.0) * iw_t[h:h + 1, :]
        sc = sc * idx_scale
        bits = pltpu.bitcast(sc, jnp.int32)
        key = bits ^ ((bits >> 31) & 0x7FFFFFFF)
        valid = (i * t + col) >= (j * t + row)
        keys_ref[j] = jnp.where(valid, key, INT_MIN)
        top = jnp.where(valid, (key >> 24) + 128, -1)
        cand_ref[j] = top.astype(F32).astype(BF16)
        for b in range(3):
            byte_ref[b, j] = ((key >> (16 - 8 * b)) & 0xFF).astype(F32).astype(BF16)
        return carry

    lax.fori_loop(0, n_live, score_body, 0)

    one_b, zero_b = jnp.ones((), BF16), jnp.zeros((), BF16)
    packed_rows = 16

    def radix_select(live):
        def count(hit_fn):
            part = None
            for j in range(live):
                hit = jnp.where(hit_fn(cand_ref[j]), one_b, zero_b)
                for r in range(t // packed_rows):
                    rows = hit[packed_rows * r:packed_rows * (r + 1)]
                    part = rows if part is None else part + rows
            return jnp.sum(part.astype(F32), axis=0, keepdims=True)

        need = jnp.full((1, t), topk, F32)
        thr = jnp.zeros((1, t), jnp.int32)
        for b in range(4):
            def bit_body(it, val, need=need):
                cand = val | lax.shift_left(jnp.int32(1), 7 - it)
                cand_b = cand.astype(F32).astype(BF16)
                return jnp.where(count(lambda x: x >= cand_b) >= need, cand, val)

            val = lax.fori_loop(0, 8, bit_body, jnp.zeros((1, t), jnp.int32))
            val_b = val.astype(F32).astype(BF16)
            need = need - count(lambda x: x > val_b)
            if b < 3:
                for j in range(live):
                    cand_ref[j] = jnp.where(cand_ref[j] == val_b, byte_ref[b, j], -one_b)
            else:
                tied_ref[...] = count(lambda x: x == val_b)
            piece = val - (128 if b == 0 else 0)
            thr = thr | lax.shift_left(piece, 24 - 8 * b)
        thr_ref[...] = thr
        need_ref[...] = need

    assert (t // packed_rows) * n_tiles <= 256
    for live in range(1, n_tiles + 1):
        pl.when(n_live == live)(functools.partial(radix_select, live))

    thr = thr_ref[...]
    need = need_ref[...]
    has_rank = thr > INT_MIN
    excess = jnp.where(has_rank & (tied_ref[...] > need), 1.0, 0.0)
    cut_ref[...] = jnp.where(has_rank, jnp.int32(2 * n_tiles * t), jnp.int32(0))

    @pl.when(jnp.max(excess) > 0.0)
    def _():
        n_bits = (2 * n_tiles * t - 1).bit_length()

        def count_ties_before(cand):
            def cb(j, acc):
                hit = (keys_ref[j] == thr) & ((j * t + row) < cand)
                return acc + jnp.sum(jnp.where(hit, 1.0, 0.0), axis=0, keepdims=True)
            return lax.fori_loop(0, n_live, cb, jnp.zeros((1, t), F32))

        def cut_body(it, cut):
            cand = cut | lax.shift_left(jnp.int32(1), n_bits - 1 - it)
            return jnp.where(count_ties_before(cand) <= need, cand, cut)

        cut = lax.fori_loop(0, n_bits, cut_body, jnp.zeros((1, t), jnp.int32))
        cut_ref[...] = jnp.where(has_rank, cut, jnp.int32(0))

    scale = DSA_LATENT ** -0.5 * LOG2E

    def tile_ctx(j):
        key = keys_ref[j]
        thr_v = thr_ref[...]
        sel = (key > thr_v) | ((key == thr_v) & ((j * t + row) < cut_ref[...]))
        return sel, _key_tile(kvn_ref, j), jnp.minimum(i - j, 2)

    def logits(j, h, ctx):
        sel, kvb, kind = ctx
        return jnp.where(sel, _nn(kvb, qt_ref[h]) * scale + bias_ref[h, kind], NEG)

    _attend(n_live, DSA_HEADS, tile_ctx, logits, lambda j, h: kvt_ref[0, j],
            s_ref, m_ref, a_ref, acc_ref)

    y_t = jnp.zeros((DSA_HEADS * DSA_V_DIM, t), F32)
    for h in range(DSA_HEADS):
        o_h = _attend_result(acc_ref, h, DSA_LATENT).astype(BF16)
        y_t = y_t + _nn(wuv_ref[h], o_h)
    o_ref[0] = _nt(_eye(t), y_t.astype(BF16)).astype(o_ref.dtype)


def _dsa_attention(p3, bias_tiles, kv_norm_g, wuv_t_pad):
    bsz, seq, _ = p3.shape
    t = ATT_TILE
    n_tiles = seq // t
    topk = min(DSA_TOPK_MAX, seq // 4)
    kern = functools.partial(_dsa_kernel, n_tiles=n_tiles, topk=float(topk))
    n_out = DSA_HEADS * DSA_V_DIM
    return pl.pallas_call(
        kern,
        grid=(bsz, n_tiles),
        in_specs=[
            pl.BlockSpec((1, t, 4 * LANES), lambda b, i: (b, i, BLK_BQ // 4)),
            pl.BlockSpec((1, seq, LANES), lambda b, i: (b, 0, BLK_BKV)),
            pl.BlockSpec((1, t, 2 * LANES), lambda b, i: (b, i, BLK_BIQ // 2)),
            pl.BlockSpec((1, seq, LANES), lambda b, i: (b, 0, BLK_IK)),
            pl.BlockSpec((1, t, LANES), lambda b, i: (b, i, BLK_IW)),
            pl.BlockSpec((DSA_HEADS, 3, t, t), lambda b, i: (DIFF_HEADS // DSA_HEADS, 0, 0, 0)),
            pl.BlockSpec((1, DSA_LATENT), lambda b, i: (0, 0)),
            pl.BlockSpec((DSA_HEADS, n_out, DSA_LATENT), lambda b, i: (0, 0, 0)),
        ],
        out_specs=pl.BlockSpec((1, t, n_out), lambda b, i: (b, i, 0)),
        out_shape=jax.ShapeDtypeStruct((bsz, seq, n_out), BF16),
        scratch_shapes=[
            pltpu.VMEM((1, seq, DSA_LATENT), BF16),
            pltpu.VMEM((1, n_tiles, DSA_LATENT + SUM_ROWS, t), BF16),
            pltpu.VMEM((n_tiles, t, t), jnp.int32),
            pltpu.VMEM((3, n_tiles, t, t), BF16),
            pltpu.VMEM((n_tiles, t, t), BF16),
            pltpu.VMEM((1, t), jnp.int32),
            pltpu.VMEM((1, t), jnp.int32),
            pltpu.VMEM((1, t), F32),
            pltpu.VMEM((1, t), F32),
            pltpu.VMEM((IDX_HEADS, LANES, t), BF16),
            pltpu.VMEM((DSA_HEADS, LANES, t), BF16),
        ] + _attend_scratch(DSA_HEADS, DSA_LATENT),
        compiler_params=pltpu.CompilerParams(
            dimension_semantics=("parallel", "arbitrary"), vmem_limit_bytes=VMEM_LIMIT),
        name="dsa_attn",
    )(p3, p3, p3, p3, p3, bias_tiles, kv_norm_g.reshape(1, DSA_LATENT), wuv_t_pad)


def _moba_kernel(q_ref, k_ref, v_ref, bias_ref, o_ref,
                 vt_ref, kmean_ref, selb_ref, qs_ref,
                 s_ref, m_ref, a_ref, acc_ref, *, n_tiles, topb):
    t = ATT_TILE
    i = pl.program_id(1)
    pairs = MOBA_HEADS // 2

    @pl.when(i == 0)
    def _():
        _transpose_tiles(v_ref, vt_ref, n_tiles, pairs)
        for n in range(n_tiles):
            kb = k_ref[0, n * t:(n + 1) * t, :].astype(F32)
            kmean_ref[n:n + 1, :] = jnp.mean(kb, axis=0, keepdims=True)

    scale = HEAD_DIM ** -0.5
    eye_l = _eye(LANES)
    blk = lax.broadcasted_iota(jnp.int32, (n_tiles, t), 0)
    own = jnp.full((1, t), i, jnp.int32)
    for hd in range(MOBA_HEADS):
        g, half = hd // 2, hd % 2
        q_f = _lane_group_mask(q_ref[0, :, g * LANES:(g + 1) * LANES], half, HEAD_DIM)
        qs_ref[hd] = _transposed(q_f * (scale * LOG2E), eye_l)
        kmean = kmean_ref[:, g * LANES:(g + 1) * LANES].astype(BF16)
        gate = _nt(kmean, q_f.astype(BF16))
        for n in range(n_tiles):
            gn = gate[n:n + 1, :]
            ahead = (gate > gn) | ((gate == gn) & (blk < n))
            ahead = ahead & (blk < i)
            rank = jnp.sum(jnp.where(ahead, 1.0, 0.0), axis=0, keepdims=True)
            chosen = ((rank < topb) & (own > n)) | (own == n)
            selb_ref[hd, n] = jnp.broadcast_to(jnp.where(chosen, 0.0, NEG), (8, t))

    def logits(j, hd, ctx):
        g = hd // 2
        kblk = _key_tile(k_ref, j, slice(g * LANES, (g + 1) * LANES))
        return (_nn(kblk, qs_ref[hd]) + bias_ref[hd, jnp.minimum(i - j, 2)]
                + selb_ref[hd, j][0:1, :])

    _attend(i + 1, MOBA_HEADS, lambda j: None, logits, lambda j, hd: vt_ref[hd // 2, j],
            s_ref, m_ref, a_ref, acc_ref)

    rows = lax.broadcasted_iota(jnp.int32, (LANES, t), 0)
    eye = _eye(t)
    for g in range(pairs):
        o_lo = _attend_result(acc_ref, 2 * g, LANES)
        o_hi = _attend_result(acc_ref, 2 * g + 1, LANES)
        o = jnp.where(rows < HEAD_DIM, o_lo, o_hi)
        o_ref[0, :, g * LANES:(g + 1) * LANES] = _nt(eye, o.astype(BF16)).astype(o_ref.dtype)


def _moba_attention(p3, bias_tiles):
    bsz, seq, _ = p3.shape
    t = ATT_TILE
    n_tiles = seq // t
    topb = min(MOBA_TOPK_MAX, n_tiles)
    kern = functools.partial(_moba_kernel, n_tiles=n_tiles, topb=float(topb))
    w = MOBA_HEADS * HEAD_DIM
    blocks = w // LANES
    first = (DIFF_HEADS + DSA_HEADS) // MOBA_HEADS
    return pl.pallas_call(
        kern,
        grid=(bsz, n_tiles),
        in_specs=[
            pl.BlockSpec((1, t, w), lambda b, i: (b, i, BLK_CQ // blocks)),
            pl.BlockSpec((1, seq, w), lambda b, i: (b, 0, BLK_CK // blocks)),
            pl.BlockSpec((1, seq, w), lambda b, i: (b, 0, BLK_CV // blocks)),
            pl.BlockSpec((MOBA_HEADS, 3, t, t), lambda b, i: (first, 0, 0, 0)),
        ],
        out_specs=pl.BlockSpec((1, t, w), lambda b, i: (b, i, 0)),
        out_shape=jax.ShapeDtypeStruct((bsz, seq, w), BF16),
        scratch_shapes=[
            pltpu.VMEM((blocks, n_tiles, LANES + SUM_ROWS, t), BF16),
            pltpu.VMEM((n_tiles, w), F32),
            pltpu.VMEM((MOBA_HEADS, n_tiles, 8, t), F32),
            pltpu.VMEM((MOBA_HEADS, LANES, t), BF16),
        ] + _attend_scratch(MOBA_HEADS, LANES),
        compiler_params=pltpu.CompilerParams(
            dimension_semantics=("parallel", "arbitrary"), vmem_limit_bytes=VMEM_LIMIT),
        name="moba_attn",
    )(p3, p3, p3, bias_tiles)


def _merge_kernel(x_ref, g_ref, gb_ref, oa_ref, ob_ref, oc_ref, wa_ref, wb_ref, wc_ref, wo_ref,
                  mod_ref, o_ref):
    d = D_MODEL
    gates = jax.nn.sigmoid(g_ref[...].astype(F32) + gb_ref[...])
    merged = (gates[:, 0:d] * _nn(oa_ref[...], wa_ref[...])
              + gates[:, d:2 * d] * _nn(ob_ref[...], wb_ref[...])
              + gates[:, 2 * d:3 * d] * _nn(oc_ref[...], wc_ref[...]))
    z = _nn(merged.astype(BF16), wo_ref[...])
    o_ref[...] = x_ref[...] + mod_ref[0, 2:3, :] * z


def _merge(x2, p2, gate_b, oa, ob, oc, wa, wb, wc, wo, mod_l, seq):
    m, d = x2.shape
    tm = min(512, seq)
    full = lambda a: pl.BlockSpec(a.shape, lambda i: (0, 0))
    return pl.pallas_call(
        _merge_kernel,
        grid=(m // tm,),
        in_specs=[
            pl.BlockSpec((tm, d), lambda i: (i, 0)),
            pl.BlockSpec((tm, 3 * d), lambda i: (i, BLK_G)),
            pl.BlockSpec((1, 3 * d), lambda i: (0, 0)),
            pl.BlockSpec((tm, oa.shape[1]), lambda i: (i, 0)),
            pl.BlockSpec((tm, ob.shape[1]), lambda i: (i, 0)),
            pl.BlockSpec((tm, oc.shape[1]), lambda i: (i, 0)),
            full(wa), full(wb), full(wc), full(wo),
            pl.BlockSpec((1, 6, d), lambda i: ((i * tm) // seq, 0, 0)),
        ],
        out_specs=pl.BlockSpec((tm, d), lambda i: (i, 0)),
        out_shape=jax.ShapeDtypeStruct((m, d), F32),
        compiler_params=pltpu.CompilerParams(
            dimension_semantics=("parallel",), vmem_limit_bytes=VMEM_LIMIT),
        name="merge",
    )(x2, p2, gate_b.reshape(1, 3 * d), oa, ob, oc, wa, wb, wc, wo, mod_l)


def _mlp_kernel(x_ref, mod_ref, g_ref, w1_ref, w2_ref, gf_ref, o_ref, u_ref, acc_ref, *, final):
    f = pl.program_id(1)

    @pl.when(f == 0)
    def _():
        u = _norm_modulate(x_ref[...], g_ref[...], mod_ref[0, 3:4, :], mod_ref[0, 4:5, :])
        u_ref[...] = u.astype(BF16)
        acc_ref[...] = jnp.zeros(acc_ref.shape, F32)

    h = jnp.square(jnp.maximum(_nn(u_ref[...], w1_ref[...]), 0.0))
    acc_ref[...] += _nn(h.astype(BF16), w2_ref[...])

    @pl.when(f == pl.num_programs(1) - 1)
    def _():
        y = x_ref[...] + mod_ref[0, 5:6, :] * acc_ref[...]
        if final:
            ms = jnp.mean(y * y, axis=-1, keepdims=True)
            y = y * lax.rsqrt(ms + EPS) * gf_ref[...]
        o_ref[...] = y


def _mlp(x2, mod_l, g, w1, w2, g_final, seq, final):
    m, d = x2.shape
    dff = w1.shape[1]
    tm = min(1024, seq)
    tf = 1024
    return pl.pallas_call(
        functools.partial(_mlp_kernel, final=final),
        grid=(m // tm, dff // tf),
        in_specs=[
            pl.BlockSpec((tm, d), lambda i, f: (i, 0)),
            pl.BlockSpec((1, 6, d), lambda i, f: ((i * tm) // seq, 0, 0)),
            pl.BlockSpec((1, d), lambda i, f: (0, 0)),
            pl.BlockSpec((d, tf), lambda i, f: (0, f)),
            pl.BlockSpec((tf, d), lambda i, f: (f, 0)),
            pl.BlockSpec((1, d), lambda i, f: (0, 0)),
        ],
        out_specs=pl.BlockSpec((tm, d), lambda i, f: (i, 0)),
        out_shape=jax.ShapeDtypeStruct((m, d), F32),
        scratch_shapes=[pltpu.VMEM((tm, d), BF16), pltpu.VMEM((tm, d), F32)],
        compiler_params=pltpu.CompilerParams(
            dimension_semantics=("parallel", "arbitrary"), vmem_limit_bytes=VMEM_LIMIT),
        name="mlp",
    )(x2, mod_l, g.reshape(1, d), w1, w2, g_final.reshape(1, d))


def kernel(x, c, rel_bias, ada_w, ada_b, norm_mix, w_in, gate_b, diff_lambda, diff_subln,
           dsa_kv_norm, dsa_w_uv, w_br_a, w_br_b, w_br_c, w_o, norm_mlp, w_ff1, w_ff2,
           norm_final):
    bsz, seq, d = x.shape
    depth = w_in.shape[0]
    assert seq % ATT_TILE == 0 and 2 * ATT_TILE > MAX_DISTANCE

    w_in_p = _pack_columns(w_in).astype(BF16)
    n_out = DSA_HEADS * DSA_V_DIM
    wuv_t = jnp.transpose(dsa_w_uv, (0, 1, 3, 2))
    wuv_t_pad = jnp.zeros((depth, DSA_HEADS, n_out, DSA_LATENT), F32)
    for h in range(DSA_HEADS):
        wuv_t_pad = wuv_t_pad.at[:, h, h * DSA_V_DIM:(h + 1) * DSA_V_DIM, :].set(wuv_t[:, h])
    wuv_t_pad = wuv_t_pad.astype(BF16)
    wa, wb, wc, wo = (w.astype(BF16) for w in (w_br_a, w_br_b, w_br_c, w_o))
    w1, w2 = w_ff1.astype(BF16), w_ff2.astype(BF16)

    bias_tiles = _bias_tiles(rel_bias)
    mod = _ada(c, ada_w, ada_b).reshape(depth, bsz, 6, d)

    x2 = x.reshape(bsz * seq, d)
    for l in range(depth):
        lam_init = 0.8 - 0.6 * math.exp(-0.3 * l)
        p2 = _inproj(x2, mod[l], norm_mix[l], w_in_p[l], seq)
        p3 = p2.reshape(bsz, seq, PACKED_COLS)
        oa = _diff_attention(p3, bias_tiles, diff_lambda[l], diff_subln[l], lam_init)
        ob = _dsa_attention(p3, bias_tiles, dsa_kv_norm[l], wuv_t_pad[l])
        oc = _moba_attention(p3, bias_tiles)
        x2 = _merge(x2, p2, gate_b[l],
                    oa.reshape(bsz * seq, -1), ob.reshape(bsz * seq, -1), oc.reshape(bsz * seq, -1),
                    wa[l], wb[l], wc[l], wo[l], mod[l], seq)
        x2 = _mlp(x2, mod[l], norm_mlp[l], w1[l], w2[l], norm_final, seq, final=(l == depth - 1))
    return x2.reshape(bsz, seq, d)
```

```python
import functools
import math

import jax
import jax.numpy as jnp
from jax import lax
from jax.experimental import pallas as pl
from jax.experimental.pallas import tpu as pltpu

F32 = jnp.float32
BF16 = jnp.bfloat16

D_MODEL = 1024
HEAD_DIM = 64
DIFF_HEADS = 4
DIFF_V_DIM = 2 * HEAD_DIM
DSA_HEADS = 4
DSA_LATENT = 128
DSA_V_DIM = 64
IDX_HEADS = 8
IDX_DIM = 32
DSA_TOPK_MAX = 256
MOBA_HEADS = 4
MOBA_BLOCK = 256
MOBA_TOPK_MAX = 3
N_BUCKETS = 32
MAX_DISTANCE = 128
N_BIAS_HEADS = DIFF_HEADS + DSA_HEADS + MOBA_HEADS
D_FF = 4 * D_MODEL
EPS = 1e-6

LANES = 128
ATT_TILE = MOBA_BLOCK
NEG = -1e30
INT_MIN = -2 ** 31
VMEM_LIMIT = 52 * 1024 * 1024
DIFF_HEADS_PER_STEP = 4
SUM_ROWS = 16
LOG2E = math.log2(math.e)

_O_AQ, _O_AK, _O_AV, _O_BQ, _O_BKV, _O_BIQ, _O_BIK, _O_BIW, _O_CQ, _O_CK, _O_CV, _O_G = (
    0, 512, 1024, 1536, 2048, 2176, 2432, 2464, 2472, 2728, 2984, 3240)

BLK_G, BLK_AQ, BLK_AK, BLK_AV, BLK_BQ, BLK_BIQ, BLK_BKV, BLK_IK, BLK_CQ, BLK_CK, BLK_CV, BLK_IW = (
    0, 24, 28, 32, 36, 40, 42, 43, 44, 46, 48, 50)
N_BLKS = 51
PACKED_COLS = N_BLKS * LANES


def _pack_columns(w):
    seg = lambda off, n: w[..., off:off + n]
    ik = seg(_O_BIK, IDX_DIM)
    parts = [
        seg(_O_G, 3 * D_MODEL), seg(_O_AQ, 512), seg(_O_AK, 512), seg(_O_AV, 512),
        seg(_O_BQ, 512), seg(_O_BIQ, 256), seg(_O_BKV, 128),
        ik, ik, ik, ik,
        seg(_O_CQ, 256), seg(_O_CK, 256), seg(_O_CV, 256),
        seg(_O_BIW, IDX_HEADS),
        jnp.zeros(w.shape[:-1] + (LANES - IDX_HEADS,), w.dtype),
    ]
    out = jnp.concatenate(parts, axis=-1)
    assert out.shape[-1] == PACKED_COLS
    return out


def _nt(a, b):
    return lax.dot_general(a, b, (((1,), (1,)), ((), ())), preferred_element_type=F32)


def _nn(a, b):
    return jnp.dot(a, b, preferred_element_type=F32)


def _eye(n):
    r = lax.broadcasted_iota(jnp.int32, (n, n), 0)
    c = lax.broadcasted_iota(jnp.int32, (n, n), 1)
    return jnp.where(r == c, 1.0, 0.0).astype(BF16)


def _transposed(x_f32, eye):
    return _nt(eye, x_f32.astype(BF16)).astype(BF16)


def _lane_group_mask(x_bf16, group, width):
    lane = lax.broadcasted_iota(jnp.int32, x_bf16.shape, 1)
    keep = (lane >= group * width) & (lane < (group + 1) * width)
    return jnp.where(keep, x_bf16.astype(F32), 0.0)


def _ada_kernel(c_ref, w_ref, b_ref, o_ref):
    c = c_ref[...]
    cond = c * jax.nn.sigmoid(c)
    o_ref[0] = _nn(cond, w_ref[0]) + b_ref[0]


def _ada(c, ada_w, ada_b):
    depth, d, n = ada_w.shape
    bsz = c.shape[0]
    tn = 1536
    return pl.pallas_call(
        _ada_kernel,
        grid=(depth, n // tn),
        in_specs=[
            pl.BlockSpec((bsz, d), lambda l, j: (0, 0)),
            pl.BlockSpec((1, d, tn), lambda l, j: (l, 0, j)),
            pl.BlockSpec((1, 1, tn), lambda l, j: (l, 0, j)),
        ],
        out_specs=pl.BlockSpec((1, bsz, tn), lambda l, j: (l, 0, j)),
        out_shape=jax.ShapeDtypeStruct((depth, bsz, n), F32),
        compiler_params=pltpu.CompilerParams(
            dimension_semantics=("arbitrary", "arbitrary"), vmem_limit_bytes=VMEM_LIMIT),
        name="ada_mod",
    )(c, ada_w, ada_b.reshape(depth, 1, n))


def _t5_bucket(dist):
    max_exact = N_BUCKETS // 2
    n = jnp.maximum(dist, 0)
    nf = jnp.maximum(n, 1).astype(F32)
    large = max_exact + (jnp.log(nf / max_exact) / math.log(MAX_DISTANCE / max_exact)
                         * (N_BUCKETS - max_exact)).astype(jnp.int32)
    large = jnp.minimum(large, N_BUCKETS - 1)
    return jnp.where(n < max_exact, n, large)


def _bias_kernel(rb_ref, o_ref):
    t = ATT_TILE
    h = pl.program_id(0)
    kk = lax.broadcasted_iota(jnp.int32, (t, t), 0)
    qq = lax.broadcasted_iota(jnp.int32, (t, t), 1)
    for kind in range(3):
        dist = jnp.full((t, t), 2 * t, jnp.int32) if kind == 2 else kind * t + qq - kk
        bucket = _t5_bucket(dist)
        tile = jnp.zeros((t, t), F32)
        for b in range(N_BUCKETS):
            tile = jnp.where(bucket == b, rb_ref[b, h], tile)
        o_ref[0, kind] = jnp.where(dist >= 0, tile * LOG2E, NEG)


def _bias_tiles(rel_bias):
    t = ATT_TILE
    heads = rel_bias.shape[1]
    return pl.pallas_call(
        _bias_kernel,
        grid=(heads,),
        in_specs=[pl.BlockSpec(memory_space=pltpu.SMEM)],
        out_specs=pl.BlockSpec((1, 3, t, t), lambda h: (h, 0, 0, 0)),
        out_shape=jax.ShapeDtypeStruct((heads, 3, t, t), F32),
        compiler_params=pltpu.CompilerParams(dimension_semantics=("arbitrary",)),
        name="bias_tiles",
    )(rel_bias.astype(F32))


def _norm_modulate(x, g, shift, scale):
    ms = jnp.mean(x * x, axis=-1, keepdims=True)
    y = x * lax.rsqrt(ms + EPS) * g
    return y * (1.0 + scale) + shift


def _inproj_kernel(x_ref, mod_ref, g_ref, w_ref, o_ref, *, tn):
    u = _norm_modulate(x_ref[...], g_ref[...], mod_ref[0, 0:1, :], mod_ref[0, 1:2, :])
    u = u.astype(BF16)
    for j in range(w_ref.shape[1] // tn):
        cols = slice(j * tn, (j + 1) * tn)
        o_ref[:, cols] = _nn(u, w_ref[:, cols]).astype(o_ref.dtype)


def _inproj(x2, mod_l, g, w_packed, seq):
    m, d = x2.shape
    n = w_packed.shape[1]
    tm = min(512, seq)
    return pl.pallas_call(
        functools.partial(_inproj_kernel, tn=n // 3),
        grid=(m // tm,),
        in_specs=[
            pl.BlockSpec((tm, d), lambda i: (i, 0)),
            pl.BlockSpec((1, 6, d), lambda i: ((i * tm) // seq, 0, 0)),
            pl.BlockSpec((1, d), lambda i: (0, 0)),
            pl.BlockSpec((d, n), lambda i: (0, 0), pipeline_mode=pl.Buffered(1)),
        ],
        out_specs=pl.BlockSpec((tm, n), lambda i: (i, 0)),
        out_shape=jax.ShapeDtypeStruct((m, n), BF16),
        compiler_params=pltpu.CompilerParams(
            dimension_semantics=("parallel",), vmem_limit_bytes=VMEM_LIMIT),
        name="in_proj",
    )(x2, mod_l, g.reshape(1, d), w_packed)


def _attend(n_live, chains, tile_ctx, logits, values, s_ref, m_ref, a_ref, acc_ref):
    m_ref[...] = jnp.full(m_ref.shape, NEG, F32)
    acc_ref[...] = jnp.zeros(acc_ref.shape, F32)

    def start_chain(j, slot, c, ctx):
        s = logits(j, c, ctx)
        m_old = m_ref[1 - slot, c]
        m_new = jnp.maximum(m_old, jnp.max(s, axis=0, keepdims=True))
        s_ref[slot, c] = s
        a_ref[slot, c] = jnp.exp2(m_old - m_new)
        m_ref[slot, c] = m_new

    def finish_chain(j, slot, c):
        p = jnp.exp2(s_ref[slot, c] - m_ref[slot, c])
        acc_ref[c] = a_ref[slot, c] * acc_ref[c] + _nn(values(j, c), p.astype(BF16))

    ctx = tile_ctx(0)
    for c in range(chains):
        start_chain(0, 0, c, ctx)

    def body(j, carry):
        for slot in range(2):
            @pl.when(j % 2 == slot)
            def _(slot=slot):
                ctx = tile_ctx(j)
                for c in range(chains):
                    start_chain(j, slot, c, ctx)
                    finish_chain(j - 1, 1 - slot, c)
        return carry

    lax.fori_loop(1, n_live, body, 0)
    for slot in range(2):
        @pl.when((n_live - 1) % 2 == slot)
        def _(slot=slot):
            for c in range(chains):
                finish_chain(n_live - 1, slot, c)


def _attend_scratch(chains, dv):
    t = ATT_TILE
    return [
        pltpu.VMEM((2, chains, t, t), F32),
        pltpu.VMEM((2, chains, 1, t), F32),
        pltpu.VMEM((2, chains, 1, t), F32),
        pltpu.VMEM((chains, dv + SUM_ROWS, t), F32),
    ]


def _attend_result(acc_ref, c, dv):
    return acc_ref[c, 0:dv, :] / acc_ref[c, dv:dv + 1, :]


def _key_tile(ref, j, cols=None):
    t = ATT_TILE
    rows = slice(j * t, (j + 1) * t) if isinstance(j, int) else pl.ds(pl.multiple_of(j * t, t), t)
    return ref[0, rows, :] if cols is None else ref[0, rows, cols]


def _transpose_tiles(src_ref, dst_ref, n_tiles, groups):
    t = ATT_TILE
    eye = _eye(LANES)
    for g in range(groups):
        for j in range(n_tiles):
            blk = src_ref[0, j * t:(j + 1) * t, g * LANES:(g + 1) * LANES]
            dst_ref[g, j, 0:LANES, :] = _nt(eye, blk).astype(BF16)
            dst_ref[g, j, LANES:LANES + SUM_ROWS, :] = jnp.ones((SUM_ROWS, t), BF16)


def _diff_kernel(q_ref, k_ref, v_ref, bias_ref, dl_ref, g_ref, o_ref,
                 vt_ref, qs_ref, s_ref, m_ref, a_ref, acc_ref, *, lam_init, n_tiles, hp):
    t = ATT_TILE
    i = pl.program_id(2)

    @pl.when(i == 0)
    def _():
        _transpose_tiles(v_ref, vt_ref, n_tiles, hp)

    scale = HEAD_DIM ** -0.5 * LOG2E
    eye_l = _eye(LANES)
    for h in range(hp):
        q = q_ref[0, :, h * LANES:(h + 1) * LANES]
        for half in range(2):
            qs_ref[2 * h + half] = _transposed(_lane_group_mask(q, half, HEAD_DIM) * scale, eye_l)

    def logits(j, c, ctx):
        h = c // 2
        kblk = _key_tile(k_ref, j, slice(h * LANES, (h + 1) * LANES))
        return _nn(kblk, qs_ref[c]) + bias_ref[h, jnp.minimum(i - j, 2)]

    _attend(i + 1, 2 * hp, lambda j: None, logits, lambda j, c: vt_ref[c // 2, j],
            s_ref, m_ref, a_ref, acc_ref)

    dl = dl_ref[...]
    lam = (jnp.exp(jnp.sum(dl[0:1] * dl[1:2], keepdims=True))
           - jnp.exp(jnp.sum(dl[2:3] * dl[3:4], keepdims=True)) + lam_init)
    eye = _eye(t)
    for h in range(hp):
        o = (_attend_result(acc_ref, 2 * h, DIFF_V_DIM)
             - lam * _attend_result(acc_ref, 2 * h + 1, DIFF_V_DIM))
        ms = jnp.mean(o * o, axis=0, keepdims=True)
        o = o * lax.rsqrt(ms + EPS) * g_ref[...] * (1.0 - lam_init)
        o_ref[0, :, h * LANES:(h + 1) * LANES] = _nt(eye, o.astype(BF16)).astype(o_ref.dtype)


def _diff_attention(p3, bias_tiles, diff_lambda_l, subln_g, lam_init):
    bsz, seq, _ = p3.shape
    t = ATT_TILE
    n_tiles = seq // t
    hp = DIFF_HEADS_PER_STEP
    w = hp * LANES
    kern = functools.partial(_diff_kernel, lam_init=lam_init, n_tiles=n_tiles, hp=hp)
    return pl.pallas_call(
        kern,
        grid=(bsz, DIFF_HEADS // hp, n_tiles),
        in_specs=[
            pl.BlockSpec((1, t, w), lambda b, h, i: (b, i, BLK_AQ // hp + h)),
            pl.BlockSpec((1, seq, w), lambda b, h, i: (b, 0, BLK_AK // hp + h)),
            pl.BlockSpec((1, seq, w), lambda b, h, i: (b, 0, BLK_AV // hp + h)),
            pl.BlockSpec((hp, 3, t, t), lambda b, h, i: (h, 0, 0, 0)),
            pl.BlockSpec((4, HEAD_DIM), lambda b, h, i: (0, 0)),
            pl.BlockSpec((DIFF_V_DIM, 1), lambda b, h, i: (0, 0)),
        ],
        out_specs=pl.BlockSpec((1, t, w), lambda b, h, i: (b, i, h)),
        out_shape=jax.ShapeDtypeStruct((bsz, seq, DIFF_HEADS * DIFF_V_DIM), BF16),
        scratch_shapes=[
            pltpu.VMEM((hp, n_tiles, LANES + SUM_ROWS, t), BF16),
            pltpu.VMEM((2 * hp, LANES, t), BF16),
        ] + _attend_scratch(2 * hp, DIFF_V_DIM),
        compiler_params=pltpu.CompilerParams(
            dimension_semantics=("parallel", "parallel", "arbitrary"),
            vmem_limit_bytes=VMEM_LIMIT),
        name="diff_attn",
    )(p3, p3, p3, bias_tiles, diff_lambda_l, subln_g.reshape(DIFF_V_DIM, 1))


def _dsa_kernel(q_ref, kv_ref, iq_ref, ik_ref, iw_ref, bias_ref, g_ref, wuv_ref, o_ref,
                kvn_ref, kvt_ref, keys_ref, byte_ref, cand_ref, thr_ref, cut_ref, need_ref, tied_ref,
                iqh_ref, qt_ref,
                s_ref, m_ref, a_ref, acc_ref, *, n_tiles, topk):
    t = ATT_TILE
    i = pl.program_id(1)
    n_live = i + 1

    @pl.when(i == 0)
    def _():
        kv = kv_ref[0].astype(F32)
        ms = jnp.mean(kv * kv, axis=-1, keepdims=True)
        kvn_ref[0] = (kv * lax.rsqrt(ms + EPS) * g_ref[...]).astype(BF16)
        _transpose_tiles(kvn_ref, kvt_ref, n_tiles, 1)

    groups = LANES // IDX_DIM
    eye_l = _eye(LANES)
    for h in range(IDX_HEADS):
        blk = iq_ref[0, :, (h // groups) * LANES:(h // groups + 1) * LANES]
        iqh_ref[h] = _transposed(_lane_group_mask(blk, h % groups, IDX_DIM), eye_l)
    for h in range(DSA_HEADS):
        qt_ref[h] = _nt(eye_l, q_ref[0, :, h * LANES:(h + 1) * LANES]).astype(BF16)
    sel_rows = lax.broadcasted_iota(jnp.int32, (IDX_HEADS, LANES), 0)
    sel_cols = lax.broadcasted_iota(jnp.int32, (IDX_HEADS, LANES), 1)
    pick = jnp.where(sel_rows == sel_cols, 1.0, 0.0).astype(BF16)
    iw_t = _nt(pick, iw_ref[0])
    idx_scale = (IDX_HEADS ** -0.5) * (IDX_DIM ** -0.5)
    row = lax.broadcasted_iota(jnp.int32, (t, t), 0)
    col = lax.broadcasted_iota(jnp.int32, (t, t), 1)

    def score_body(j, carry):
        ik = _key_tile(ik_ref, j)
        sc = jnp.zeros((t, t), F32)
        for h in range(IDX_HEADS):
            sc = sc + jnp.maximum(_nn(ik, iqh_ref[h]), 0.0) * iw_t[h:h + 1, :]
        sc = sc * idx_scale
        bits = pltpu.bitcast(sc, jnp.int32)
        key = bits ^ ((bits >> 31) & 0x7FFFFFFF)
        valid = (i * t + col) >= (j * t + row)
        keys_ref[j] = jnp.where(valid, key, INT_MIN)
        top = jnp.where(valid, (key >> 24) + 128, -1)
        cand_ref[j] = top.astype(F32).astype(BF16)
        for b in range(3):
            byte_ref[b, j] = ((key >> (16 - 8 * b)) & 0xFF).astype(F32).astype(BF16)
        return carry

    lax.fori_loop(0, n_live, score_body, 0)

    one_b, zero_b = jnp.ones((), BF16), jnp.zeros((), BF16)
    packed_rows = 16

    def radix_select(live):
        def count(hit_fn):
            part = None
            for j in range(live):
                hit = jnp.where(hit_fn(cand_ref[j]), one_b, zero_b)
                for r in range(t // packed_rows):
                    rows = hit[packed_rows * r:packed_rows * (r + 1)]
                    part = rows if part is None else part + rows
            return jnp.sum(part.astype(F32), axis=0, keepdims=True)

        need = jnp.full((1, t), topk, F32)
        thr = jnp.zeros((1, t), jnp.int32)
        for b in range(4):
            def bit_body(it, val, need=need):
                cand = val | lax.shift_left(jnp.int32(1), 7 - it)
                cand_b = cand.astype(F32).astype(BF16)
                return jnp.where(count(lambda x: x >= cand_b) >= need, cand, val)

            val = lax.fori_loop(0, 8, bit_body, jnp.zeros((1, t), jnp.int32))
            val_b = val.astype(F32).astype(BF16)
            need = need - count(lambda x: x > val_b)
            if b < 3:
                for j in range(live):
                    cand_ref[j] = jnp.where(cand_ref[j] == val_b, byte_ref[b, j], -one_b)
            else:
                tied_ref[...] = count(lambda x: x == val_b)
            piece = val - (128 if b == 0 else 0)
            thr = thr | lax.shift_left(piece, 24 - 8 * b)
        thr_ref[...] = thr
        need_ref[...] = need

    assert (t // packed_rows) * n_tiles <= 256
    for live in range(1, n_tiles + 1):
        pl.when(n_live == live)(functools.partial(radix_select, live))

    thr = thr_ref[...]
    need = need_ref[...]
    has_rank = thr > INT_MIN
    excess = jnp.where(has_rank & (tied_ref[...] > need), 1.0, 0.0)
    cut_ref[...] = jnp.where(has_rank, jnp.int32(2 * n_tiles * t), jnp.int32(0))

    @pl.when(jnp.max(excess) > 0.0)
    def _():
        n_bits = (2 * n_tiles * t - 1).bit_length()

        def count_ties_before(cand):
            def cb(j, acc):
                hit = (keys_ref[j] == thr) & ((j * t + row) < cand)
                return acc + jnp.sum(jnp.where(hit, 1.0, 0.0), axis=0, keepdims=True)
            return lax.fori_loop(0, n_live, cb, jnp.zeros((1, t), F32))

        def cut_body(it, cut):
            cand = cut | lax.shift_left(jnp.int32(1), n_bits - 1 - it)
            return jnp.where(count_ties_before(cand) <= need, cand, cut)

        cut = lax.fori_loop(0, n_bits, cut_body, jnp.zeros((1, t), jnp.int32))
        cut_ref[...] = jnp.where(has_rank, cut, jnp.int32(0))

    scale = DSA_LATENT ** -0.5 * LOG2E

    def tile_ctx(j):
        key = keys_ref[j]
        thr_v = thr_ref[...]
        sel = (key > thr_v) | ((key == thr_v) & ((j * t + row) < cut_ref[...]))
        return sel, _key_tile(kvn_ref, j), jnp.minimum(i - j, 2)

    def logits(j, h, ctx):
        sel, kvb, kind = ctx
        return jnp.where(sel, _nn(kvb, qt_ref[h]) * scale + bias_ref[h, kind], NEG)

    _attend(n_live, DSA_HEADS, tile_ctx, logits, lambda j, h: kvt_ref[0, j],
            s_ref, m_ref, a_ref, acc_ref)

    y_t = jnp.zeros((DSA_HEADS * DSA_V_DIM, t), F32)
    for h in range(DSA_HEADS):
        o_h = _attend_result(acc_ref, h, DSA_LATENT).astype(BF16)
        y_t = y_t + _nn(wuv_ref[h], o_h)
    o_ref[0] = _nt(_eye(t), y_t.astype(BF16)).astype(o_ref.dtype)


def _dsa_attention(p3, bias_tiles, kv_norm_g, wuv_t_pad):
    bsz, seq, _ = p3.shape
    t = ATT_TILE
    n_tiles = seq // t
    topk = min(DSA_TOPK_MAX, seq // 4)
    kern = functools.partial(_dsa_kernel, n_tiles=n_tiles, topk=float(topk))
    n_out = DSA_HEADS * DSA_V_DIM
    return pl.pallas_call(
        kern,
        grid=(bsz, n_tiles),
        in_specs=[
            pl.BlockSpec((1, t, 4 * LANES), lambda b, i: (b, i, BLK_BQ // 4)),
            pl.BlockSpec((1, seq, LANES), lambda b, i: (b, 0, BLK_BKV)),
            pl.BlockSpec((1, t, 2 * LANES), lambda b, i: (b, i, BLK_BIQ // 2)),
            pl.BlockSpec((1, seq, LANES), lambda b, i: (b, 0, BLK_IK)),
            pl.BlockSpec((1, t, LANES), lambda b, i: (b, i, BLK_IW)),
            pl.BlockSpec((DSA_HEADS, 3, t, t), lambda b, i: (DIFF_HEADS // DSA_HEADS, 0, 0, 0)),
            pl.BlockSpec((1, DSA_LATENT), lambda b, i: (0, 0)),
            pl.BlockSpec((DSA_HEADS, n_out, DSA_LATENT), lambda b, i: (0, 0, 0)),
        ],
        out_specs=pl.BlockSpec((1, t, n_out), lambda b, i: (b, i, 0)),
        out_shape=jax.ShapeDtypeStruct((bsz, seq, n_out), BF16),
        scratch_shapes=[
            pltpu.VMEM((1, seq, DSA_LATENT), BF16),
            pltpu.VMEM((1, n_tiles, DSA_LATENT + SUM_ROWS, t), BF16),
            pltpu.VMEM((n_tiles, t, t), jnp.int32),
            pltpu.VMEM((3, n_tiles, t, t), BF16),
            pltpu.VMEM((n_tiles, t, t), BF16),
            pltpu.VMEM((1, t), jnp.int32),
            pltpu.VMEM((1, t), jnp.int32),
            pltpu.VMEM((1, t), F32),
            pltpu.VMEM((1, t), F32),
            pltpu.VMEM((IDX_HEADS, LANES, t), BF16),
            pltpu.VMEM((DSA_HEADS, LANES, t), BF16),
        ] + _attend_scratch(DSA_HEADS, DSA_LATENT),
        compiler_params=pltpu.CompilerParams(
            dimension_semantics=("parallel", "arbitrary"), vmem_limit_bytes=VMEM_LIMIT),
        name="dsa_attn",
    )(p3, p3, p3, p3, p3, bias_tiles, kv_norm_g.reshape(1, DSA_LATENT), wuv_t_pad)


def _moba_kernel(q_ref, k_ref, v_ref, bias_ref, o_ref,
                 vt_ref, kmean_ref, selb_ref, qs_ref,
                 s_ref, m_ref, a_ref, acc_ref, *, n_tiles, topb):
    t = ATT_TILE
    i = pl.program_id(1)
    pairs = MOBA_HEADS // 2

    @pl.when(i == 0)
    def _():
        _transpose_tiles(v_ref, vt_ref, n_tiles, pairs)
        for n in range(n_tiles):
            kb = k_ref[0, n * t:(n + 1) * t, :].astype(F32)
            kmean_ref[n:n + 1, :] = jnp.mean(kb, axis=0, keepdims=True)

    scale = HEAD_DIM ** -0.5
    eye_l = _eye(LANES)
    blk = lax.broadcasted_iota(jnp.int32, (n_tiles, t), 0)
    own = jnp.full((1, t), i, jnp.int32)
    for hd in range(MOBA_HEADS):
        g, half = hd // 2, hd % 2
        q_f = _lane_group_mask(q_ref[0, :, g * LANES:(g + 1) * LANES], half, HEAD_DIM)
        qs_ref[hd] = _transposed(q_f * (scale * LOG2E), eye_l)
        kmean = kmean_ref[:, g * LANES:(g + 1) * LANES].astype(BF16)
        gate = _nt(kmean, q_f.astype(BF16))
        for n in range(n_tiles):
            gn = gate[n:n + 1, :]
            ahead = (gate > gn) | ((gate == gn) & (blk < n))
            ahead = ahead & (blk < i)
            rank = jnp.sum(jnp.where(ahead, 1.0, 0.0), axis=0, keepdims=True)
            chosen = ((rank < topb) & (own > n)) | (own == n)
            selb_ref[hd, n] = jnp.broadcast_to(jnp.where(chosen, 0.0, NEG), (8, t))

    def logits(j, hd, ctx):
        g = hd // 2
        kblk = _key_tile(k_ref, j, slice(g * LANES, (g + 1) * LANES))
        return (_nn(kblk, qs_ref[hd]) + bias_ref[hd, jnp.minimum(i - j, 2)]
                + selb_ref[hd, j][0:1, :])

    _attend(i + 1, MOBA_HEADS, lambda j: None, logits, lambda j, hd: vt_ref[hd // 2, j],
            s_ref, m_ref, a_ref, acc_ref)

    rows = lax.broadcasted_iota(jnp.int32, (LANES, t), 0)
    eye = _eye(t)
    for g in range(pairs):
        o_lo = _attend_result(acc_ref, 2 * g, LANES)
        o_hi = _attend_result(acc_ref, 2 * g + 1, LANES)
        o = jnp.where(rows < HEAD_DIM, o_lo, o_hi)
        o_ref[0, :, g * LANES:(g + 1) * LANES] = _nt(eye, o.astype(BF16)).astype(o_ref.dtype)


def _moba_attention(p3, bias_tiles):
    bsz, seq, _ = p3.shape
    t = ATT_TILE
    n_tiles = seq // t
    topb = min(MOBA_TOPK_MAX, n_tiles)
    kern = functools.partial(_moba_kernel, n_tiles=n_tiles, topb=float(topb))
    w = MOBA_HEADS * HEAD_DIM
    blocks = w // LANES
    first = (DIFF_HEADS + DSA_HEADS) // MOBA_HEADS
    return pl.pallas_call(
        kern,
        grid=(bsz, n_tiles),
        in_specs=[
            pl.BlockSpec((1, t, w), lambda b, i: (b, i, BLK_CQ // blocks)),
            pl.BlockSpec((1, seq, w), lambda b, i: (b, 0, BLK_CK // blocks)),
            pl.BlockSpec((1, seq, w), lambda b, i: (b, 0, BLK_CV // blocks)),
            pl.BlockSpec((MOBA_HEADS, 3, t, t), lambda b, i: (first, 0, 0, 0)),
        ],
        out_specs=pl.BlockSpec((1, t, w), lambda b, i: (b, i, 0)),
        out_shape=jax.ShapeDtypeStruct((bsz, seq, w), BF16),
        scratch_shapes=[
            pltpu.VMEM((blocks, n_tiles, LANES + SUM_ROWS, t), BF16),
            pltpu.VMEM((n_tiles, w), F32),
            pltpu.VMEM((MOBA_HEADS, n_tiles, 8, t), F32),
            pltpu.VMEM((MOBA_HEADS, LANES, t), BF16),
        ] + _attend_scratch(MOBA_HEADS, LANES),
        compiler_params=pltpu.CompilerParams(
            dimension_semantics=("parallel", "arbitrary"), vmem_limit_bytes=VMEM_LIMIT),
        name="moba_attn",
    )(p3, p3, p3, bias_tiles)


def _merge_kernel(x_ref, g_ref, gb_ref, oa_ref, ob_ref, oc_ref, wa_ref, wb_ref, wc_ref, wo_ref,
                  mod_ref, o_ref):
    d = D_MODEL
    gates = jax.nn.sigmoid(g_ref[...].astype(F32) + gb_ref[...])
    merged = (gates[:, 0:d] * _nn(oa_ref[...], wa_ref[...])
              + gates[:, d:2 * d] * _nn(ob_ref[...], wb_ref[...])
              + gates[:, 2 * d:3 * d] * _nn(oc_ref[...], wc_ref[...]))
    z = _nn(merged.astype(BF16), wo_ref[...])
    o_ref[...] = x_ref[...] + mod_ref[0, 2:3, :] * z


def _merge(x2, p2, gate_b, oa, ob, oc, wa, wb, wc, wo, mod_l, seq):
    m, d = x2.shape
    tm = min(512, seq)
    full = lambda a: pl.BlockSpec(a.shape, lambda i: (0, 0))
    return pl.pallas_call(
        _merge_kernel,
        grid=(m // tm,),
        in_specs=[
            pl.BlockSpec((tm, d), lambda i: (i, 0)),
            pl.BlockSpec((tm, 3 * d), lambda i: (i, BLK_G)),
            pl.BlockSpec((1, 3 * d), lambda i: (0, 0)),
            pl.BlockSpec((tm, oa.shape[1]), lambda i: (i, 0)),
            pl.BlockSpec((tm, ob.shape[1]), lambda i: (i, 0)),
            pl.BlockSpec((tm, oc.shape[1]), lambda i: (i, 0)),
            full(wa), full(wb), full(wc), full(wo),
            pl.BlockSpec((1, 6, d), lambda i: ((i * tm) // seq, 0, 0)),
        ],
        out_specs=pl.BlockSpec((tm, d), lambda i: (i, 0)),
        out_shape=jax.ShapeDtypeStruct((m, d), F32),
        compiler_params=pltpu.CompilerParams(
            dimension_semantics=("parallel",), vmem_limit_bytes=VMEM_LIMIT),
        name="merge",
    )(x2, p2, gate_b.reshape(1, 3 * d), oa, ob, oc, wa, wb, wc, wo, mod_l)


def _mlp_kernel(x_ref, mod_ref, g_ref, w1_ref, w2_ref, gf_ref, o_ref, *, final, tf):
    x = x_ref[...]
    u = _norm_modulate(x, g_ref[...], mod_ref[0, 3:4, :], mod_ref[0, 4:5, :]).astype(BF16)
    acc = None
    for f in range(w1_ref.shape[1] // tf):
        h = jnp.square(jnp.maximum(_nn(u, w1_ref[:, f * tf:(f + 1) * tf]), 0.0))
        part = _nn(h.astype(BF16), w2_ref[f * tf:(f + 1) * tf, :])
        acc = part if acc is None else acc + part
    y = x + mod_ref[0, 5:6, :] * acc
    if final:
        ms = jnp.mean(y * y, axis=-1, keepdims=True)
        y = y * lax.rsqrt(ms + EPS) * gf_ref[...]
    o_ref[...] = y


def _mlp(x2, mod_l, g, w1, w2, g_final, seq, final):
    m, d = x2.shape
    dff = w1.shape[1]
    tm = min(512, seq)
    resident = lambda shape: pl.BlockSpec(shape, lambda i: (0, 0), pipeline_mode=pl.Buffered(1))
    return pl.pallas_call(
        functools.partial(_mlp_kernel, final=final, tf=1024),
        grid=(m // tm,),
        in_specs=[
            pl.BlockSpec((tm, d), lambda i: (i, 0)),
            pl.BlockSpec((1, 6, d), lambda i: ((i * tm) // seq, 0, 0)),
            pl.BlockSpec((1, d), lambda i: (0, 0)),
            resident((d, dff)),
            resident((dff, d)),
            pl.BlockSpec((1, d), lambda i: (0, 0)),
        ],
        out_specs=pl.BlockSpec((tm, d), lambda i: (i, 0)),
        out_shape=jax.ShapeDtypeStruct((m, d), F32),
        compiler_params=pltpu.CompilerParams(
            dimension_semantics=("parallel",), vmem_limit_bytes=VMEM_LIMIT),
        name="mlp",
    )(x2, mod_l, g.reshape(1, d), w1, w2, g_final.reshape(1, d))


def kernel(x, c, rel_bias, ada_w, ada_b, norm_mix, w_in, gate_b, diff_lambda, diff_subln,
           dsa_kv_norm, dsa_w_uv, w_br_a, w_br_b, w_br_c, w_o, norm_mlp, w_ff1, w_ff2,
           norm_final):
    bsz, seq, d = x.shape
    depth = w_in.shape[0]
    assert seq % ATT_TILE == 0 and 2 * ATT_TILE > MAX_DISTANCE

    w_in_p = _pack_columns(w_in).astype(BF16)
    n_out = DSA_HEADS * DSA_V_DIM
    wuv_t = jnp.transpose(dsa_w_uv, (0, 1, 3, 2))
    wuv_t_pad = jnp.zeros((depth, DSA_HEADS, n_out, DSA_LATENT), F32)
    for h in range(DSA_HEADS):
        wuv_t_pad = wuv_t_pad.at[:, h, h * DSA_V_DIM:(h + 1) * DSA_V_DIM, :].set(wuv_t[:, h])
    wuv_t_pad = wuv_t_pad.astype(BF16)
    wa, wb, wc, wo = (w.astype(BF16) for w in (w_br_a, w_br_b, w_br_c, w_o))
    w1, w2 = w_ff1.astype(BF16), w_ff2.astype(BF16)

    bias_tiles = _bias_tiles(rel_bias)
    mod = _ada(c, ada_w, ada_b).reshape(depth, bsz, 6, d)

    x2 = x.reshape(bsz * seq, d)
    for l in range(depth):
        lam_init = 0.8 - 0.6 * math.exp(-0.3 * l)
        p2 = _inproj(x2, mod[l], norm_mix[l], w_in_p[l], seq)
        p3 = p2.reshape(bsz, seq, PACKED_COLS)
        oa = _diff_attention(p3, bias_tiles, diff_lambda[l], diff_subln[l], lam_init)
        ob = _dsa_attention(p3, bias_tiles, dsa_kv_norm[l], wuv_t_pad[l])
        oc = _moba_attention(p3, bias_tiles)
        x2 = _merge(x2, p2, gate_b[l],
                    oa.reshape(bsz * seq, -1), ob.reshape(bsz * seq, -1), oc.reshape(bsz * seq, -1),
                    wa[l], wb[l], wc[l], wo[l], mod[l], seq)
        x2 = _mlp(x2, mod[l], norm_mlp[l], w1[l], w2[l], norm_final, seq, final=(l == depth - 1))
    return x2.reshape(bsz, seq, d)
```

```python
import functools
import math

import jax
import jax.numpy as jnp
from jax import lax
from jax.experimental import pallas as pl
from jax.experimental.pallas import tpu as pltpu

F32 = jnp.float32
BF16 = jnp.bfloat16

D_MODEL = 1024
HEAD_DIM = 64
DIFF_HEADS = 4
DIFF_V_DIM = 2 * HEAD_DIM
DSA_HEADS = 4
DSA_LATENT = 128
DSA_V_DIM = 64
IDX_HEADS = 8
IDX_DIM = 32
DSA_TOPK_MAX = 256
MOBA_HEADS = 4
MOBA_BLOCK = 256
MOBA_TOPK_MAX = 3
N_BUCKETS = 32
MAX_DISTANCE = 128
N_BIAS_HEADS = DIFF_HEADS + DSA_HEADS + MOBA_HEADS
D_FF = 4 * D_MODEL
EPS = 1e-6

LANES = 128
ATT_TILE = MOBA_BLOCK
NEG = -1e30
INT_MIN = -2 ** 31
VMEM_LIMIT = 52 * 1024 * 1024
DIFF_HEADS_PER_STEP = 4
SUM_ROWS = 16
LOG2E = math.log2(math.e)

_O_AQ, _O_AK, _O_AV, _O_BQ, _O_BKV, _O_BIQ, _O_BIK, _O_BIW, _O_CQ, _O_CK, _O_CV, _O_G = (
    0, 512, 1024, 1536, 2048, 2176, 2432, 2464, 2472, 2728, 2984, 3240)

BLK_G, BLK_AQ, BLK_AK, BLK_AV, BLK_BQ, BLK_BIQ, BLK_BKV, BLK_IK, BLK_CQ, BLK_CK, BLK_CV, BLK_IW = (
    0, 24, 28, 32, 36, 40, 42, 43, 44, 46, 48, 50)
N_BLKS = 51
PACKED_COLS = N_BLKS * LANES


def _pack_columns(w):
    seg = lambda off, n: w[..., off:off + n]
    ik = seg(_O_BIK, IDX_DIM)
    parts = [
        seg(_O_G, 3 * D_MODEL), seg(_O_AQ, 512), seg(_O_AK, 512), seg(_O_AV, 512),
        seg(_O_BQ, 512), seg(_O_BIQ, 256), seg(_O_BKV, 128),
        ik, ik, ik, ik,
        seg(_O_CQ, 256), seg(_O_CK, 256), seg(_O_CV, 256),
        seg(_O_BIW, IDX_HEADS),
        jnp.zeros(w.shape[:-1] + (LANES - IDX_HEADS,), w.dtype),
    ]
    out = jnp.concatenate(parts, axis=-1)
    assert out.shape[-1] == PACKED_COLS
    return out


def _nt(a, b):
    return lax.dot_general(a, b, (((1,), (1,)), ((), ())), preferred_element_type=F32)


def _nn(a, b):
    return jnp.dot(a, b, preferred_element_type=F32)


def _eye(n):
    r = lax.broadcasted_iota(jnp.int32, (n, n), 0)
    c = lax.broadcasted_iota(jnp.int32, (n, n), 1)
    return jnp.where(r == c, 1.0, 0.0).astype(BF16)


def _transposed(x_f32, eye):
    return _nt(eye, x_f32.astype(BF16)).astype(BF16)


def _lane_group_mask(x_bf16, group, width):
    lane = lax.broadcasted_iota(jnp.int32, x_bf16.shape, 1)
    keep = (lane >= group * width) & (lane < (group + 1) * width)
    return jnp.where(keep, x_bf16.astype(F32), 0.0)


def _ada_kernel(c_ref, w_ref, b_ref, o_ref):
    c = c_ref[...]
    cond = c * jax.nn.sigmoid(c)
    o_ref[0] = _nn(cond, w_ref[0]) + b_ref[0]


def _ada(c, ada_w, ada_b):
    depth, d, n = ada_w.shape
    bsz = c.shape[0]
    tn = 1536
    return pl.pallas_call(
        _ada_kernel,
        grid=(depth, n // tn),
        in_specs=[
            pl.BlockSpec((bsz, d), lambda l, j: (0, 0)),
            pl.BlockSpec((1, d, tn), lambda l, j: (l, 0, j)),
            pl.BlockSpec((1, 1, tn), lambda l, j: (l, 0, j)),
        ],
        out_specs=pl.BlockSpec((1, bsz, tn), lambda l, j: (l, 0, j)),
        out_shape=jax.ShapeDtypeStruct((depth, bsz, n), F32),
        compiler_params=pltpu.CompilerParams(
            dimension_semantics=("arbitrary", "arbitrary"), vmem_limit_bytes=VMEM_LIMIT),
        name="ada_mod",
    )(c, ada_w, ada_b.reshape(depth, 1, n))


def _t5_bucket(dist):
    max_exact = N_BUCKETS // 2
    n = jnp.maximum(dist, 0)
    nf = jnp.maximum(n, 1).astype(F32)
    large = max_exact + (jnp.log(nf / max_exact) / math.log(MAX_DISTANCE / max_exact)
                         * (N_BUCKETS - max_exact)).astype(jnp.int32)
    large = jnp.minimum(large, N_BUCKETS - 1)
    return jnp.where(n < max_exact, n, large)


def _bias_kernel(rb_ref, o_ref):
    t = ATT_TILE
    h = pl.program_id(0)
    kk = lax.broadcasted_iota(jnp.int32, (t, t), 0)
    qq = lax.broadcasted_iota(jnp.int32, (t, t), 1)
    for kind in range(3):
        dist = jnp.full((t, t), 2 * t, jnp.int32) if kind == 2 else kind * t + qq - kk
        bucket = _t5_bucket(dist)
        tile = jnp.zeros((t, t), F32)
        for b in range(N_BUCKETS):
            tile = jnp.where(bucket == b, rb_ref[b, h], tile)
        o_ref[0, kind] = jnp.where(dist >= 0, tile * LOG2E, NEG)


def _bias_tiles(rel_bias):
    t = ATT_TILE
    heads = rel_bias.shape[1]
    return pl.pallas_call(
        _bias_kernel,
        grid=(heads,),
        in_specs=[pl.BlockSpec(memory_space=pltpu.SMEM)],
        out_specs=pl.BlockSpec((1, 3, t, t), lambda h: (h, 0, 0, 0)),
        out_shape=jax.ShapeDtypeStruct((heads, 3, t, t), F32),
        compiler_params=pltpu.CompilerParams(dimension_semantics=("arbitrary",)),
        name="bias_tiles",
    )(rel_bias.astype(F32))


def _norm_modulate(x, g, shift, scale):
    ms = jnp.mean(x * x, axis=-1, keepdims=True)
    y = x * lax.rsqrt(ms + EPS) * g
    return y * (1.0 + scale) + shift


def _inproj_kernel(x_ref, mod_ref, g_ref, w_ref, o_ref, *, tn):
    u = _norm_modulate(x_ref[...], g_ref[...], mod_ref[0, 0:1, :], mod_ref[0, 1:2, :])
    u = u.astype(BF16)
    for j in range(w_ref.shape[1] // tn):
        cols = slice(j * tn, (j + 1) * tn)
        o_ref[:, cols] = _nn(u, w_ref[:, cols]).astype(o_ref.dtype)


def _inproj(x2, mod_l, g, w_packed, seq):
    m, d = x2.shape
    n = w_packed.shape[1]
    tm = min(512, seq)
    return pl.pallas_call(
        functools.partial(_inproj_kernel, tn=n // 3),
        grid=(m // tm,),
        in_specs=[
            pl.BlockSpec((tm, d), lambda i: (i, 0)),
            pl.BlockSpec((1, 6, d), lambda i: ((i * tm) // seq, 0, 0)),
            pl.BlockSpec((1, d), lambda i: (0, 0)),
            pl.BlockSpec((d, n), lambda i: (0, 0), pipeline_mode=pl.Buffered(1)),
        ],
        out_specs=pl.BlockSpec((tm, n), lambda i: (i, 0)),
        out_shape=jax.ShapeDtypeStruct((m, n), BF16),
        compiler_params=pltpu.CompilerParams(
            dimension_semantics=("parallel",), vmem_limit_bytes=VMEM_LIMIT),
        name="in_proj",
    )(x2, mod_l, g.reshape(1, d), w_packed)


def _attend(n_live, chains, tile_ctx, logits, values, s_ref, m_ref, a_ref, acc_ref):
    m_ref[...] = jnp.full(m_ref.shape, NEG, F32)
    acc_ref[...] = jnp.zeros(acc_ref.shape, F32)

    def start_chain(j, slot, c, ctx):
        s = logits(j, c, ctx)
        m_old = m_ref[1 - slot, c]
        m_new = jnp.maximum(m_old, jnp.max(s, axis=0, keepdims=True))
        s_ref[slot, c] = s
        a_ref[slot, c] = jnp.exp2(m_old - m_new)
        m_ref[slot, c] = m_new

    def finish_chain(j, slot, c):
        p = jnp.exp2(s_ref[slot, c] - m_ref[slot, c])
        acc_ref[c] = a_ref[slot, c] * acc_ref[c] + _nn(values(j, c), p.astype(BF16))

    ctx = tile_ctx(0)
    for c in range(chains):
        start_chain(0, 0, c, ctx)

    def body(j, carry):
        for slot in range(2):
            @pl.when(j % 2 == slot)
            def _(slot=slot):
                ctx = tile_ctx(j)
                for c in range(chains):
                    start_chain(j, slot, c, ctx)
                    finish_chain(j - 1, 1 - slot, c)
        return carry

    lax.fori_loop(1, n_live, body, 0)
    for slot in range(2):
        @pl.when((n_live - 1) % 2 == slot)
        def _(slot=slot):
            for c in range(chains):
                finish_chain(n_live - 1, slot, c)


def _attend_scratch(chains, dv):
    t = ATT_TILE
    return [
        pltpu.VMEM((2, chains, t, t), F32),
        pltpu.VMEM((2, chains, 1, t), F32),
        pltpu.VMEM((2, chains, 1, t), F32),
        pltpu.VMEM((chains, dv + SUM_ROWS, t), F32),
    ]


def _attend_result(acc_ref, c, dv):
    return acc_ref[c, 0:dv, :] / acc_ref[c, dv:dv + 1, :]


def _key_tile(ref, j, cols=None):
    t = ATT_TILE
    rows = slice(j * t, (j + 1) * t) if isinstance(j, int) else pl.ds(pl.multiple_of(j * t, t), t)
    return ref[0, rows, :] if cols is None else ref[0, rows, cols]


def _transpose_tiles(src_ref, dst_ref, n_tiles, groups):
    t = ATT_TILE
    eye = _eye(LANES)
    for g in range(groups):
        for j in range(n_tiles):
            blk = src_ref[0, j * t:(j + 1) * t, g * LANES:(g + 1) * LANES]
            dst_ref[g, j, 0:LANES, :] = _nt(eye, blk).astype(BF16)
            dst_ref[g, j, LANES:LANES + SUM_ROWS, :] = jnp.ones((SUM_ROWS, t), BF16)


def _diff_kernel(q_ref, k_ref, v_ref, bias_ref, dl_ref, g_ref, o_ref,
                 vt_ref, qs_ref, s_ref, m_ref, a_ref, acc_ref, *, lam_init, n_tiles, hp):
    t = ATT_TILE
    i = pl.program_id(2)

    @pl.when(i == 0)
    def _():
        _transpose_tiles(v_ref, vt_ref, n_tiles, hp)

    scale = HEAD_DIM ** -0.5 * LOG2E
    eye_l = _eye(LANES)
    for h in range(hp):
        q = q_ref[0, :, h * LANES:(h + 1) * LANES]
        for half in range(2):
            qs_ref[2 * h + half] = _transposed(_lane_group_mask(q, half, HEAD_DIM) * scale, eye_l)

    def logits(j, c, ctx):
        h = c // 2
        kblk = _key_tile(k_ref, j, slice(h * LANES, (h + 1) * LANES))
        return _nn(kblk, qs_ref[c]) + bias_ref[h, jnp.minimum(i - j, 2)]

    _attend(i + 1, 2 * hp, lambda j: None, logits, lambda j, c: vt_ref[c // 2, j],
            s_ref, m_ref, a_ref, acc_ref)

    dl = dl_ref[...]
    lam = (jnp.exp(jnp.sum(dl[0:1] * dl[1:2], keepdims=True))
           - jnp.exp(jnp.sum(dl[2:3] * dl[3:4], keepdims=True)) + lam_init)
    eye = _eye(t)
    for h in range(hp):
        o = (_attend_result(acc_ref, 2 * h, DIFF_V_DIM)
             - lam * _attend_result(acc_ref, 2 * h + 1, DIFF_V_DIM))
        ms = jnp.mean(o * o, axis=0, keepdims=True)
        o = o * lax.rsqrt(ms + EPS) * g_ref[...] * (1.0 - lam_init)
        o_ref[0, :, h * LANES:(h + 1) * LANES] = _nt(eye, o.astype(BF16)).astype(o_ref.dtype)


def _diff_attention(p3, bias_tiles, diff_lambda_l, subln_g, lam_init):
    bsz, seq, _ = p3.shape
    t = ATT_TILE
    n_tiles = seq // t
    hp = DIFF_HEADS_PER_STEP
    w = hp * LANES
    kern = functools.partial(_diff_kernel, lam_init=lam_init, n_tiles=n_tiles, hp=hp)
    return pl.pallas_call(
        kern,
        grid=(bsz, DIFF_HEADS // hp, n_tiles),
        in_specs=[
            pl.BlockSpec((1, t, w), lambda b, h, i: (b, i, BLK_AQ // hp + h)),
            pl.BlockSpec((1, seq, w), lambda b, h, i: (b, 0, BLK_AK // hp + h)),
            pl.BlockSpec((1, seq, w), lambda b, h, i: (b, 0, BLK_AV // hp + h)),
            pl.BlockSpec((hp, 3, t, t), lambda b, h, i: (h, 0, 0, 0)),
            pl.BlockSpec((4, HEAD_DIM), lambda b, h, i: (0, 0)),
            pl.BlockSpec((DIFF_V_DIM, 1), lambda b, h, i: (0, 0)),
        ],
        out_specs=pl.BlockSpec((1, t, w), lambda b, h, i: (b, i, h)),
        out_shape=jax.ShapeDtypeStruct((bsz, seq, DIFF_HEADS * DIFF_V_DIM), BF16),
        scratch_shapes=[
            pltpu.VMEM((hp, n_tiles, LANES + SUM_ROWS, t), BF16),
            pltpu.VMEM((2 * hp, LANES, t), BF16),
        ] + _attend_scratch(2 * hp, DIFF_V_DIM),
        compiler_params=pltpu.CompilerParams(
            dimension_semantics=("parallel", "parallel", "arbitrary"),
            vmem_limit_bytes=VMEM_LIMIT),
        name="diff_attn",
    )(p3, p3, p3, bias_tiles, diff_lambda_l, subln_g.reshape(DIFF_V_DIM, 1))


def _dsa_kernel(q_ref, kv_ref, iq_ref, ik_ref, iw_ref, bias_ref, g_ref, wuv_ref, o_ref,
                kvn_ref, kvt_ref, keys_ref, byte_ref, cand_ref, thr_ref, cut_ref, need_ref, tied_ref,
                iqh_ref, qt_ref,
                s_ref, m_ref, a_ref, acc_ref, *, n_tiles, topk):
    t = ATT_TILE
    i = pl.program_id(1)
    n_live = i + 1

    @pl.when(i == 0)
    def _():
        kv = kv_ref[0].astype(F32)
        ms = jnp.mean(kv * kv, axis=-1, keepdims=True)
        kvn_ref[0] = (kv * lax.rsqrt(ms + EPS) * g_ref[...]).astype(BF16)
        _transpose_tiles(kvn_ref, kvt_ref, n_tiles, 1)

    groups = LANES // IDX_DIM
    eye_l = _eye(LANES)
    for h in range(IDX_HEADS):
        blk = iq_ref[0, :, (h // groups) * LANES:(h // groups + 1) * LANES]
        iqh_ref[h] = _transposed(_lane_group_mask(blk, h % groups, IDX_DIM), eye_l)
    for h in range(DSA_HEADS):
        qt_ref[h] = _nt(eye_l, q_ref[0, :, h * LANES:(h + 1) * LANES]).astype(BF16)
    sel_rows = lax.broadcasted_iota(jnp.int32, (IDX_HEADS, LANES), 0)
    sel_cols = lax.broadcasted_iota(jnp.int32, (IDX_HEADS, LANES), 1)
    pick = jnp.where(sel_rows == sel_cols, 1.0, 0.0).astype(BF16)
    iw_t = _nt(pick, iw_ref[0])
    idx_scale = (IDX_HEADS ** -0.5) * (IDX_DIM ** -0.5)
    row = lax.broadcasted_iota(jnp.int32, (t, t), 0)
    col = lax.broadcasted_iota(jnp.int32, (t, t), 1)

    def score_tile(j, diagonal):
        ik = _key_tile(ik_ref, j)
        sc = jnp.zeros((t, t), F32)
        for h in range(IDX_HEADS):
            sc = sc + jnp.maximum(_nn(ik, iqh_ref[h]), 0.0) * iw_t[h:h + 1, :]
        sc = sc * idx_scale
        bits = pltpu.bitcast(sc, jnp.int32)
        key = bits ^ ((bits >> 31) & 0x7FFFFFFF)
        top = (key >> 24) + 128
        if diagonal:
            valid = col >= row
            key = jnp.where(valid, key, INT_MIN)
            top = jnp.where(valid, top, -1)
        keys_ref[j] = key
        cand_ref[j] = top.astype(F32).astype(BF16)
        for b in range(3):
            byte_ref[b, j] = ((key >> (16 - 8 * b)) & 0xFF).astype(F32).astype(BF16)

    one_b, zero_b = jnp.ones((), BF16), jnp.zeros((), BF16)
    packed_rows = 16

    def radix_select(live):
        for j in range(live):
            score_tile(j, diagonal=(j == live - 1))

        def count(hit_fn):
            part = None
            for j in range(live):
                hit = jnp.where(hit_fn(cand_ref[j]), one_b, zero_b)
                for r in range(t // packed_rows):
                    rows = hit[packed_rows * r:packed_rows * (r + 1)]
                    part = rows if part is None else part + rows
            return jnp.sum(part.astype(F32), axis=0, keepdims=True)

        need = jnp.full((1, t), topk, F32)
        thr = jnp.zeros((1, t), jnp.int32)
        for b in range(4):
            def bit_body(it, val, need=need):
                cand = val | lax.shift_left(jnp.int32(1), 7 - it)
                cand_b = cand.astype(F32).astype(BF16)
                return jnp.where(count(lambda x: x >= cand_b) >= need, cand, val)

            val = lax.fori_loop(0, 8, bit_body, jnp.zeros((1, t), jnp.int32))
            val_b = val.astype(F32).astype(BF16)
            need = need - count(lambda x: x > val_b)
            if b < 3:
                for j in range(live):
                    cand_ref[j] = jnp.where(cand_ref[j] == val_b, byte_ref[b, j], -one_b)
            else:
                tied_ref[...] = count(lambda x: x == val_b)
            piece = val - (128 if b == 0 else 0)
            thr = thr | lax.shift_left(piece, 24 - 8 * b)
        thr_ref[...] = thr
        need_ref[...] = need

    assert (t // packed_rows) * n_tiles <= 256
    for live in range(1, n_tiles + 1):
        pl.when(n_live == live)(functools.partial(radix_select, live))

    thr = thr_ref[...]
    need = need_ref[...]
    has_rank = thr > INT_MIN
    excess = jnp.where(has_rank & (tied_ref[...] > need), 1.0, 0.0)
    cut_ref[...] = jnp.where(has_rank, jnp.int32(2 * n_tiles * t), jnp.int32(0))

    @pl.when(jnp.max(excess) > 0.0)
    def _():
        n_bits = (2 * n_tiles * t - 1).bit_length()

        def count_ties_before(cand):
            def cb(j, acc):
                hit = (keys_ref[j] == thr) & ((j * t + row) < cand)
                return acc + jnp.sum(jnp.where(hit, 1.0, 0.0), axis=0, keepdims=True)
            return lax.fori_loop(0, n_live, cb, jnp.zeros((1, t), F32))

        def cut_body(it, cut):
            cand = cut | lax.shift_left(jnp.int32(1), n_bits - 1 - it)
            return jnp.where(count_ties_before(cand) <= need, cand, cut)

        cut = lax.fori_loop(0, n_bits, cut_body, jnp.zeros((1, t), jnp.int32))
        cut_ref[...] = jnp.where(has_rank, cut, jnp.int32(0))

    scale = DSA_LATENT ** -0.5 * LOG2E

    def tile_ctx(j):
        key = keys_ref[j]
        thr_v = thr_ref[...]
        sel = (key > thr_v) | ((key == thr_v) & ((j * t + row) < cut_ref[...]))
        return sel, _key_tile(kvn_ref, j), jnp.minimum(i - j, 2)

    def logits(j, h, ctx):
        sel, kvb, kind = ctx
        return jnp.where(sel, _nn(kvb, qt_ref[h]) * scale + bias_ref[h, kind], NEG)

    _attend(n_live, DSA_HEADS, tile_ctx, logits, lambda j, h: kvt_ref[0, j],
            s_ref, m_ref, a_ref, acc_ref)

    y_t = jnp.zeros((DSA_HEADS * DSA_V_DIM, t), F32)
    for h in range(DSA_HEADS):
        o_h = _attend_result(acc_ref, h, DSA_LATENT).astype(BF16)
        y_t = y_t + _nn(wuv_ref[h], o_h)
    o_ref[0] = _nt(_eye(t), y_t.astype(BF16)).astype(o_ref.dtype)


def _dsa_attention(p3, bias_tiles, kv_norm_g, wuv_t_pad):
    bsz, seq, _ = p3.shape
    t = ATT_TILE
    n_tiles = seq // t
    topk = min(DSA_TOPK_MAX, seq // 4)
    kern = functools.partial(_dsa_kernel, n_tiles=n_tiles, topk=float(topk))
    n_out = DSA_HEADS * DSA_V_DIM
    return pl.pallas_call(
        kern,
        grid=(bsz, n_tiles),
        in_specs=[
            pl.BlockSpec((1, t, 4 * LANES), lambda b, i: (b, i, BLK_BQ // 4)),
            pl.BlockSpec((1, seq, LANES), lambda b, i: (b, 0, BLK_BKV)),
            pl.BlockSpec((1, t, 2 * LANES), lambda b, i: (b, i, BLK_BIQ // 2)),
            pl.BlockSpec((1, seq, LANES), lambda b, i: (b, 0, BLK_IK)),
            pl.BlockSpec((1, t, LANES), lambda b, i: (b, i, BLK_IW)),
            pl.BlockSpec((DSA_HEADS, 3, t, t), lambda b, i: (DIFF_HEADS // DSA_HEADS, 0, 0, 0)),
            pl.BlockSpec((1, DSA_LATENT), lambda b, i: (0, 0)),
            pl.BlockSpec((DSA_HEADS, n_out, DSA_LATENT), lambda b, i: (0, 0, 0)),
        ],
        out_specs=pl.BlockSpec((1, t, n_out), lambda b, i: (b, i, 0)),
        out_shape=jax.ShapeDtypeStruct((bsz, seq, n_out), BF16),
        scratch_shapes=[
            pltpu.VMEM((1, seq, DSA_LATENT), BF16),
            pltpu.VMEM((1, n_tiles, DSA_LATENT + SUM_ROWS, t), BF16),
            pltpu.VMEM((n_tiles, t, t), jnp.int32),
            pltpu.VMEM((3, n_tiles, t, t), BF16),
            pltpu.VMEM((n_tiles, t, t), BF16),
            pltpu.VMEM((1, t), jnp.int32),
            pltpu.VMEM((1, t), jnp.int32),
            pltpu.VMEM((1, t), F32),
            pltpu.VMEM((1, t), F32),
            pltpu.VMEM((IDX_HEADS, LANES, t), BF16),
            pltpu.VMEM((DSA_HEADS, LANES, t), BF16),
        ] + _attend_scratch(DSA_HEADS, DSA_LATENT),
        compiler_params=pltpu.CompilerParams(
            dimension_semantics=("parallel", "arbitrary"), vmem_limit_bytes=VMEM_LIMIT),
        name="dsa_attn",
    )(p3, p3, p3, p3, p3, bias_tiles, kv_norm_g.reshape(1, DSA_LATENT), wuv_t_pad)


def _moba_kernel(q_ref, k_ref, v_ref, bias_ref, o_ref,
                 vt_ref, kmean_ref, selb_ref, qs_ref,
                 s_ref, m_ref, a_ref, acc_ref, *, n_tiles, topb):
    t = ATT_TILE
    i = pl.program_id(1)
    pairs = MOBA_HEADS // 2

    @pl.when(i == 0)
    def _():
        _transpose_tiles(v_ref, vt_ref, n_tiles, pairs)
        for n in range(n_tiles):
            kb = k_ref[0, n * t:(n + 1) * t, :].astype(F32)
            kmean_ref[n:n + 1, :] = jnp.mean(kb, axis=0, keepdims=True)

    scale = HEAD_DIM ** -0.5
    eye_l = _eye(LANES)
    blk = lax.broadcasted_iota(jnp.int32, (n_tiles, t), 0)
    own = jnp.full((1, t), i, jnp.int32)
    for hd in range(MOBA_HEADS):
        g, half = hd // 2, hd % 2
        q_f = _lane_group_mask(q_ref[0, :, g * LANES:(g + 1) * LANES], half, HEAD_DIM)
        qs_ref[hd] = _transposed(q_f * (scale * LOG2E), eye_l)
        kmean = kmean_ref[:, g * LANES:(g + 1) * LANES].astype(BF16)
        gate = _nt(kmean, q_f.astype(BF16))
        for n in range(n_tiles):
            gn = gate[n:n + 1, :]
            ahead = (gate > gn) | ((gate == gn) & (blk < n))
            ahead = ahead & (blk < i)
            rank = jnp.sum(jnp.where(ahead, 1.0, 0.0), axis=0, keepdims=True)
            chosen = ((rank < topb) & (own > n)) | (own == n)
            selb_ref[hd, n] = jnp.broadcast_to(jnp.where(chosen, 0.0, NEG), (8, t))

    def logits(j, hd, ctx):
        g = hd // 2
        kblk = _key_tile(k_ref, j, slice(g * LANES, (g + 1) * LANES))
        return (_nn(kblk, qs_ref[hd]) + bias_ref[hd, jnp.minimum(i - j, 2)]
                + selb_ref[hd, j][0:1, :])

    _attend(i + 1, MOBA_HEADS, lambda j: None, logits, lambda j, hd: vt_ref[hd // 2, j],
            s_ref, m_ref, a_ref, acc_ref)

    rows = lax.broadcasted_iota(jnp.int32, (LANES, t), 0)
    eye = _eye(t)
    for g in range(pairs):
        o_lo = _attend_result(acc_ref, 2 * g, LANES)
        o_hi = _attend_result(acc_ref, 2 * g + 1, LANES)
        o = jnp.where(rows < HEAD_DIM, o_lo, o_hi)
        o_ref[0, :, g * LANES:(g + 1) * LANES] = _nt(eye, o.astype(BF16)).astype(o_ref.dtype)


def _moba_attention(p3, bias_tiles):
    bsz, seq, _ = p3.shape
    t = ATT_TILE
    n_tiles = seq // t
    topb = min(MOBA_TOPK_MAX, n_tiles)
    kern = functools.partial(_moba_kernel, n_tiles=n_tiles, topb=float(topb))
    w = MOBA_HEADS * HEAD_DIM
    blocks = w // LANES
    first = (DIFF_HEADS + DSA_HEADS) // MOBA_HEADS
    return pl.pallas_call(
        kern,
        grid=(bsz, n_tiles),
        in_specs=[
            pl.BlockSpec((1, t, w), lambda b, i: (b, i, BLK_CQ // blocks)),
            pl.BlockSpec((1, seq, w), lambda b, i: (b, 0, BLK_CK // blocks)),
            pl.BlockSpec((1, seq, w), lambda b, i: (b, 0, BLK_CV // blocks)),
            pl.BlockSpec((MOBA_HEADS, 3, t, t), lambda b, i: (first, 0, 0, 0)),
        ],
        out_specs=pl.BlockSpec((1, t, w), lambda b, i: (b, i, 0)),
        out_shape=jax.ShapeDtypeStruct((bsz, seq, w), BF16),
        scratch_shapes=[
            pltpu.VMEM((blocks, n_tiles, LANES + SUM_ROWS, t), BF16),
            pltpu.VMEM((n_tiles, w), F32),
            pltpu.VMEM((MOBA_HEADS, n_tiles, 8, t), F32),
            pltpu.VMEM((MOBA_HEADS, LANES, t), BF16),
        ] + _attend_scratch(MOBA_HEADS, LANES),
        compiler_params=pltpu.CompilerParams(
            dimension_semantics=("parallel", "arbitrary"), vmem_limit_bytes=VMEM_LIMIT),
        name="moba_attn",
    )(p3, p3, p3, bias_tiles)


def _merge_kernel(x_ref, g_ref, gb_ref, oa_ref, ob_ref, oc_ref, wa_ref, wb_ref, wc_ref, wo_ref,
                  mod_ref, o_ref):
    d = D_MODEL
    gates = jax.nn.sigmoid(g_ref[...].astype(F32) + gb_ref[...])
    merged = (gates[:, 0:d] * _nn(oa_ref[...], wa_ref[...])
              + gates[:, d:2 * d] * _nn(ob_ref[...], wb_ref[...])
              + gates[:, 2 * d:3 * d] * _nn(oc_ref[...], wc_ref[...]))
    z = _nn(merged.astype(BF16), wo_ref[...])
    o_ref[...] = x_ref[...] + mod_ref[0, 2:3, :] * z


def _merge(x2, p2, gate_b, oa, ob, oc, wa, wb, wc, wo, mod_l, seq):
    m, d = x2.shape
    tm = min(512, seq)
    full = lambda a: pl.BlockSpec(a.shape, lambda i: (0, 0))
    return pl.pallas_call(
        _merge_kernel,
        grid=(m // tm,),
        in_specs=[
            pl.BlockSpec((tm, d), lambda i: (i, 0)),
            pl.BlockSpec((tm, 3 * d), lambda i: (i, BLK_G)),
            pl.BlockSpec((1, 3 * d), lambda i: (0, 0)),
            pl.BlockSpec((tm, oa.shape[1]), lambda i: (i, 0)),
            pl.BlockSpec((tm, ob.shape[1]), lambda i: (i, 0)),
            pl.BlockSpec((tm, oc.shape[1]), lambda i: (i, 0)),
            full(wa), full(wb), full(wc), full(wo),
            pl.BlockSpec((1, 6, d), lambda i: ((i * tm) // seq, 0, 0)),
        ],
        out_specs=pl.BlockSpec((tm, d), lambda i: (i, 0)),
        out_shape=jax.ShapeDtypeStruct((m, d), F32),
        compiler_params=pltpu.CompilerParams(
            dimension_semantics=("parallel",), vmem_limit_bytes=VMEM_LIMIT),
        name="merge",
    )(x2, p2, gate_b.reshape(1, 3 * d), oa, ob, oc, wa, wb, wc, wo, mod_l)


def _mlp_kernel(x_ref, mod_ref, g_ref, w1_ref, w2_ref, gf_ref, o_ref, *, final, tf):
    x = x_ref[...]
    u = _norm_modulate(x, g_ref[...], mod_ref[0, 3:4, :], mod_ref[0, 4:5, :]).astype(BF16)
    acc = None
    for f in range(w1_ref.shape[1] // tf):
        h = jnp.square(jnp.maximum(_nn(u, w1_ref[:, f * tf:(f + 1) * tf]), 0.0))
        part = _nn(h.astype(BF16), w2_ref[f * tf:(f + 1) * tf, :])
        acc = part if acc is None else acc + part
    y = x + mod_ref[0, 5:6, :] * acc
    if final:
        ms = jnp.mean(y * y, axis=-1, keepdims=True)
        y = y * lax.rsqrt(ms + EPS) * gf_ref[...]
    o_ref[...] = y


def _mlp(x2, mod_l, g, w1, w2, g_final, seq, final):
    m, d = x2.shape
    dff = w1.shape[1]
    tm = min(512, seq)
    resident = lambda shape: pl.BlockSpec(shape, lambda i: (0, 0), pipeline_mode=pl.Buffered(1))
    return pl.pallas_call(
        functools.partial(_mlp_kernel, final=final, tf=1024),
        grid=(m // tm,),
        in_specs=[
            pl.BlockSpec((tm, d), lambda i: (i, 0)),
            pl.BlockSpec((1, 6, d), lambda i: ((i * tm) // seq, 0, 0)),
            pl.BlockSpec((1, d), lambda i: (0, 0)),
            resident((d, dff)),
            resident((dff, d)),
            pl.BlockSpec((1, d), lambda i: (0, 0)),
        ],
        out_specs=pl.BlockSpec((tm, d), lambda i: (i, 0)),
        out_shape=jax.ShapeDtypeStruct((m, d), F32),
        compiler_params=pltpu.CompilerParams(
            dimension_semantics=("parallel",), vmem_limit_bytes=VMEM_LIMIT),
        name="mlp",
    )(x2, mod_l, g.reshape(1, d), w1, w2, g_final.reshape(1, d))


def kernel(x, c, rel_bias, ada_w, ada_b, norm_mix, w_in, gate_b, diff_lambda, diff_subln,
           dsa_kv_norm, dsa_w_uv, w_br_a, w_br_b, w_br_c, w_o, norm_mlp, w_ff1, w_ff2,
           norm_final):
    bsz, seq, d = x.shape
    depth = w_in.shape[0]
    assert seq % ATT_TILE == 0 and 2 * ATT_TILE > MAX_DISTANCE

    w_in_p = _pack_columns(w_in.astype(BF16))
    n_out = DSA_HEADS * DSA_V_DIM
    wuv_t = jnp.transpose(dsa_w_uv, (0, 1, 3, 2))
    wuv_t_pad = jnp.zeros((depth, DSA_HEADS, n_out, DSA_LATENT), F32)
    for h in range(DSA_HEADS):
        wuv_t_pad = wuv_t_pad.at[:, h, h * DSA_V_DIM:(h + 1) * DSA_V_DIM, :].set(wuv_t[:, h])
    wuv_t_pad = wuv_t_pad.astype(BF16)
    wa, wb, wc, wo = (w.astype(BF16) for w in (w_br_a, w_br_b, w_br_c, w_o))
    w1, w2 = w_ff1.astype(BF16), w_ff2.astype(BF16)

    bias_tiles = _bias_tiles(rel_bias)
    mod = _ada(c, ada_w, ada_b).reshape(depth, bsz, 6, d)

    x2 = x.reshape(bsz * seq, d)
    for l in range(depth):
        lam_init = 0.8 - 0.6 * math.exp(-0.3 * l)
        p2 = _inproj(x2, mod[l], norm_mix[l], w_in_p[l], seq)
        p3 = p2.reshape(bsz, seq, PACKED_COLS)
        oa = _diff_attention(p3, bias_tiles, diff_lambda[l], diff_subln[l], lam_init)
        ob = _dsa_attention(p3, bias_tiles, dsa_kv_norm[l], wuv_t_pad[l])
        oc = _moba_attention(p3, bias_tiles)
        x2 = _merge(x2, p2, gate_b[l],
                    oa.reshape(bsz * seq, -1), ob.reshape(bsz * seq, -1), oc.reshape(bsz * seq, -1),
                    wa[l], wb[l], wc[l], wo[l], mod[l], seq)
        x2 = _mlp(x2, mod[l], norm_mlp[l], w1[l], w2[l], norm_final, seq, final=(l == depth - 1))
    return x2.reshape(bsz, seq, d)
```

```python
import functools
import math

import jax
import jax.numpy as jnp
from jax import lax
from jax.experimental import pallas as pl
from jax.experimental.pallas import tpu as pltpu

F32 = jnp.float32
BF16 = jnp.bfloat16

D_MODEL = 1024
HEAD_DIM = 64
DIFF_HEADS = 4
DIFF_V_DIM = 2 * HEAD_DIM
DSA_HEADS = 4
DSA_LATENT = 128
DSA_V_DIM = 64
IDX_HEADS = 8
IDX_DIM = 32
DSA_TOPK_MAX = 256
MOBA_HEADS = 4
MOBA_BLOCK = 256
MOBA_TOPK_MAX = 3
N_BUCKETS = 32
MAX_DISTANCE = 128
N_BIAS_HEADS = DIFF_HEADS + DSA_HEADS + MOBA_HEADS
D_FF = 4 * D_MODEL
EPS = 1e-6

LANES = 128
ATT_TILE = MOBA_BLOCK
NEG = -1e30
INT_MIN = -2 ** 31
VMEM_LIMIT = 52 * 1024 * 1024
DIFF_HEADS_PER_STEP = 4
SUM_ROWS = 16
LOG2E = math.log2(math.e)

_O_AQ, _O_AK, _O_AV, _O_BQ, _O_BKV, _O_BIQ, _O_BIK, _O_BIW, _O_CQ, _O_CK, _O_CV, _O_G = (
    0, 512, 1024, 1536, 2048, 2176, 2432, 2464, 2472, 2728, 2984, 3240)

BLK_G, BLK_AQ, BLK_AK, BLK_AV, BLK_BQ, BLK_BIQ, BLK_BKV, BLK_IK, BLK_CQ, BLK_CK, BLK_CV, BLK_IW = (
    0, 24, 28, 32, 36, 40, 42, 43, 44, 46, 48, 50)
N_BLKS = 51
PACKED_COLS = N_BLKS * LANES


def _pack_columns(w):
    seg = lambda off, n: w[..., off:off + n]
    ik = seg(_O_BIK, IDX_DIM)
    parts = [
        seg(_O_G, 3 * D_MODEL), seg(_O_AQ, 512), seg(_O_AK, 512), seg(_O_AV, 512),
        seg(_O_BQ, 512), seg(_O_BIQ, 256), seg(_O_BKV, 128),
        ik, ik, ik, ik,
        seg(_O_CQ, 256), seg(_O_CK, 256), seg(_O_CV, 256),
        seg(_O_BIW, IDX_HEADS),
        jnp.zeros(w.shape[:-1] + (LANES - IDX_HEADS,), w.dtype),
    ]
    out = jnp.concatenate(parts, axis=-1)
    assert out.shape[-1] == PACKED_COLS
    return out


def _nt(a, b):
    return lax.dot_general(a, b, (((1,), (1,)), ((), ())), preferred_element_type=F32)


def _nn(a, b):
    return jnp.dot(a, b, preferred_element_type=F32)


def _eye(n):
    r = lax.broadcasted_iota(jnp.int32, (n, n), 0)
    c = lax.broadcasted_iota(jnp.int32, (n, n), 1)
    return jnp.where(r == c, 1.0, 0.0).astype(BF16)


def _transposed(x_f32, eye):
    return _nt(eye, x_f32.astype(BF16)).astype(BF16)


def _lane_group_mask(x_bf16, group, width):
    lane = lax.broadcasted_iota(jnp.int32, x_bf16.shape, 1)
    keep = (lane >= group * width) & (lane < (group + 1) * width)
    return jnp.where(keep, x_bf16.astype(F32), 0.0)


def _ada_kernel(c_ref, w_ref, b_ref, o_ref):
    c = c_ref[...]
    cond = c * jax.nn.sigmoid(c)
    o_ref[0] = _nn(cond, w_ref[0]) + b_ref[0]


def _ada(c, ada_w, ada_b):
    depth, d, n = ada_w.shape
    bsz = c.shape[0]
    tn = 1536
    return pl.pallas_call(
        _ada_kernel,
        grid=(depth, n // tn),
        in_specs=[
            pl.BlockSpec((bsz, d), lambda l, j: (0, 0)),
            pl.BlockSpec((1, d, tn), lambda l, j: (l, 0, j)),
            pl.BlockSpec((1, 1, tn), lambda l, j: (l, 0, j)),
        ],
        out_specs=pl.BlockSpec((1, bsz, tn), lambda l, j: (l, 0, j)),
        out_shape=jax.ShapeDtypeStruct((depth, bsz, n), F32),
        compiler_params=pltpu.CompilerParams(
            dimension_semantics=("arbitrary", "arbitrary"), vmem_limit_bytes=VMEM_LIMIT),
        name="ada_mod",
    )(c, ada_w, ada_b.reshape(depth, 1, n))


def _t5_bucket(dist):
    max_exact = N_BUCKETS // 2
    n = jnp.maximum(dist, 0)
    nf = jnp.maximum(n, 1).astype(F32)
    large = max_exact + (jnp.log(nf / max_exact) / math.log(MAX_DISTANCE / max_exact)
                         * (N_BUCKETS - max_exact)).astype(jnp.int32)
    large = jnp.minimum(large, N_BUCKETS - 1)
    return jnp.where(n < max_exact, n, large)


def _bias_kernel(rb_ref, o_ref):
    t = ATT_TILE
    h = pl.program_id(0)
    kk = lax.broadcasted_iota(jnp.int32, (t, t), 0)
    qq = lax.broadcasted_iota(jnp.int32, (t, t), 1)
    for kind in range(3):
        dist = jnp.full((t, t), 2 * t, jnp.int32) if kind == 2 else kind * t + qq - kk
        bucket = _t5_bucket(dist)
        tile = jnp.zeros((t, t), F32)
        for b in range(N_BUCKETS):
            tile = jnp.where(bucket == b, rb_ref[b, h], tile)
        o_ref[0, kind] = jnp.where(dist >= 0, tile * LOG2E, NEG)


def _bias_tiles(rel_bias):
    t = ATT_TILE
    heads = rel_bias.shape[1]
    return pl.pallas_call(
        _bias_kernel,
        grid=(heads,),
        in_specs=[pl.BlockSpec(memory_space=pltpu.SMEM)],
        out_specs=pl.BlockSpec((1, 3, t, t), lambda h: (h, 0, 0, 0)),
        out_shape=jax.ShapeDtypeStruct((heads, 3, t, t), F32),
        compiler_params=pltpu.CompilerParams(dimension_semantics=("arbitrary",)),
        name="bias_tiles",
    )(rel_bias.astype(F32))


def _norm_modulate(x, g, shift, scale):
    ms = jnp.mean(x * x, axis=-1, keepdims=True)
    y = x * lax.rsqrt(ms + EPS) * g
    return y * (1.0 + scale) + shift


def _inproj_kernel(x_ref, mod_ref, g_ref, w_ref, o_ref, *, tn):
    u = _norm_modulate(x_ref[...], g_ref[...], mod_ref[0, 0:1, :], mod_ref[0, 1:2, :])
    u = u.astype(BF16)
    for j in range(w_ref.shape[1] // tn):
        cols = slice(j * tn, (j + 1) * tn)
        o_ref[:, cols] = _nn(u, w_ref[:, cols]).astype(o_ref.dtype)


def _inproj(x2, mod_l, g, w_packed, seq):
    m, d = x2.shape
    n = w_packed.shape[1]
    tm = min(512, seq)
    return pl.pallas_call(
        functools.partial(_inproj_kernel, tn=n // 3),
        grid=(m // tm,),
        in_specs=[
            pl.BlockSpec((tm, d), lambda i: (i, 0)),
            pl.BlockSpec((1, 6, d), lambda i: ((i * tm) // seq, 0, 0)),
            pl.BlockSpec((1, d), lambda i: (0, 0)),
            pl.BlockSpec((d, n), lambda i: (0, 0), pipeline_mode=pl.Buffered(1)),
        ],
        out_specs=pl.BlockSpec((tm, n), lambda i: (i, 0)),
        out_shape=jax.ShapeDtypeStruct((m, n), BF16),
        compiler_params=pltpu.CompilerParams(
            dimension_semantics=("parallel",), vmem_limit_bytes=VMEM_LIMIT),
        name="in_proj",
    )(x2, mod_l, g.reshape(1, d), w_packed)


def _attend_unrolled(n_live, n_tiles, chains, tile_ctx, logits, values, s_ref, m_ref, a_ref, acc_ref):
    def variant(live):
        m_ref[...] = jnp.full(m_ref.shape, NEG, F32)
        acc_ref[...] = jnp.zeros(acc_ref.shape, F32)
        for j in range(live + 1):
            if j < live:
                ctx = tile_ctx(j)
                kind = min(live - 1 - j, 2)
            for c in range(chains):
                if j < live:
                    s = logits(j, c, ctx, kind)
                    m_old = m_ref[1 - j % 2, c]
                    m_new = jnp.maximum(m_old, jnp.max(s, axis=0, keepdims=True))
                    s_ref[j % 2, c] = s
                    a_ref[j % 2, c] = jnp.exp2(m_old - m_new)
                    m_ref[j % 2, c] = m_new
                if j > 0:
                    slot = (j - 1) % 2
                    p = jnp.exp2(s_ref[slot, c] - m_ref[slot, c])
                    acc_ref[c] = (a_ref[slot, c] * acc_ref[c]
                                  + _nn(values(j - 1, c), p.astype(BF16)))

    for live in range(1, n_tiles + 1):
        pl.when(n_live == live)(functools.partial(variant, live))


def _attend_scratch(chains, dv):
    t = ATT_TILE
    return [
        pltpu.VMEM((2, chains, t, t), F32),
        pltpu.VMEM((2, chains, 1, t), F32),
        pltpu.VMEM((2, chains, 1, t), F32),
        pltpu.VMEM((chains, dv + SUM_ROWS, t), F32),
    ]


def _attend_result(acc_ref, c, dv):
    return acc_ref[c, 0:dv, :] / acc_ref[c, dv:dv + 1, :]


def _key_tile(ref, j, cols=None):
    t = ATT_TILE
    rows = slice(j * t, (j + 1) * t) if isinstance(j, int) else pl.ds(pl.multiple_of(j * t, t), t)
    return ref[0, rows, :] if cols is None else ref[0, rows, cols]


def _transpose_tiles(src_ref, dst_ref, n_tiles, groups):
    t = ATT_TILE
    eye = _eye(LANES)
    for g in range(groups):
        for j in range(n_tiles):
            blk = src_ref[0, j * t:(j + 1) * t, g * LANES:(g + 1) * LANES]
            dst_ref[g, j, 0:LANES, :] = _nt(eye, blk).astype(BF16)
            dst_ref[g, j, LANES:LANES + SUM_ROWS, :] = jnp.ones((SUM_ROWS, t), BF16)


def _diff_kernel(q_ref, k_ref, v_ref, bias_ref, dl_ref, g_ref, o_ref,
                 vt_ref, qs_ref, s_ref, m_ref, a_ref, acc_ref, *, lam_init, n_tiles, hp):
    t = ATT_TILE
    i = pl.program_id(2)

    @pl.when(i == 0)
    def _():
        _transpose_tiles(v_ref, vt_ref, n_tiles, hp)

    scale = HEAD_DIM ** -0.5 * LOG2E
    eye_l = _eye(LANES)
    for h in range(hp):
        q = q_ref[0, :, h * LANES:(h + 1) * LANES]
        for half in range(2):
            qs_ref[2 * h + half] = _transposed(_lane_group_mask(q, half, HEAD_DIM) * scale, eye_l)

    def logits(j, c, ctx, kind):
        h = c // 2
        kblk = _key_tile(k_ref, j, slice(h * LANES, (h + 1) * LANES))
        return _nn(kblk, qs_ref[c]) + bias_ref[h, kind]

    _attend_unrolled(i + 1, n_tiles, 2 * hp, lambda j: None, logits,
                     lambda j, c: vt_ref[c // 2, j], s_ref, m_ref, a_ref, acc_ref)

    dl = dl_ref[...]
    lam = (jnp.exp(jnp.sum(dl[0:1] * dl[1:2], keepdims=True))
           - jnp.exp(jnp.sum(dl[2:3] * dl[3:4], keepdims=True)) + lam_init)
    eye = _eye(t)
    for h in range(hp):
        o = (_attend_result(acc_ref, 2 * h, DIFF_V_DIM)
             - lam * _attend_result(acc_ref, 2 * h + 1, DIFF_V_DIM))
        ms = jnp.mean(o * o, axis=0, keepdims=True)
        o = o * lax.rsqrt(ms + EPS) * g_ref[...] * (1.0 - lam_init)
        o_ref[0, :, h * LANES:(h + 1) * LANES] = _nt(eye, o.astype(BF16)).astype(o_ref.dtype)


def _diff_attention(p3, bias_tiles, diff_lambda_l, subln_g, lam_init):
    bsz, seq, _ = p3.shape
    t = ATT_TILE
    n_tiles = seq // t
    hp = DIFF_HEADS_PER_STEP
    w = hp * LANES
    kern = functools.partial(_diff_kernel, lam_init=lam_init, n_tiles=n_tiles, hp=hp)
    return pl.pallas_call(
        kern,
        grid=(bsz, DIFF_HEADS // hp, n_tiles),
        in_specs=[
            pl.BlockSpec((1, t, w), lambda b, h, i: (b, i, BLK_AQ // hp + h)),
            pl.BlockSpec((1, seq, w), lambda b, h, i: (b, 0, BLK_AK // hp + h)),
            pl.BlockSpec((1, seq, w), lambda b, h, i: (b, 0, BLK_AV // hp + h)),
            pl.BlockSpec((hp, 3, t, t), lambda b, h, i: (h, 0, 0, 0)),
            pl.BlockSpec((4, HEAD_DIM), lambda b, h, i: (0, 0)),
            pl.BlockSpec((DIFF_V_DIM, 1), lambda b, h, i: (0, 0)),
        ],
        out_specs=pl.BlockSpec((1, t, w), lambda b, h, i: (b, i, h)),
        out_shape=jax.ShapeDtypeStruct((bsz, seq, DIFF_HEADS * DIFF_V_DIM), BF16),
        scratch_shapes=[
            pltpu.VMEM((hp, n_tiles, LANES + SUM_ROWS, t), BF16),
            pltpu.VMEM((2 * hp, LANES, t), BF16),
        ] + _attend_scratch(2 * hp, DIFF_V_DIM),
        compiler_params=pltpu.CompilerParams(
            dimension_semantics=("parallel", "parallel", "arbitrary"),
            vmem_limit_bytes=VMEM_LIMIT),
        name="diff_attn",
    )(p3, p3, p3, bias_tiles, diff_lambda_l, subln_g.reshape(DIFF_V_DIM, 1))


def _dsa_kernel(q_ref, kv_ref, iq_ref, ik_ref, iw_ref, bias_ref, g_ref, wuv_ref, o_ref,
                kvn_ref, kvt_ref, keys_ref, byte_ref, cand_ref, thr_ref, cut_ref, need_ref, tied_ref,
                iqh_ref, qt_ref,
                s_ref, m_ref, a_ref, acc_ref, *, n_tiles, topk):
    t = ATT_TILE
    i = pl.program_id(1)
    n_live = i + 1

    @pl.when(i == 0)
    def _():
        kv = kv_ref[0].astype(F32)
        ms = jnp.mean(kv * kv, axis=-1, keepdims=True)
        kvn_ref[0] = (kv * lax.rsqrt(ms + EPS) * g_ref[...]).astype(BF16)
        _transpose_tiles(kvn_ref, kvt_ref, n_tiles, 1)

    groups = LANES // IDX_DIM
    eye_l = _eye(LANES)
    for h in range(IDX_HEADS):
        blk = iq_ref[0, :, (h // groups) * LANES:(h // groups + 1) * LANES]
        iqh_ref[h] = _transposed(_lane_group_mask(blk, h % groups, IDX_DIM), eye_l)
    for h in range(DSA_HEADS):
        qt_ref[h] = _nt(eye_l, q_ref[0, :, h * LANES:(h + 1) * LANES]).astype(BF16)
    sel_rows = lax.broadcasted_iota(jnp.int32, (IDX_HEADS, LANES), 0)
    sel_cols = lax.broadcasted_iota(jnp.int32, (IDX_HEADS, LANES), 1)
    pick = jnp.where(sel_rows == sel_cols, 1.0, 0.0).astype(BF16)
    iw_t = _nt(pick, iw_ref[0])
    idx_scale = (IDX_HEADS ** -0.5) * (IDX_DIM ** -0.5)
    row = lax.broadcasted_iota(jnp.int32, (t, t), 0)
    col = lax.broadcasted_iota(jnp.int32, (t, t), 1)

    def score_tile(j, diagonal):
        ik = _key_tile(ik_ref, j)
        sc = jnp.zeros((t, t), F32)
        for h in range(IDX_HEADS):
            sc = sc + jnp.maximum(_nn(ik, iqh_ref[h]), 0.0) * iw_t[h:h + 1, :]
        sc = sc * idx_scale
        bits = pltpu.bitcast(sc, jnp.int32)
        key = bits ^ ((bits >> 31) & 0x7FFFFFFF)
        top = (key >> 24) + 128
        if diagonal:
            valid = col >= row
            key = jnp.where(valid, key, INT_MIN)
            top = jnp.where(valid, top, -1)
        keys_ref[j] = key
        cand_ref[j] = top.astype(F32).astype(BF16)
        for b in range(3):
            byte_ref[b, j] = ((key >> (16 - 8 * b)) & 0xFF).astype(F32).astype(BF16)

    one_b, zero_b = jnp.ones((), BF16), jnp.zeros((), BF16)
    packed_rows = 16

    def radix_select(live):
        for j in range(live):
            score_tile(j, diagonal=(j == live - 1))

        def count(hit_fn):
            part = None
            for j in range(live):
                hit = jnp.where(hit_fn(cand_ref[j]), one_b, zero_b)
                for r in range(t // packed_rows):
                    rows = hit[packed_rows * r:packed_rows * (r + 1)]
                    part = rows if part is None else part + rows
            return jnp.sum(part.astype(F32), axis=0, keepdims=True)

        need = jnp.full((1, t), topk, F32)
        thr = jnp.zeros((1, t), jnp.int32)
        for b in range(4):
            def bit_body(it, val, need=need):
                cand = val | lax.shift_left(jnp.int32(1), 7 - it)
                cand_b = cand.astype(F32).astype(BF16)
                return jnp.where(count(lambda x: x >= cand_b) >= need, cand, val)

            val = lax.fori_loop(0, 8, bit_body, jnp.zeros((1, t), jnp.int32))
            val_b = val.astype(F32).astype(BF16)
            need = need - count(lambda x: x > val_b)
            if b < 3:
                for j in range(live):
                    cand_ref[j] = jnp.where(cand_ref[j] == val_b, byte_ref[b, j], -one_b)
            else:
                tied_ref[...] = count(lambda x: x == val_b)
            piece = val - (128 if b == 0 else 0)
            thr = thr | lax.shift_left(piece, 24 - 8 * b)
        thr_ref[...] = thr
        need_ref[...] = need

    assert (t // packed_rows) * n_tiles <= 256
    for live in range(1, n_tiles + 1):
        pl.when(n_live == live)(functools.partial(radix_select, live))

    thr = thr_ref[...]
    need = need_ref[...]
    has_rank = thr > INT_MIN
    excess = jnp.where(has_rank & (tied_ref[...] > need), 1.0, 0.0)
    cut_ref[...] = jnp.where(has_rank, jnp.int32(2 * n_tiles * t), jnp.int32(0))

    @pl.when(jnp.max(excess) > 0.0)
    def _():
        n_bits = (2 * n_tiles * t - 1).bit_length()

        def count_ties_before(cand):
            def cb(j, acc):
                hit = (keys_ref[j] == thr) & ((j * t + row) < cand)
                return acc + jnp.sum(jnp.where(hit, 1.0, 0.0), axis=0, keepdims=True)
            return lax.fori_loop(0, n_live, cb, jnp.zeros((1, t), F32))

        def cut_body(it, cut):
            cand = cut | lax.shift_left(jnp.int32(1), n_bits - 1 - it)
            return jnp.where(count_ties_before(cand) <= need, cand, cut)

        cut = lax.fori_loop(0, n_bits, cut_body, jnp.zeros((1, t), jnp.int32))
        cut_ref[...] = jnp.where(has_rank, cut, jnp.int32(0))

    scale = DSA_LATENT ** -0.5 * LOG2E

    def tile_ctx(j):
        key = keys_ref[j]
        thr_v = thr_ref[...]
        sel = (key > thr_v) | ((key == thr_v) & ((j * t + row) < cut_ref[...]))
        return sel, _key_tile(kvn_ref, j)

    def logits(j, h, ctx, kind):
        sel, kvb = ctx
        return jnp.where(sel, _nn(kvb, qt_ref[h]) * scale + bias_ref[h, kind], NEG)

    _attend_unrolled(n_live, n_tiles, DSA_HEADS, tile_ctx, logits, lambda j, h: kvt_ref[0, j],
                     s_ref, m_ref, a_ref, acc_ref)

    y_t = jnp.zeros((DSA_HEADS * DSA_V_DIM, t), F32)
    for h in range(DSA_HEADS):
        o_h = _attend_result(acc_ref, h, DSA_LATENT).astype(BF16)
        y_t = y_t + _nn(wuv_ref[h], o_h)
    o_ref[0] = _nt(_eye(t), y_t.astype(BF16)).astype(o_ref.dtype)


def _dsa_attention(p3, bias_tiles, kv_norm_g, wuv_t_pad):
    bsz, seq, _ = p3.shape
    t = ATT_TILE
    n_tiles = seq // t
    topk = min(DSA_TOPK_MAX, seq // 4)
    kern = functools.partial(_dsa_kernel, n_tiles=n_tiles, topk=float(topk))
    n_out = DSA_HEADS * DSA_V_DIM
    return pl.pallas_call(
        kern,
        grid=(bsz, n_tiles),
        in_specs=[
            pl.BlockSpec((1, t, 4 * LANES), lambda b, i: (b, i, BLK_BQ // 4)),
            pl.BlockSpec((1, seq, LANES), lambda b, i: (b, 0, BLK_BKV)),
            pl.BlockSpec((1, t, 2 * LANES), lambda b, i: (b, i, BLK_BIQ // 2)),
            pl.BlockSpec((1, seq, LANES), lambda b, i: (b, 0, BLK_IK)),
            pl.BlockSpec((1, t, LANES), lambda b, i: (b, i, BLK_IW)),
            pl.BlockSpec((DSA_HEADS, 3, t, t), lambda b, i: (DIFF_HEADS // DSA_HEADS, 0, 0, 0)),
            pl.BlockSpec((1, DSA_LATENT), lambda b, i: (0, 0)),
            pl.BlockSpec((DSA_HEADS, n_out, DSA_LATENT), lambda b, i: (0, 0, 0)),
        ],
        out_specs=pl.BlockSpec((1, t, n_out), lambda b, i: (b, i, 0)),
        out_shape=jax.ShapeDtypeStruct((bsz, seq, n_out), BF16),
        scratch_shapes=[
            pltpu.VMEM((1, seq, DSA_LATENT), BF16),
            pltpu.VMEM((1, n_tiles, DSA_LATENT + SUM_ROWS, t), BF16),
            pltpu.VMEM((n_tiles, t, t), jnp.int32),
            pltpu.VMEM((3, n_tiles, t, t), BF16),
            pltpu.VMEM((n_tiles, t, t), BF16),
            pltpu.VMEM((1, t), jnp.int32),
            pltpu.VMEM((1, t), jnp.int32),
            pltpu.VMEM((1, t), F32),
            pltpu.VMEM((1, t), F32),
            pltpu.VMEM((IDX_HEADS, LANES, t), BF16),
            pltpu.VMEM((DSA_HEADS, LANES, t), BF16),
        ] + _attend_scratch(DSA_HEADS, DSA_LATENT),
        compiler_params=pltpu.CompilerParams(
            dimension_semantics=("parallel", "arbitrary"), vmem_limit_bytes=VMEM_LIMIT),
        name="dsa_attn",
    )(p3, p3, p3, p3, p3, bias_tiles, kv_norm_g.reshape(1, DSA_LATENT), wuv_t_pad)


def _moba_kernel(q_ref, k_ref, v_ref, bias_ref, o_ref,
                 vt_ref, kmean_ref, selb_ref, qs_ref,
                 s_ref, m_ref, a_ref, acc_ref, *, n_tiles, topb):
    t = ATT_TILE
    i = pl.program_id(1)
    pairs = MOBA_HEADS // 2

    @pl.when(i == 0)
    def _():
        _transpose_tiles(v_ref, vt_ref, n_tiles, pairs)
        for n in range(n_tiles):
            kb = k_ref[0, n * t:(n + 1) * t, :].astype(F32)
            kmean_ref[n:n + 1, :] = jnp.mean(kb, axis=0, keepdims=True)

    scale = HEAD_DIM ** -0.5
    eye_l = _eye(LANES)
    blk = lax.broadcasted_iota(jnp.int32, (n_tiles, t), 0)
    own = jnp.full((1, t), i, jnp.int32)
    for hd in range(MOBA_HEADS):
        g, half = hd // 2, hd % 2
        q_f = _lane_group_mask(q_ref[0, :, g * LANES:(g + 1) * LANES], half, HEAD_DIM)
        qs_ref[hd] = _transposed(q_f * (scale * LOG2E), eye_l)
        kmean = kmean_ref[:, g * LANES:(g + 1) * LANES].astype(BF16)
        gate = _nt(kmean, q_f.astype(BF16))
        for n in range(n_tiles):
            gn = gate[n:n + 1, :]
            ahead = (gate > gn) | ((gate == gn) & (blk < n))
            ahead = ahead & (blk < i)
            rank = jnp.sum(jnp.where(ahead, 1.0, 0.0), axis=0, keepdims=True)
            chosen = ((rank < topb) & (own > n)) | (own == n)
            selb_ref[hd, n] = jnp.broadcast_to(jnp.where(chosen, 0.0, NEG), (8, t))

    def logits(j, hd, ctx, kind):
        g = hd // 2
        kblk = _key_tile(k_ref, j, slice(g * LANES, (g + 1) * LANES))
        return _nn(kblk, qs_ref[hd]) + bias_ref[hd, kind] + selb_ref[hd, j][0:1, :]

    _attend_unrolled(i + 1, n_tiles, MOBA_HEADS, lambda j: None, logits,
                     lambda j, hd: vt_ref[hd // 2, j], s_ref, m_ref, a_ref, acc_ref)

    rows = lax.broadcasted_iota(jnp.int32, (LANES, t), 0)
    eye = _eye(t)
    for g in range(pairs):
        o_lo = _attend_result(acc_ref, 2 * g, LANES)
        o_hi = _attend_result(acc_ref, 2 * g + 1, LANES)
        o = jnp.where(rows < HEAD_DIM, o_lo, o_hi)
        o_ref[0, :, g * LANES:(g + 1) * LANES] = _nt(eye, o.astype(BF16)).astype(o_ref.dtype)


def _moba_attention(p3, bias_tiles):
    bsz, seq, _ = p3.shape
    t = ATT_TILE
    n_tiles = seq // t
    topb = min(MOBA_TOPK_MAX, n_tiles)
    kern = functools.partial(_moba_kernel, n_tiles=n_tiles, topb=float(topb))
    w = MOBA_HEADS * HEAD_DIM
    blocks = w // LANES
    first = (DIFF_HEADS + DSA_HEADS) // MOBA_HEADS
    return pl.pallas_call(
        kern,
        grid=(bsz, n_tiles),
        in_specs=[
            pl.BlockSpec((1, t, w), lambda b, i: (b, i, BLK_CQ // blocks)),
            pl.BlockSpec((1, seq, w), lambda b, i: (b, 0, BLK_CK // blocks)),
            pl.BlockSpec((1, seq, w), lambda b, i: (b, 0, BLK_CV // blocks)),
            pl.BlockSpec((MOBA_HEADS, 3, t, t), lambda b, i: (first, 0, 0, 0)),
        ],
        out_specs=pl.BlockSpec((1, t, w), lambda b, i: (b, i, 0)),
        out_shape=jax.ShapeDtypeStruct((bsz, seq, w), BF16),
        scratch_shapes=[
            pltpu.VMEM((blocks, n_tiles, LANES + SUM_ROWS, t), BF16),
            pltpu.VMEM((n_tiles, w), F32),
            pltpu.VMEM((MOBA_HEADS, n_tiles, 8, t), F32),
            pltpu.VMEM((MOBA_HEADS, LANES, t), BF16),
        ] + _attend_scratch(MOBA_HEADS, LANES),
        compiler_params=pltpu.CompilerParams(
            dimension_semantics=("parallel", "arbitrary"), vmem_limit_bytes=VMEM_LIMIT),
        name="moba_attn",
    )(p3, p3, p3, bias_tiles)


def _merge_kernel(x_ref, g_ref, gb_ref, oa_ref, ob_ref, oc_ref, wa_ref, wb_ref, wc_ref, wo_ref,
                  mod_ref, o_ref):
    d = D_MODEL
    gates = jax.nn.sigmoid(g_ref[...].astype(F32) + gb_ref[...])
    merged = (gates[:, 0:d] * _nn(oa_ref[...], wa_ref[...])
              + gates[:, d:2 * d] * _nn(ob_ref[...], wb_ref[...])
              + gates[:, 2 * d:3 * d] * _nn(oc_ref[...], wc_ref[...]))
    z = _nn(merged.astype(BF16), wo_ref[...])
    o_ref[...] = x_ref[...] + mod_ref[0, 2:3, :] * z


def _merge(x2, p2, gate_b, oa, ob, oc, wa, wb, wc, wo, mod_l, seq):
    m, d = x2.shape
    tm = min(512, seq)
    full = lambda a: pl.BlockSpec(a.shape, lambda i: (0, 0))
    return pl.pallas_call(
        _merge_kernel,
        grid=(m // tm,),
        in_specs=[
            pl.BlockSpec((tm, d), lambda i: (i, 0)),
            pl.BlockSpec((tm, 3 * d), lambda i: (i, BLK_G)),
            pl.BlockSpec((1, 3 * d), lambda i: (0, 0)),
            pl.BlockSpec((tm, oa.shape[1]), lambda i: (i, 0)),
            pl.BlockSpec((tm, ob.shape[1]), lambda i: (i, 0)),
            pl.BlockSpec((tm, oc.shape[1]), lambda i: (i, 0)),
            full(wa), full(wb), full(wc), full(wo),
            pl.BlockSpec((1, 6, d), lambda i: ((i * tm) // seq, 0, 0)),
        ],
        out_specs=pl.BlockSpec((tm, d), lambda i: (i, 0)),
        out_shape=jax.ShapeDtypeStruct((m, d), F32),
        compiler_params=pltpu.CompilerParams(
            dimension_semantics=("parallel",), vmem_limit_bytes=VMEM_LIMIT),
        name="merge",
    )(x2, p2, gate_b.reshape(1, 3 * d), oa, ob, oc, wa, wb, wc, wo, mod_l)


def _mlp_kernel(x_ref, mod_ref, g_ref, w1_ref, w2_ref, gf_ref, o_ref, *, final, tf):
    x = x_ref[...]
    u = _norm_modulate(x, g_ref[...], mod_ref[0, 3:4, :], mod_ref[0, 4:5, :]).astype(BF16)
    acc = None
    for f in range(w1_ref.shape[1] // tf):
        h = jnp.square(jnp.maximum(_nn(u, w1_ref[:, f * tf:(f + 1) * tf]), 0.0))
        part = _nn(h.astype(BF16), w2_ref[f * tf:(f + 1) * tf, :])
        acc = part if acc is None else acc + part
    y = x + mod_ref[0, 5:6, :] * acc
    if final:
        ms = jnp.mean(y * y, axis=-1, keepdims=True)
        y = y * lax.rsqrt(ms + EPS) * gf_ref[...]
    o_ref[...] = y


def _mlp(x2, mod_l, g, w1, w2, g_final, seq, final):
    m, d = x2.shape
    dff = w1.shape[1]
    tm = min(512, seq)
    resident = lambda shape: pl.BlockSpec(shape, lambda i: (0, 0), pipeline_mode=pl.Buffered(1))
    return pl.pallas_call(
        functools.partial(_mlp_kernel, final=final, tf=1024),
        grid=(m // tm,),
        in_specs=[
            pl.BlockSpec((tm, d), lambda i: (i, 0)),
            pl.BlockSpec((1, 6, d), lambda i: ((i * tm) // seq, 0, 0)),
            pl.BlockSpec((1, d), lambda i: (0, 0)),
            resident((d, dff)),
            resident((dff, d)),
            pl.BlockSpec((1, d), lambda i: (0, 0)),
        ],
        out_specs=pl.BlockSpec((tm, d), lambda i: (i, 0)),
        out_shape=jax.ShapeDtypeStruct((m, d), F32),
        compiler_params=pltpu.CompilerParams(
            dimension_semantics=("parallel",), vmem_limit_bytes=VMEM_LIMIT),
        name="mlp",
    )(x2, mod_l, g.reshape(1, d), w1, w2, g_final.reshape(1, d))


def kernel(x, c, rel_bias, ada_w, ada_b, norm_mix, w_in, gate_b, diff_lambda, diff_subln,
           dsa_kv_norm, dsa_w_uv, w_br_a, w_br_b, w_br_c, w_o, norm_mlp, w_ff1, w_ff2,
           norm_final):
    bsz, seq, d = x.shape
    depth = w_in.shape[0]
    assert seq % ATT_TILE == 0 and 2 * ATT_TILE > MAX_DISTANCE

    w_in_p = _pack_columns(w_in.astype(BF16))
    n_out = DSA_HEADS * DSA_V_DIM
    wuv_t = jnp.transpose(dsa_w_uv, (0, 1, 3, 2))
    wuv_t_pad = jnp.zeros((depth, DSA_HEADS, n_out, DSA_LATENT), F32)
    for h in range(DSA_HEADS):
        wuv_t_pad = wuv_t_pad.at[:, h, h * DSA_V_DIM:(h + 1) * DSA_V_DIM, :].set(wuv_t[:, h])
    wuv_t_pad = wuv_t_pad.astype(BF16)
    wa, wb, wc, wo = (w.astype(BF16) for w in (w_br_a, w_br_b, w_br_c, w_o))
    w1, w2 = w_ff1.astype(BF16), w_ff2.astype(BF16)

    bias_tiles = _bias_tiles(rel_bias)
    mod = _ada(c, ada_w, ada_b).reshape(depth, bsz, 6, d)

    x2 = x.reshape(bsz * seq, d)
    for l in range(depth):
        lam_init = 0.8 - 0.6 * math.exp(-0.3 * l)
        p2 = _inproj(x2, mod[l], norm_mix[l], w_in_p[l], seq)
        p3 = p2.reshape(bsz, seq, PACKED_COLS)
        oa = _diff_attention(p3, bias_tiles, diff_lambda[l], diff_subln[l], lam_init)
        ob = _dsa_attention(p3, bias_tiles, dsa_kv_norm[l], wuv_t_pad[l])
        oc = _moba_attention(p3, bias_tiles)
        x2 = _merge(x2, p2, gate_b[l],
                    oa.reshape(bsz * seq, -1), ob.reshape(bsz * seq, -1), oc.reshape(bsz * seq, -1),
                    wa[l], wb[l], wc[l], wo[l], mod[l], seq)
        x2 = _mlp(x2, mod[l], norm_mlp[l], w1[l], w2[l], norm_final, seq, final=(l == depth - 1))
    return x2.reshape(bsz, seq, d)
```

```python
import functools
import math

import jax
import jax.numpy as jnp
from jax import lax
from jax.experimental import pallas as pl
from jax.experimental.pallas import tpu as pltpu

F32 = jnp.float32
BF16 = jnp.bfloat16

D_MODEL = 1024
HEAD_DIM = 64
DIFF_HEADS = 4
DIFF_V_DIM = 2 * HEAD_DIM
DSA_HEADS = 4
DSA_LATENT = 128
DSA_V_DIM = 64
IDX_HEADS = 8
IDX_DIM = 32
DSA_TOPK_MAX = 256
MOBA_HEADS = 4
MOBA_BLOCK = 256
MOBA_TOPK_MAX = 3
N_BUCKETS = 32
MAX_DISTANCE = 128
N_BIAS_HEADS = DIFF_HEADS + DSA_HEADS + MOBA_HEADS
D_FF = 4 * D_MODEL
EPS = 1e-6

LANES = 128
ATT_TILE = MOBA_BLOCK
NEG = -1e30
INT_MIN = -2 ** 31
VMEM_LIMIT = 52 * 1024 * 1024
DIFF_HEADS_PER_STEP = 4
SUM_ROWS = 16
LOG2E = math.log2(math.e)

_O_AQ, _O_AK, _O_AV, _O_BQ, _O_BKV, _O_BIQ, _O_BIK, _O_BIW, _O_CQ, _O_CK, _O_CV, _O_G = (
    0, 512, 1024, 1536, 2048, 2176, 2432, 2464, 2472, 2728, 2984, 3240)

BLK_G, BLK_AQ, BLK_AK, BLK_AV, BLK_BQ, BLK_BIQ, BLK_BKV, BLK_IK, BLK_CQ, BLK_CK, BLK_CV, BLK_IW = (
    0, 24, 28, 32, 36, 40, 42, 43, 44, 46, 48, 50)
N_BLKS = 51
PACKED_COLS = N_BLKS * LANES


def _pack_columns(w):
    seg = lambda off, n: w[..., off:off + n]
    ik = seg(_O_BIK, IDX_DIM)
    parts = [
        seg(_O_G, 3 * D_MODEL), seg(_O_AQ, 512), seg(_O_AK, 512), seg(_O_AV, 512),
        seg(_O_BQ, 512), seg(_O_BIQ, 256), seg(_O_BKV, 128),
        ik, ik, ik, ik,
        seg(_O_CQ, 256), seg(_O_CK, 256), seg(_O_CV, 256),
        seg(_O_BIW, IDX_HEADS),
        jnp.zeros(w.shape[:-1] + (LANES - IDX_HEADS,), w.dtype),
    ]
    out = jnp.concatenate(parts, axis=-1)
    assert out.shape[-1] == PACKED_COLS
    return out


def _nt(a, b):
    return lax.dot_general(a, b, (((1,), (1,)), ((), ())), preferred_element_type=F32)


def _nn(a, b):
    return jnp.dot(a, b, preferred_element_type=F32)


def _eye(n):
    r = lax.broadcasted_iota(jnp.int32, (n, n), 0)
    c = lax.broadcasted_iota(jnp.int32, (n, n), 1)
    return jnp.where(r == c, 1.0, 0.0).astype(BF16)


def _transposed(x_f32, eye):
    return _nt(eye, x_f32.astype(BF16)).astype(BF16)


def _lane_group_mask(x_bf16, group, width):
    lane = lax.broadcasted_iota(jnp.int32, x_bf16.shape, 1)
    keep = (lane >= group * width) & (lane < (group + 1) * width)
    return jnp.where(keep, x_bf16.astype(F32), 0.0)


def _ada_kernel(c_ref, w_ref, b_ref, o_ref):
    c = c_ref[...]
    cond = c * jax.nn.sigmoid(c)
    o_ref[0] = _nn(cond, w_ref[0]) + b_ref[0]


def _ada(c, ada_w, ada_b):
    depth, d, n = ada_w.shape
    bsz = c.shape[0]
    tn = 1536
    return pl.pallas_call(
        _ada_kernel,
        grid=(depth, n // tn),
        in_specs=[
            pl.BlockSpec((bsz, d), lambda l, j: (0, 0)),
            pl.BlockSpec((1, d, tn), lambda l, j: (l, 0, j)),
            pl.BlockSpec((1, 1, tn), lambda l, j: (l, 0, j)),
        ],
        out_specs=pl.BlockSpec((1, bsz, tn), lambda l, j: (l, 0, j)),
        out_shape=jax.ShapeDtypeStruct((depth, bsz, n), F32),
        compiler_params=pltpu.CompilerParams(
            dimension_semantics=("arbitrary", "arbitrary"), vmem_limit_bytes=VMEM_LIMIT),
        name="ada_mod",
    )(c, ada_w, ada_b.reshape(depth, 1, n))


def _t5_bucket(dist):
    max_exact = N_BUCKETS // 2
    n = jnp.maximum(dist, 0)
    nf = jnp.maximum(n, 1).astype(F32)
    large = max_exact + (jnp.log(nf / max_exact) / math.log(MAX_DISTANCE / max_exact)
                         * (N_BUCKETS - max_exact)).astype(jnp.int32)
    large = jnp.minimum(large, N_BUCKETS - 1)
    return jnp.where(n < max_exact, n, large)


def _bias_kernel(rb_ref, o_ref):
    t = ATT_TILE
    h = pl.program_id(0)
    kk = lax.broadcasted_iota(jnp.int32, (t, t), 0)
    qq = lax.broadcasted_iota(jnp.int32, (t, t), 1)
    for kind in range(3):
        dist = jnp.full((t, t), 2 * t, jnp.int32) if kind == 2 else kind * t + qq - kk
        bucket = _t5_bucket(dist)
        tile = jnp.zeros((t, t), F32)
        for b in range(N_BUCKETS):
            tile = jnp.where(bucket == b, rb_ref[b, h], tile)
        o_ref[0, kind] = jnp.where(dist >= 0, tile * LOG2E, NEG)


def _bias_tiles(rel_bias):
    t = ATT_TILE
    heads = rel_bias.shape[1]
    return pl.pallas_call(
        _bias_kernel,
        grid=(heads,),
        in_specs=[pl.BlockSpec(memory_space=pltpu.SMEM)],
        out_specs=pl.BlockSpec((1, 3, t, t), lambda h: (h, 0, 0, 0)),
        out_shape=jax.ShapeDtypeStruct((heads, 3, t, t), F32),
        compiler_params=pltpu.CompilerParams(dimension_semantics=("arbitrary",)),
        name="bias_tiles",
    )(rel_bias.astype(F32))


def _norm_modulate(x, g, shift, scale):
    ms = jnp.mean(x * x, axis=-1, keepdims=True)
    y = x * lax.rsqrt(ms + EPS) * g
    return y * (1.0 + scale) + shift


def _inproj_kernel(x_ref, mod_ref, g_ref, w_ref, o_ref, *, tn):
    u = _norm_modulate(x_ref[...], g_ref[...], mod_ref[0, 0:1, :], mod_ref[0, 1:2, :])
    u = u.astype(BF16)
    for j in range(w_ref.shape[1] // tn):
        cols = slice(j * tn, (j + 1) * tn)
        o_ref[:, cols] = _nn(u, w_ref[:, cols]).astype(o_ref.dtype)


def _inproj(x2, mod_l, g, w_packed, seq):
    m, d = x2.shape
    n = w_packed.shape[1]
    tm = min(512, seq)
    return pl.pallas_call(
        functools.partial(_inproj_kernel, tn=n // 3),
        grid=(m // tm,),
        in_specs=[
            pl.BlockSpec((tm, d), lambda i: (i, 0)),
            pl.BlockSpec((1, 6, d), lambda i: ((i * tm) // seq, 0, 0)),
            pl.BlockSpec((1, d), lambda i: (0, 0)),
            pl.BlockSpec((d, n), lambda i: (0, 0), pipeline_mode=pl.Buffered(1)),
        ],
        out_specs=pl.BlockSpec((tm, n), lambda i: (i, 0)),
        out_shape=jax.ShapeDtypeStruct((m, n), BF16),
        compiler_params=pltpu.CompilerParams(
            dimension_semantics=("parallel",), vmem_limit_bytes=VMEM_LIMIT),
        name="in_proj",
    )(x2, mod_l, g.reshape(1, d), w_packed)


def _attend(n_live, n_tiles, chains, tile_ctx, logits, values, s_ref, m_ref, a_ref, acc_ref,
            *, unroll):
    def init():
        m_ref[...] = jnp.full(m_ref.shape, NEG, F32)
        acc_ref[...] = jnp.zeros(acc_ref.shape, F32)

    def step(j, slot, kind, start=True, finish=True):
        ctx = tile_ctx(j) if start else None
        for c in range(chains):
            if start:
                s = logits(j, c, ctx, kind)
                m_old = m_ref[1 - slot, c]
                m_new = jnp.maximum(m_old, jnp.max(s, axis=0, keepdims=True))
                s_ref[slot, c] = s
                a_ref[slot, c] = jnp.exp2(m_old - m_new)
                m_ref[slot, c] = m_new
            if finish:
                p = jnp.exp2(s_ref[1 - slot, c] - m_ref[1 - slot, c])
                acc_ref[c] = (a_ref[1 - slot, c] * acc_ref[c]
                              + _nn(values(j - 1, c), p.astype(BF16)))

    if unroll:
        def variant(live):
            init()
            for j in range(live + 1):
                step(j, j % 2, min(live - 1 - j, 2), start=j < live, finish=j > 0)

        for live in range(1, n_tiles + 1):
            pl.when(n_live == live)(functools.partial(variant, live))
        return

    init()
    step(0, 0, jnp.minimum(n_live - 1, 2), finish=False)

    def body(j, carry):
        for slot in range(2):
            pl.when(j % 2 == slot)(
                functools.partial(step, j, slot, jnp.minimum(n_live - 1 - j, 2)))
        return carry

    lax.fori_loop(1, n_live, body, 0)
    for slot in range(2):
        pl.when(n_live % 2 == slot)(functools.partial(step, n_live, slot, None, start=False))


def _attend_scratch(chains, dv):
    t = ATT_TILE
    return [
        pltpu.VMEM((2, chains, t, t), F32),
        pltpu.VMEM((2, chains, 1, t), F32),
        pltpu.VMEM((2, chains, 1, t), F32),
        pltpu.VMEM((chains, dv + SUM_ROWS, t), F32),
    ]


def _attend_result(acc_ref, c, dv):
    return acc_ref[c, 0:dv, :] / acc_ref[c, dv:dv + 1, :]


def _key_tile(ref, j, cols=None):
    t = ATT_TILE
    rows = slice(j * t, (j + 1) * t) if isinstance(j, int) else pl.ds(pl.multiple_of(j * t, t), t)
    return ref[0, rows, :] if cols is None else ref[0, rows, cols]


def _transpose_tiles(src_ref, dst_ref, n_tiles, groups):
    t = ATT_TILE
    eye = _eye(LANES)
    for g in range(groups):
        for j in range(n_tiles):
            blk = src_ref[0, j * t:(j + 1) * t, g * LANES:(g + 1) * LANES]
            dst_ref[g, j, 0:LANES, :] = _nt(eye, blk).astype(BF16)
            dst_ref[g, j, LANES:LANES + SUM_ROWS, :] = jnp.ones((SUM_ROWS, t), BF16)


def _diff_kernel(q_ref, k_ref, v_ref, bias_ref, dl_ref, g_ref, o_ref,
                 vt_ref, qs_ref, s_ref, m_ref, a_ref, acc_ref, *, lam_init, n_tiles, hp):
    t = ATT_TILE
    i = pl.program_id(2)

    @pl.when(i == 0)
    def _():
        _transpose_tiles(v_ref, vt_ref, n_tiles, hp)

    scale = HEAD_DIM ** -0.5 * LOG2E
    eye_l = _eye(LANES)
    for h in range(hp):
        q = q_ref[0, :, h * LANES:(h + 1) * LANES]
        for half in range(2):
            qs_ref[2 * h + half] = _transposed(_lane_group_mask(q, half, HEAD_DIM) * scale, eye_l)

    def logits(j, c, ctx, kind):
        h = c // 2
        kblk = _key_tile(k_ref, j, slice(h * LANES, (h + 1) * LANES))
        return _nn(kblk, qs_ref[c]) + bias_ref[h, kind]

    _attend(i + 1, n_tiles, 2 * hp, lambda j: None, logits, lambda j, c: vt_ref[c // 2, j],
            s_ref, m_ref, a_ref, acc_ref, unroll=True)

    dl = dl_ref[...]
    lam = (jnp.exp(jnp.sum(dl[0:1] * dl[1:2], keepdims=True))
           - jnp.exp(jnp.sum(dl[2:3] * dl[3:4], keepdims=True)) + lam_init)
    eye = _eye(t)
    for h in range(hp):
        o = (_attend_result(acc_ref, 2 * h, DIFF_V_DIM)
             - lam * _attend_result(acc_ref, 2 * h + 1, DIFF_V_DIM))
        ms = jnp.mean(o * o, axis=0, keepdims=True)
        o = o * lax.rsqrt(ms + EPS) * g_ref[...] * (1.0 - lam_init)
        o_ref[0, :, h * LANES:(h + 1) * LANES] = _nt(eye, o.astype(BF16)).astype(o_ref.dtype)


def _diff_attention(p3, bias_tiles, diff_lambda_l, subln_g, lam_init):
    bsz, seq, _ = p3.shape
    t = ATT_TILE
    n_tiles = seq // t
    hp = DIFF_HEADS_PER_STEP
    w = hp * LANES
    kern = functools.partial(_diff_kernel, lam_init=lam_init, n_tiles=n_tiles, hp=hp)
    return pl.pallas_call(
        kern,
        grid=(bsz, DIFF_HEADS // hp, n_tiles),
        in_specs=[
            pl.BlockSpec((1, t, w), lambda b, h, i: (b, i, BLK_AQ // hp + h)),
            pl.BlockSpec((1, seq, w), lambda b, h, i: (b, 0, BLK_AK // hp + h)),
            pl.BlockSpec((1, seq, w), lambda b, h, i: (b, 0, BLK_AV // hp + h)),
            pl.BlockSpec((hp, 3, t, t), lambda b, h, i: (h, 0, 0, 0)),
            pl.BlockSpec((4, HEAD_DIM), lambda b, h, i: (0, 0)),
            pl.BlockSpec((DIFF_V_DIM, 1), lambda b, h, i: (0, 0)),
        ],
        out_specs=pl.BlockSpec((1, t, w), lambda b, h, i: (b, i, h)),
        out_shape=jax.ShapeDtypeStruct((bsz, seq, DIFF_HEADS * DIFF_V_DIM), BF16),
        scratch_shapes=[
            pltpu.VMEM((hp, n_tiles, LANES + SUM_ROWS, t), BF16),
            pltpu.VMEM((2 * hp, LANES, t), BF16),
        ] + _attend_scratch(2 * hp, DIFF_V_DIM),
        compiler_params=pltpu.CompilerParams(
            dimension_semantics=("parallel", "parallel", "arbitrary"),
            vmem_limit_bytes=VMEM_LIMIT),
        name="diff_attn",
    )(p3, p3, p3, bias_tiles, diff_lambda_l, subln_g.reshape(DIFF_V_DIM, 1))


def _dsa_kernel(q_ref, kv_ref, iq_ref, ik_ref, iw_ref, bias_ref, g_ref, wuv_ref, o_ref,
                kvn_ref, kvt_ref, keys_ref, byte_ref, cand_ref, thr_ref, cut_ref, need_ref, tied_ref,
                iqh_ref, qt_ref,
                s_ref, m_ref, a_ref, acc_ref, *, n_tiles, topk):
    t = ATT_TILE
    i = pl.program_id(1)
    n_live = i + 1

    @pl.when(i == 0)
    def _():
        kv = kv_ref[0].astype(F32)
        ms = jnp.mean(kv * kv, axis=-1, keepdims=True)
        kvn_ref[0] = (kv * lax.rsqrt(ms + EPS) * g_ref[...]).astype(BF16)
        _transpose_tiles(kvn_ref, kvt_ref, n_tiles, 1)

    groups = LANES // IDX_DIM
    eye_l = _eye(LANES)
    for h in range(IDX_HEADS):
        blk = iq_ref[0, :, (h // groups) * LANES:(h // groups + 1) * LANES]
        iqh_ref[h] = _transposed(_lane_group_mask(blk, h % groups, IDX_DIM), eye_l)
    for h in range(DSA_HEADS):
        qt_ref[h] = _nt(eye_l, q_ref[0, :, h * LANES:(h + 1) * LANES]).astype(BF16)
    sel_rows = lax.broadcasted_iota(jnp.int32, (IDX_HEADS, LANES), 0)
    sel_cols = lax.broadcasted_iota(jnp.int32, (IDX_HEADS, LANES), 1)
    pick = jnp.where(sel_rows == sel_cols, 1.0, 0.0).astype(BF16)
    iw_t = _nt(pick, iw_ref[0])
    idx_scale = (IDX_HEADS ** -0.5) * (IDX_DIM ** -0.5)
    row = lax.broadcasted_iota(jnp.int32, (t, t), 0)
    col = lax.broadcasted_iota(jnp.int32, (t, t), 1)

    def score_tile(j, diagonal):
        ik = _key_tile(ik_ref, j)
        sc = jnp.zeros((t, t), F32)
        for h in range(IDX_HEADS):
            sc = sc + jnp.maximum(_nn(ik, iqh_ref[h]), 0.0) * iw_t[h:h + 1, :]
        sc = sc * idx_scale
        bits = pltpu.bitcast(sc, jnp.int32)
        key = bits ^ ((bits >> 31) & 0x7FFFFFFF)
        top = (key >> 24) + 128
        if diagonal:
            valid = col >= row
            key = jnp.where(valid, key, INT_MIN)
            top = jnp.where(valid, top, -1)
        keys_ref[j] = key
        cand_ref[j] = top.astype(F32).astype(BF16)
        for b in range(3):
            byte_ref[b, j] = ((key >> (16 - 8 * b)) & 0xFF).astype(F32).astype(BF16)

    one_b, zero_b = jnp.ones((), BF16), jnp.zeros((), BF16)
    packed_rows = 16

    def radix_select(live):
        for j in range(live):
            score_tile(j, diagonal=(j == live - 1))

        def count(hit_fn):
            part = None
            for j in range(live):
                hit = jnp.where(hit_fn(cand_ref[j]), one_b, zero_b)
                for r in range(t // packed_rows):
                    rows = hit[packed_rows * r:packed_rows * (r + 1)]
                    part = rows if part is None else part + rows
            return jnp.sum(part.astype(F32), axis=0, keepdims=True)

        need = jnp.full((1, t), topk, F32)
        thr = jnp.zeros((1, t), jnp.int32)
        for b in range(4):
            def bit_body(it, val, need=need):
                cand = val | lax.shift_left(jnp.int32(1), 7 - it)
                cand_b = cand.astype(F32).astype(BF16)
                return jnp.where(count(lambda x: x >= cand_b) >= need, cand, val)

            val = lax.fori_loop(0, 8, bit_body, jnp.zeros((1, t), jnp.int32))
            val_b = val.astype(F32).astype(BF16)
            need = need - count(lambda x: x > val_b)
            if b < 3:
                for j in range(live):
                    cand_ref[j] = jnp.where(cand_ref[j] == val_b, byte_ref[b, j], -one_b)
            else:
                tied_ref[...] = count(lambda x: x == val_b)
            piece = val - (128 if b == 0 else 0)
            thr = thr | lax.shift_left(piece, 24 - 8 * b)
        thr_ref[...] = thr
        need_ref[...] = need

    assert (t // packed_rows) * n_tiles <= 256
    for live in range(1, n_tiles + 1):
        pl.when(n_live == live)(functools.partial(radix_select, live))

    thr = thr_ref[...]
    need = need_ref[...]
    has_rank = thr > INT_MIN
    excess = jnp.where(has_rank & (tied_ref[...] > need), 1.0, 0.0)
    cut_ref[...] = jnp.where(has_rank, jnp.int32(2 * n_tiles * t), jnp.int32(0))

    @pl.when(jnp.max(excess) > 0.0)
    def _():
        n_bits = (2 * n_tiles * t - 1).bit_length()

        def count_ties_before(cand):
            def cb(j, acc):
                hit = (keys_ref[j] == thr) & ((j * t + row) < cand)
                return acc + jnp.sum(jnp.where(hit, 1.0, 0.0), axis=0, keepdims=True)
            return lax.fori_loop(0, n_live, cb, jnp.zeros((1, t), F32))

        def cut_body(it, cut):
            cand = cut | lax.shift_left(jnp.int32(1), n_bits - 1 - it)
            return jnp.where(count_ties_before(cand) <= need, cand, cut)

        cut = lax.fori_loop(0, n_bits, cut_body, jnp.zeros((1, t), jnp.int32))
        cut_ref[...] = jnp.where(has_rank, cut, jnp.int32(0))

    scale = DSA_LATENT ** -0.5 * LOG2E

    def tile_ctx(j):
        key = keys_ref[j]
        thr_v = thr_ref[...]
        sel = (key > thr_v) | ((key == thr_v) & ((j * t + row) < cut_ref[...]))
        return sel, _key_tile(kvn_ref, j)

    def logits(j, h, ctx, kind):
        sel, kvb = ctx
        return jnp.where(sel, _nn(kvb, qt_ref[h]) * scale + bias_ref[h, kind], NEG)

    _attend(n_live, n_tiles, DSA_HEADS, tile_ctx, logits, lambda j, h: kvt_ref[0, j],
            s_ref, m_ref, a_ref, acc_ref, unroll=False)

    y_t = jnp.zeros((DSA_HEADS * DSA_V_DIM, t), F32)
    for h in range(DSA_HEADS):
        o_h = _attend_result(acc_ref, h, DSA_LATENT).astype(BF16)
        y_t = y_t + _nn(wuv_ref[h], o_h)
    o_ref[0] = _nt(_eye(t), y_t.astype(BF16)).astype(o_ref.dtype)


def _dsa_attention(p3, bias_tiles, kv_norm_g, wuv_t_pad):
    bsz, seq, _ = p3.shape
    t = ATT_TILE
    n_tiles = seq // t
    topk = min(DSA_TOPK_MAX, seq // 4)
    kern = functools.partial(_dsa_kernel, n_tiles=n_tiles, topk=float(topk))
    n_out = DSA_HEADS * DSA_V_DIM
    return pl.pallas_call(
        kern,
        grid=(bsz, n_tiles),
        in_specs=[
            pl.BlockSpec((1, t, 4 * LANES), lambda b, i: (b, i, BLK_BQ // 4)),
            pl.BlockSpec((1, seq, LANES), lambda b, i: (b, 0, BLK_BKV)),
            pl.BlockSpec((1, t, 2 * LANES), lambda b, i: (b, i, BLK_BIQ // 2)),
            pl.BlockSpec((1, seq, LANES), lambda b, i: (b, 0, BLK_IK)),
            pl.BlockSpec((1, t, LANES), lambda b, i: (b, i, BLK_IW)),
            pl.BlockSpec((DSA_HEADS, 3, t, t), lambda b, i: (DIFF_HEADS // DSA_HEADS, 0, 0, 0)),
            pl.BlockSpec((1, DSA_LATENT), lambda b, i: (0, 0)),
            pl.BlockSpec((DSA_HEADS, n_out, DSA_LATENT), lambda b, i: (0, 0, 0)),
        ],
        out_specs=pl.BlockSpec((1, t, n_out), lambda b, i: (b, i, 0)),
        out_shape=jax.ShapeDtypeStruct((bsz, seq, n_out), BF16),
        scratch_shapes=[
            pltpu.VMEM((1, seq, DSA_LATENT), BF16),
            pltpu.VMEM((1, n_tiles, DSA_LATENT + SUM_ROWS, t), BF16),
            pltpu.VMEM((n_tiles, t, t), jnp.int32),
            pltpu.VMEM((3, n_tiles, t, t), BF16),
            pltpu.VMEM((n_tiles, t, t), BF16),
            pltpu.VMEM((1, t), jnp.int32),
            pltpu.VMEM((1, t), jnp.int32),
            pltpu.VMEM((1, t), F32),
            pltpu.VMEM((1, t), F32),
            pltpu.VMEM((IDX_HEADS, LANES, t), BF16),
            pltpu.VMEM((DSA_HEADS, LANES, t), BF16),
        ] + _attend_scratch(DSA_HEADS, DSA_LATENT),
        compiler_params=pltpu.CompilerParams(
            dimension_semantics=("parallel", "arbitrary"), vmem_limit_bytes=VMEM_LIMIT),
        name="dsa_attn",
    )(p3, p3, p3, p3, p3, bias_tiles, kv_norm_g.reshape(1, DSA_LATENT), wuv_t_pad)


def _moba_kernel(q_ref, k_ref, v_ref, bias_ref, o_ref,
                 vt_ref, kmean_ref, selb_ref, qs_ref,
                 s_ref, m_ref, a_ref, acc_ref, *, n_tiles, topb):
    t = ATT_TILE
    i = pl.program_id(1)
    pairs = MOBA_HEADS // 2

    @pl.when(i == 0)
    def _():
        _transpose_tiles(v_ref, vt_ref, n_tiles, pairs)
        for n in range(n_tiles):
            kb = k_ref[0, n * t:(n + 1) * t, :].astype(F32)
            kmean_ref[n:n + 1, :] = jnp.mean(kb, axis=0, keepdims=True)

    scale = HEAD_DIM ** -0.5
    eye_l = _eye(LANES)
    blk = lax.broadcasted_iota(jnp.int32, (n_tiles, t), 0)
    own = jnp.full((1, t), i, jnp.int32)
    for hd in range(MOBA_HEADS):
        g, half = hd // 2, hd % 2
        q_f = _lane_group_mask(q_ref[0, :, g * LANES:(g + 1) * LANES], half, HEAD_DIM)
        qs_ref[hd] = _transposed(q_f * (scale * LOG2E), eye_l)
        kmean = kmean_ref[:, g * LANES:(g + 1) * LANES].astype(BF16)
        gate = _nt(kmean, q_f.astype(BF16))
        for n in range(n_tiles):
            gn = gate[n:n + 1, :]
            ahead = (gate > gn) | ((gate == gn) & (blk < n))
            ahead = ahead & (blk < i)
            rank = jnp.sum(jnp.where(ahead, 1.0, 0.0), axis=0, keepdims=True)
            chosen = ((rank < topb) & (own > n)) | (own == n)
            selb_ref[hd, n] = jnp.broadcast_to(jnp.where(chosen, 0.0, NEG), (8, t))

    def logits(j, hd, ctx, kind):
        g = hd // 2
        kblk = _key_tile(k_ref, j, slice(g * LANES, (g + 1) * LANES))
        return _nn(kblk, qs_ref[hd]) + bias_ref[hd, kind] + selb_ref[hd, j][0:1, :]

    _attend(i + 1, n_tiles, MOBA_HEADS, lambda j: None, logits, lambda j, hd: vt_ref[hd // 2, j],
            s_ref, m_ref, a_ref, acc_ref, unroll=True)

    rows = lax.broadcasted_iota(jnp.int32, (LANES, t), 0)
    eye = _eye(t)
    for g in range(pairs):
        o_lo = _attend_result(acc_ref, 2 * g, LANES)
        o_hi = _attend_result(acc_ref, 2 * g + 1, LANES)
        o = jnp.where(rows < HEAD_DIM, o_lo, o_hi)
        o_ref[0, :, g * LANES:(g + 1) * LANES] = _nt(eye, o.astype(BF16)).astype(o_ref.dtype)


def _moba_attention(p3, bias_tiles):
    bsz, seq, _ = p3.shape
    t = ATT_TILE
    n_tiles = seq // t
    topb = min(MOBA_TOPK_MAX, n_tiles)
    kern = functools.partial(_moba_kernel, n_tiles=n_tiles, topb=float(topb))
    w = MOBA_HEADS * HEAD_DIM
    blocks = w // LANES
    first = (DIFF_HEADS + DSA_HEADS) // MOBA_HEADS
    return pl.pallas_call(
        kern,
        grid=(bsz, n_tiles),
        in_specs=[
            pl.BlockSpec((1, t, w), lambda b, i: (b, i, BLK_CQ // blocks)),
            pl.BlockSpec((1, seq, w), lambda b, i: (b, 0, BLK_CK // blocks)),
            pl.BlockSpec((1, seq, w), lambda b, i: (b, 0, BLK_CV // blocks)),
            pl.BlockSpec((MOBA_HEADS, 3, t, t), lambda b, i: (first, 0, 0, 0)),
        ],
        out_specs=pl.BlockSpec((1, t, w), lambda b, i: (b, i, 0)),
        out_shape=jax.ShapeDtypeStruct((bsz, seq, w), BF16),
        scratch_shapes=[
            pltpu.VMEM((blocks, n_tiles, LANES + SUM_ROWS, t), BF16),
            pltpu.VMEM((n_tiles, w), F32),
            pltpu.VMEM((MOBA_HEADS, n_tiles, 8, t), F32),
            pltpu.VMEM((MOBA_HEADS, LANES, t), BF16),
        ] + _attend_scratch(MOBA_HEADS, LANES),
        compiler_params=pltpu.CompilerParams(
            dimension_semantics=("parallel", "arbitrary"), vmem_limit_bytes=VMEM_LIMIT),
        name="moba_attn",
    )(p3, p3, p3, bias_tiles)


def _merge_kernel(x_ref, g_ref, gb_ref, oa_ref, ob_ref, oc_ref, wa_ref, wb_ref, wc_ref, wo_ref,
                  mod_ref, o_ref):
    d = D_MODEL
    gates = jax.nn.sigmoid(g_ref[...].astype(F32) + gb_ref[...])
    merged = (gates[:, 0:d] * _nn(oa_ref[...], wa_ref[...])
              + gates[:, d:2 * d] * _nn(ob_ref[...], wb_ref[...])
              + gates[:, 2 * d:3 * d] * _nn(oc_ref[...], wc_ref[...]))
    z = _nn(merged.astype(BF16), wo_ref[...])
    o_ref[...] = x_ref[...] + mod_ref[0, 2:3, :] * z


def _merge(x2, p2, gate_b, oa, ob, oc, wa, wb, wc, wo, mod_l, seq):
    m, d = x2.shape
    tm = min(512, seq)
    full = lambda a: pl.BlockSpec(a.shape, lambda i: (0, 0))
    return pl.pallas_call(
        _merge_kernel,
        grid=(m // tm,),
        in_specs=[
            pl.BlockSpec((tm, d), lambda i: (i, 0)),
            pl.BlockSpec((tm, 3 * d), lambda i: (i, BLK_G)),
            pl.BlockSpec((1, 3 * d), lambda i: (0, 0)),
            pl.BlockSpec((tm, oa.shape[1]), lambda i: (i, 0)),
            pl.BlockSpec((tm, ob.shape[1]), lambda i: (i, 0)),
            pl.BlockSpec((tm, oc.shape[1]), lambda i: (i, 0)),
            full(wa), full(wb), full(wc), full(wo),
            pl.BlockSpec((1, 6, d), lambda i: ((i * tm) // seq, 0, 0)),
        ],
        out_specs=pl.BlockSpec((tm, d), lambda i: (i, 0)),
        out_shape=jax.ShapeDtypeStruct((m, d), F32),
        compiler_params=pltpu.CompilerParams(
            dimension_semantics=("parallel",), vmem_limit_bytes=VMEM_LIMIT),
        name="merge",
    )(x2, p2, gate_b.reshape(1, 3 * d), oa, ob, oc, wa, wb, wc, wo, mod_l)


def _mlp_kernel(x_ref, mod_ref, g_ref, w1_ref, w2_ref, gf_ref, o_ref, *, final, tf):
    x = x_ref[...]
    u = _norm_modulate(x, g_ref[...], mod_ref[0, 3:4, :], mod_ref[0, 4:5, :]).astype(BF16)
    acc = None
    for f in range(w1_ref.shape[1] // tf):
        h = jnp.square(jnp.maximum(_nn(u, w1_ref[:, f * tf:(f + 1) * tf]), 0.0))
        part = _nn(h.astype(BF16), w2_ref[f * tf:(f + 1) * tf, :])
        acc = part if acc is None else acc + part
    y = x + mod_ref[0, 5:6, :] * acc
    if final:
        ms = jnp.mean(y * y, axis=-1, keepdims=True)
        y = y * lax.rsqrt(ms + EPS) * gf_ref[...]
    o_ref[...] = y


def _mlp(x2, mod_l, g, w1, w2, g_final, seq, final):
    m, d = x2.shape
    dff = w1.shape[1]
    tm = min(512, seq)
    resident = lambda shape: pl.BlockSpec(shape, lambda i: (0, 0), pipeline_mode=pl.Buffered(1))
    return pl.pallas_call(
        functools.partial(_mlp_kernel, final=final, tf=1024),
        grid=(m // tm,),
        in_specs=[
            pl.BlockSpec((tm, d), lambda i: (i, 0)),
            pl.BlockSpec((1, 6, d), lambda i: ((i * tm) // seq, 0, 0)),
            pl.BlockSpec((1, d), lambda i: (0, 0)),
            resident((d, dff)),
            resident((dff, d)),
            pl.BlockSpec((1, d), lambda i: (0, 0)),
        ],
        out_specs=pl.BlockSpec((tm, d), lambda i: (i, 0)),
        out_shape=jax.ShapeDtypeStruct((m, d), F32),
        compiler_params=pltpu.CompilerParams(
            dimension_semantics=("parallel",), vmem_limit_bytes=VMEM_LIMIT),
        name="mlp",
    )(x2, mod_l, g.reshape(1, d), w1, w2, g_final.reshape(1, d))


def kernel(x, c, rel_bias, ada_w, ada_b, norm_mix, w_in, gate_b, diff_lambda, diff_subln,
           dsa_kv_norm, dsa_w_uv, w_br_a, w_br_b, w_br_c, w_o, norm_mlp, w_ff1, w_ff2,
           norm_final):
    bsz, seq, d = x.shape
    depth = w_in.shape[0]
    assert seq % ATT_TILE == 0 and 2 * ATT_TILE > MAX_DISTANCE

    w_in_p = _pack_columns(w_in.astype(BF16))
    n_out = DSA_HEADS * DSA_V_DIM
    wuv_t = jnp.transpose(dsa_w_uv, (0, 1, 3, 2))
    wuv_t_pad = jnp.zeros((depth, DSA_HEADS, n_out, DSA_LATENT), F32)
    for h in range(DSA_HEADS):
        wuv_t_pad = wuv_t_pad.at[:, h, h * DSA_V_DIM:(h + 1) * DSA_V_DIM, :].set(wuv_t[:, h])
    wuv_t_pad = wuv_t_pad.astype(BF16)
    wa, wb, wc, wo = (w.astype(BF16) for w in (w_br_a, w_br_b, w_br_c, w_o))
    w1, w2 = w_ff1.astype(BF16), w_ff2.astype(BF16)

    bias_tiles = _bias_tiles(rel_bias)
    mod = _ada(c, ada_w, ada_b).reshape(depth, bsz, 6, d)

    x2 = x.reshape(bsz * seq, d)
    for l in range(depth):
        lam_init = 0.8 - 0.6 * math.exp(-0.3 * l)
        p2 = _inproj(x2, mod[l], norm_mix[l], w_in_p[l], seq)
        p3 = p2.reshape(bsz, seq, PACKED_COLS)
        oa = _diff_attention(p3, bias_tiles, diff_lambda[l], diff_subln[l], lam_init)
        ob = _dsa_attention(p3, bias_tiles, dsa_kv_norm[l], wuv_t_pad[l])
        oc = _moba_attention(p3, bias_tiles)
        x2 = _merge(x2, p2, gate_b[l],
                    oa.reshape(bsz * seq, -1), ob.reshape(bsz * seq, -1), oc.reshape(bsz * seq, -1),
                    wa[l], wb[l], wc[l], wo[l], mod[l], seq)
        x2 = _mlp(x2, mod[l], norm_mlp[l], w1[l], w2[l], norm_final, seq, final=(l == depth - 1))
    return x2.reshape(bsz, seq, d)
```

```python
import functools
import math

import jax
import jax.numpy as jnp
from jax import lax
from jax.experimental import pallas as pl
from jax.experimental.pallas import tpu as pltpu

F32 = jnp.float32
BF16 = jnp.bfloat16

D_MODEL = 1024
HEAD_DIM = 64
DIFF_HEADS = 4
DIFF_V_DIM = 2 * HEAD_DIM
DSA_HEADS = 4
DSA_LATENT = 128
DSA_V_DIM = 64
IDX_HEADS = 8
IDX_DIM = 32
DSA_TOPK_MAX = 256
MOBA_HEADS = 4
MOBA_BLOCK = 256
MOBA_TOPK_MAX = 3
N_BUCKETS = 32
MAX_DISTANCE = 128
N_BIAS_HEADS = DIFF_HEADS + DSA_HEADS + MOBA_HEADS
D_FF = 4 * D_MODEL
EPS = 1e-6

LANES = 128
ATT_TILE = MOBA_BLOCK
NEG = -1e30
INT_MIN = -2 ** 31
VMEM_LIMIT = 52 * 1024 * 1024
DIFF_HEADS_PER_STEP = 4
SUM_ROWS = 16
LOG2E = math.log2(math.e)

_O_AQ, _O_AK, _O_AV, _O_BQ, _O_BKV, _O_BIQ, _O_BIK, _O_BIW, _O_CQ, _O_CK, _O_CV, _O_G = (
    0, 512, 1024, 1536, 2048, 2176, 2432, 2464, 2472, 2728, 2984, 3240)

BLK_G, BLK_AQ, BLK_AK, BLK_AV, BLK_BQ, BLK_BIQ, BLK_BKV, BLK_IK, BLK_CQ, BLK_CK, BLK_CV, BLK_IW = (
    0, 24, 28, 32, 36, 40, 42, 43, 44, 46, 48, 50)
N_BLKS = 51
PACKED_COLS = N_BLKS * LANES


def _pack_columns(w):
    seg = lambda off, n: w[..., off:off + n]
    ik = seg(_O_BIK, IDX_DIM)
    parts = [
        seg(_O_G, 3 * D_MODEL), seg(_O_AQ, 512), seg(_O_AK, 512), seg(_O_AV, 512),
        seg(_O_BQ, 512), seg(_O_BIQ, 256), seg(_O_BKV, 128),
        ik, ik, ik, ik,
        seg(_O_CQ, 256), seg(_O_CK, 256), seg(_O_CV, 256),
        seg(_O_BIW, IDX_HEADS),
        jnp.zeros(w.shape[:-1] + (LANES - IDX_HEADS,), w.dtype),
    ]
    out = jnp.concatenate(parts, axis=-1)
    assert out.shape[-1] == PACKED_COLS
    return out


def _nt(a, b):
    return lax.dot_general(a, b, (((1,), (1,)), ((), ())), preferred_element_type=F32)


def _nn(a, b):
    return jnp.dot(a, b, preferred_element_type=F32)


def _eye(n):
    r = lax.broadcasted_iota(jnp.int32, (n, n), 0)
    c = lax.broadcasted_iota(jnp.int32, (n, n), 1)
    return jnp.where(r == c, 1.0, 0.0).astype(BF16)


def _transposed(x_f32, eye):
    return _nt(eye, x_f32.astype(BF16)).astype(BF16)


def _lane_group_mask(x_bf16, group, width):
    lane = lax.broadcasted_iota(jnp.int32, x_bf16.shape, 1)
    keep = (lane >= group * width) & (lane < (group + 1) * width)
    return jnp.where(keep, x_bf16.astype(F32), 0.0)


def _ada_kernel(c_ref, w_ref, b_ref, o_ref):
    c = c_ref[...]
    cond = c * jax.nn.sigmoid(c)
    o_ref[0] = _nn(cond, w_ref[0]) + b_ref[0]


def _ada(c, ada_w, ada_b):
    depth, d, n = ada_w.shape
    bsz = c.shape[0]
    tn = 1536
    return pl.pallas_call(
        _ada_kernel,
        grid=(depth, n // tn),
        in_specs=[
            pl.BlockSpec((bsz, d), lambda l, j: (0, 0)),
            pl.BlockSpec((1, d, tn), lambda l, j: (l, 0, j)),
            pl.BlockSpec((1, 1, tn), lambda l, j: (l, 0, j)),
        ],
        out_specs=pl.BlockSpec((1, bsz, tn), lambda l, j: (l, 0, j)),
        out_shape=jax.ShapeDtypeStruct((depth, bsz, n), F32),
        compiler_params=pltpu.CompilerParams(
            dimension_semantics=("arbitrary", "arbitrary"), vmem_limit_bytes=VMEM_LIMIT),
        name="ada_mod",
    )(c, ada_w, ada_b.reshape(depth, 1, n))


def _t5_bucket(dist):
    max_exact = N_BUCKETS // 2
    n = jnp.maximum(dist, 0)
    nf = jnp.maximum(n, 1).astype(F32)
    large = max_exact + (jnp.log(nf / max_exact) / math.log(MAX_DISTANCE / max_exact)
                         * (N_BUCKETS - max_exact)).astype(jnp.int32)
    large = jnp.minimum(large, N_BUCKETS - 1)
    return jnp.where(n < max_exact, n, large)


def _bias_kernel(rb_ref, o_ref):
    t = ATT_TILE
    h = pl.program_id(0)
    kk = lax.broadcasted_iota(jnp.int32, (t, t), 0)
    qq = lax.broadcasted_iota(jnp.int32, (t, t), 1)
    for kind in range(3):
        dist = jnp.full((t, t), 2 * t, jnp.int32) if kind == 2 else kind * t + qq - kk
        bucket = _t5_bucket(dist)
        tile = jnp.zeros((t, t), F32)
        for b in range(N_BUCKETS):
            tile = jnp.where(bucket == b, rb_ref[b, h], tile)
        o_ref[0, kind] = jnp.where(dist >= 0, tile * LOG2E, NEG)


def _bias_tiles(rel_bias):
    t = ATT_TILE
    heads = rel_bias.shape[1]
    return pl.pallas_call(
        _bias_kernel,
        grid=(heads,),
        in_specs=[pl.BlockSpec(memory_space=pltpu.SMEM)],
        out_specs=pl.BlockSpec((1, 3, t, t), lambda h: (h, 0, 0, 0)),
        out_shape=jax.ShapeDtypeStruct((heads, 3, t, t), F32),
        compiler_params=pltpu.CompilerParams(dimension_semantics=("arbitrary",)),
        name="bias_tiles",
    )(rel_bias.astype(F32))


def _norm_modulate(x, g, shift, scale):
    ms = jnp.mean(x * x, axis=-1, keepdims=True)
    y = x * lax.rsqrt(ms + EPS) * g
    return y * (1.0 + scale) + shift


def _inproj_kernel(x_ref, mod_ref, g_ref, w_ref, o_ref, *, tn):
    u = _norm_modulate(x_ref[...], g_ref[...], mod_ref[0, 0:1, :], mod_ref[0, 1:2, :])
    u = u.astype(BF16)
    for j in range(w_ref.shape[1] // tn):
        cols = slice(j * tn, (j + 1) * tn)
        o_ref[:, cols] = _nn(u, w_ref[:, cols]).astype(o_ref.dtype)


def _inproj(x2, mod_l, g, w_packed, seq):
    m, d = x2.shape
    n = w_packed.shape[1]
    tm = min(512, seq)
    return pl.pallas_call(
        functools.partial(_inproj_kernel, tn=n // 3),
        grid=(m // tm,),
        in_specs=[
            pl.BlockSpec((tm, d), lambda i: (i, 0)),
            pl.BlockSpec((1, 6, d), lambda i: ((i * tm) // seq, 0, 0)),
            pl.BlockSpec((1, d), lambda i: (0, 0)),
            pl.BlockSpec((d, n), lambda i: (0, 0), pipeline_mode=pl.Buffered(1)),
        ],
        out_specs=pl.BlockSpec((tm, n), lambda i: (i, 0)),
        out_shape=jax.ShapeDtypeStruct((m, n), BF16),
        compiler_params=pltpu.CompilerParams(
            dimension_semantics=("parallel",), vmem_limit_bytes=VMEM_LIMIT),
        name="in_proj",
    )(x2, mod_l, g.reshape(1, d), w_packed)


def _attend(n_live, lives, chains, tile_ctx, logits, values, s_ref, m_ref, a_ref, acc_ref,
            *, unroll):
    def init():
        m_ref[...] = jnp.full(m_ref.shape, NEG, F32)
        acc_ref[...] = jnp.zeros(acc_ref.shape, F32)

    def step(j, slot, kind, start=True, finish=True):
        ctx = tile_ctx(j) if start else None
        for c in range(chains):
            if start:
                s = logits(j, c, ctx, kind)
                m_old = m_ref[1 - slot, c]
                m_new = jnp.maximum(m_old, jnp.max(s, axis=0, keepdims=True))
                s_ref[slot, c] = s
                a_ref[slot, c] = jnp.exp2(m_old - m_new)
                m_ref[slot, c] = m_new
            if finish:
                p = jnp.exp2(s_ref[1 - slot, c] - m_ref[1 - slot, c])
                acc_ref[c] = (a_ref[1 - slot, c] * acc_ref[c]
                              + _nn(values(j - 1, c), p.astype(BF16)))

    if unroll:
        def variant(live):
            init()
            for j in range(live + 1):
                step(j, j % 2, min(live - 1 - j, 2), start=j < live, finish=j > 0)

        for live in lives:
            pl.when(n_live == live)(functools.partial(variant, live))
        return

    init()
    step(0, 0, jnp.minimum(n_live - 1, 2), finish=False)

    def body(j, carry):
        for slot in range(2):
            pl.when(j % 2 == slot)(
                functools.partial(step, j, slot, jnp.minimum(n_live - 1 - j, 2)))
        return carry

    lax.fori_loop(1, n_live, body, 0)
    for slot in range(2):
        pl.when(n_live % 2 == slot)(functools.partial(step, n_live, slot, None, start=False))


def _attend_scratch(chains, dv):
    t = ATT_TILE
    return [
        pltpu.VMEM((2, chains, t, t), F32),
        pltpu.VMEM((2, chains, 1, t), F32),
        pltpu.VMEM((2, chains, 1, t), F32),
        pltpu.VMEM((chains, dv + SUM_ROWS, t), F32),
    ]


def _attend_result(acc_ref, c, dv):
    return acc_ref[c, 0:dv, :] / acc_ref[c, dv:dv + 1, :]


def _key_tile(ref, j, cols=None):
    t = ATT_TILE
    rows = slice(j * t, (j + 1) * t) if isinstance(j, int) else pl.ds(pl.multiple_of(j * t, t), t)
    return ref[0, rows, :] if cols is None else ref[0, rows, cols]


def _transpose_tiles(src_ref, dst_ref, n_tiles, groups):
    t = ATT_TILE
    eye = _eye(LANES)
    for g in range(groups):
        for j in range(n_tiles):
            blk = src_ref[0, j * t:(j + 1) * t, g * LANES:(g + 1) * LANES]
            dst_ref[g, j, 0:LANES, :] = _nt(eye, blk).astype(BF16)
            dst_ref[g, j, LANES:LANES + SUM_ROWS, :] = jnp.ones((SUM_ROWS, t), BF16)


def _diff_kernel(q_ref, k_ref, v_ref, bias_ref, dl_ref, g_ref, o_ref,
                 vt_ref, qs_ref, s_ref, m_ref, a_ref, acc_ref, *, lam_init, n_tiles, hp):
    t = ATT_TILE
    i = pl.program_id(2)

    @pl.when(i == 0)
    def _():
        _transpose_tiles(v_ref, vt_ref, n_tiles, hp)

    scale = HEAD_DIM ** -0.5 * LOG2E
    eye_l = _eye(LANES)
    for h in range(hp):
        q = q_ref[0, :, h * LANES:(h + 1) * LANES]
        for half in range(2):
            qs_ref[2 * h + half] = _transposed(_lane_group_mask(q, half, HEAD_DIM) * scale, eye_l)

    def logits(j, c, ctx, kind):
        h = c // 2
        kblk = _key_tile(k_ref, j, slice(h * LANES, (h + 1) * LANES))
        return _nn(kblk, qs_ref[c]) + bias_ref[h, kind]

    _attend(i + 1, range(1, n_tiles + 1), 2 * hp, lambda j: None, logits, lambda j, c: vt_ref[c // 2, j],
            s_ref, m_ref, a_ref, acc_ref, unroll=True)

    dl = dl_ref[...]
    lam = (jnp.exp(jnp.sum(dl[0:1] * dl[1:2], keepdims=True))
           - jnp.exp(jnp.sum(dl[2:3] * dl[3:4], keepdims=True)) + lam_init)
    eye = _eye(t)
    for h in range(hp):
        o = (_attend_result(acc_ref, 2 * h, DIFF_V_DIM)
             - lam * _attend_result(acc_ref, 2 * h + 1, DIFF_V_DIM))
        ms = jnp.mean(o * o, axis=0, keepdims=True)
        o = o * lax.rsqrt(ms + EPS) * g_ref[...] * (1.0 - lam_init)
        o_ref[0, :, h * LANES:(h + 1) * LANES] = _nt(eye, o.astype(BF16)).astype(o_ref.dtype)


def _diff_attention(p3, bias_tiles, diff_lambda_l, subln_g, lam_init):
    bsz, seq, _ = p3.shape
    t = ATT_TILE
    n_tiles = seq // t
    hp = DIFF_HEADS_PER_STEP
    w = hp * LANES
    kern = functools.partial(_diff_kernel, lam_init=lam_init, n_tiles=n_tiles, hp=hp)
    return pl.pallas_call(
        kern,
        grid=(bsz, DIFF_HEADS // hp, n_tiles),
        in_specs=[
            pl.BlockSpec((1, t, w), lambda b, h, i: (b, i, BLK_AQ // hp + h)),
            pl.BlockSpec((1, seq, w), lambda b, h, i: (b, 0, BLK_AK // hp + h)),
            pl.BlockSpec((1, seq, w), lambda b, h, i: (b, 0, BLK_AV // hp + h)),
            pl.BlockSpec((hp, 3, t, t), lambda b, h, i: (h, 0, 0, 0)),
            pl.BlockSpec((4, HEAD_DIM), lambda b, h, i: (0, 0)),
            pl.BlockSpec((DIFF_V_DIM, 1), lambda b, h, i: (0, 0)),
        ],
        out_specs=pl.BlockSpec((1, t, w), lambda b, h, i: (b, i, h)),
        out_shape=jax.ShapeDtypeStruct((bsz, seq, DIFF_HEADS * DIFF_V_DIM), BF16),
        scratch_shapes=[
            pltpu.VMEM((hp, n_tiles, LANES + SUM_ROWS, t), BF16),
            pltpu.VMEM((2 * hp, LANES, t), BF16),
        ] + _attend_scratch(2 * hp, DIFF_V_DIM),
        compiler_params=pltpu.CompilerParams(
            dimension_semantics=("parallel", "parallel", "arbitrary"),
            vmem_limit_bytes=VMEM_LIMIT),
        name="diff_attn",
    )(p3, p3, p3, bias_tiles, diff_lambda_l, subln_g.reshape(DIFF_V_DIM, 1))


def _dsa_kernel(q_ref, kv_ref, iq_ref, ik_ref, iw_ref, bias_ref, g_ref, wuv_ref, o_ref,
                kvn_ref, kvt_ref, keys_ref, byte_ref, cand_ref, thr_ref, cut_ref, need_ref, tied_ref,
                iqh_ref, qt_ref,
                s_ref, m_ref, a_ref, acc_ref, *, n_tiles, first_tile, last_tile, topk):
    t = ATT_TILE
    i = pl.program_id(1) + first_tile
    n_live = i + 1
    lives = range(first_tile + 1, last_tile + 2)

    @pl.when(pl.program_id(1) == 0)
    def _():
        kv = kv_ref[0].astype(F32)
        ms = jnp.mean(kv * kv, axis=-1, keepdims=True)
        kvn_ref[0] = (kv * lax.rsqrt(ms + EPS) * g_ref[...]).astype(BF16)
        _transpose_tiles(kvn_ref, kvt_ref, n_tiles, 1)

    groups = LANES // IDX_DIM
    eye_l = _eye(LANES)
    for h in range(IDX_HEADS):
        blk = iq_ref[0, :, (h // groups) * LANES:(h // groups + 1) * LANES]
        iqh_ref[h] = _transposed(_lane_group_mask(blk, h % groups, IDX_DIM), eye_l)
    for h in range(DSA_HEADS):
        qt_ref[h] = _nt(eye_l, q_ref[0, :, h * LANES:(h + 1) * LANES]).astype(BF16)
    sel_rows = lax.broadcasted_iota(jnp.int32, (IDX_HEADS, LANES), 0)
    sel_cols = lax.broadcasted_iota(jnp.int32, (IDX_HEADS, LANES), 1)
    pick = jnp.where(sel_rows == sel_cols, 1.0, 0.0).astype(BF16)
    iw_t = _nt(pick, iw_ref[0])
    idx_scale = (IDX_HEADS ** -0.5) * (IDX_DIM ** -0.5)
    row = lax.broadcasted_iota(jnp.int32, (t, t), 0)
    col = lax.broadcasted_iota(jnp.int32, (t, t), 1)

    def score_tile(j, diagonal):
        ik = _key_tile(ik_ref, j)
        sc = jnp.zeros((t, t), F32)
        for h in range(IDX_HEADS):
            sc = sc + jnp.maximum(_nn(ik, iqh_ref[h]), 0.0) * iw_t[h:h + 1, :]
        sc = sc * idx_scale
        bits = pltpu.bitcast(sc, jnp.int32)
        key = bits ^ ((bits >> 31) & 0x7FFFFFFF)
        top = (key >> 24) + 128
        if diagonal:
            valid = col >= row
            key = jnp.where(valid, key, INT_MIN)
            top = jnp.where(valid, top, -1)
        keys_ref[j] = key
        cand_ref[j] = top.astype(F32).astype(BF16)
        for b in range(3):
            byte_ref[b, j] = ((key >> (16 - 8 * b)) & 0xFF).astype(F32).astype(BF16)

    one_b, zero_b = jnp.ones((), BF16), jnp.zeros((), BF16)
    packed_rows = 16

    def radix_select(live):
        for j in range(live):
            score_tile(j, diagonal=(j == live - 1))

        def count(hit_fn):
            part = None
            for j in range(live):
                hit = jnp.where(hit_fn(cand_ref[j]), one_b, zero_b)
                for r in range(t // packed_rows):
                    rows = hit[packed_rows * r:packed_rows * (r + 1)]
                    part = rows if part is None else part + rows
            return jnp.sum(part.astype(F32), axis=0, keepdims=True)

        need = jnp.full((1, t), topk, F32)
        thr = jnp.zeros((1, t), jnp.int32)
        for b in range(4):
            def bit_body(it, val, need=need):
                cand = val | lax.shift_left(jnp.int32(1), 7 - it)
                cand_b = cand.astype(F32).astype(BF16)
                return jnp.where(count(lambda x: x >= cand_b) >= need, cand, val)

            val = lax.fori_loop(0, 8, bit_body, jnp.zeros((1, t), jnp.int32))
            val_b = val.astype(F32).astype(BF16)
            need = need - count(lambda x: x > val_b)
            if b < 3:
                for j in range(live):
                    cand_ref[j] = jnp.where(cand_ref[j] == val_b, byte_ref[b, j], -one_b)
            else:
                tied_ref[...] = count(lambda x: x == val_b)
            piece = val - (128 if b == 0 else 0)
            thr = thr | lax.shift_left(piece, 24 - 8 * b)
        thr_ref[...] = thr
        need_ref[...] = need

    assert (t // packed_rows) * n_tiles <= 256
    for live in lives:
        pl.when(n_live == live)(functools.partial(radix_select, live))

    thr = thr_ref[...]
    need = need_ref[...]
    has_rank = thr > INT_MIN
    excess = jnp.where(has_rank & (tied_ref[...] > need), 1.0, 0.0)
    cut_ref[...] = jnp.where(has_rank, jnp.int32(2 * n_tiles * t), jnp.int32(0))

    @pl.when(jnp.max(excess) > 0.0)
    def _():
        n_bits = (2 * n_tiles * t - 1).bit_length()

        def count_ties_before(cand):
            def cb(j, acc):
                hit = (keys_ref[j] == thr) & ((j * t + row) < cand)
                return acc + jnp.sum(jnp.where(hit, 1.0, 0.0), axis=0, keepdims=True)
            return lax.fori_loop(0, n_live, cb, jnp.zeros((1, t), F32))

        def cut_body(it, cut):
            cand = cut | lax.shift_left(jnp.int32(1), n_bits - 1 - it)
            return jnp.where(count_ties_before(cand) <= need, cand, cut)

        cut = lax.fori_loop(0, n_bits, cut_body, jnp.zeros((1, t), jnp.int32))
        cut_ref[...] = jnp.where(has_rank, cut, jnp.int32(0))

    scale = DSA_LATENT ** -0.5 * LOG2E

    def tile_ctx(j):
        key = keys_ref[j]
        thr_v = thr_ref[...]
        sel = (key > thr_v) | ((key == thr_v) & ((j * t + row) < cut_ref[...]))
        return sel, _key_tile(kvn_ref, j)

    def logits(j, h, ctx, kind):
        sel, kvb = ctx
        return jnp.where(sel, _nn(kvb, qt_ref[h]) * scale + bias_ref[h, kind], NEG)

    _attend(n_live, lives, DSA_HEADS, tile_ctx, logits, lambda j, h: kvt_ref[0, j],
            s_ref, m_ref, a_ref, acc_ref, unroll=True)

    y_t = jnp.zeros((DSA_HEADS * DSA_V_DIM, t), F32)
    for h in range(DSA_HEADS):
        o_h = _attend_result(acc_ref, h, DSA_LATENT).astype(BF16)
        y_t = y_t + _nn(wuv_ref[h], o_h)
    o_ref[0] = _nt(_eye(t), y_t.astype(BF16)).astype(o_ref.dtype)


def _dsa_attention(p3, bias_tiles, kv_norm_g, wuv_t_pad):
    n_tiles = p3.shape[1] // ATT_TILE
    split = n_tiles - (3 * n_tiles) // 8
    parts = [_dsa_attention_part(p3, bias_tiles, kv_norm_g, wuv_t_pad, lo, hi)
             for lo, hi in ((0, split - 1), (split, n_tiles - 1)) if hi >= lo]
    return jnp.concatenate(parts, axis=1)


def _dsa_attention_part(p3, bias_tiles, kv_norm_g, wuv_t_pad, first_tile, last_tile):
    bsz, seq, _ = p3.shape
    t = ATT_TILE
    n_tiles = seq // t
    n_q = last_tile - first_tile + 1
    topk = min(DSA_TOPK_MAX, seq // 4)
    kern = functools.partial(_dsa_kernel, n_tiles=n_tiles, first_tile=first_tile,
                             last_tile=last_tile, topk=float(topk))
    n_out = DSA_HEADS * DSA_V_DIM
    return pl.pallas_call(
        kern,
        grid=(bsz, n_q),
        in_specs=[
            pl.BlockSpec((1, t, 4 * LANES), lambda b, i: (b, i + first_tile, BLK_BQ // 4)),
            pl.BlockSpec((1, seq, LANES), lambda b, i: (b, 0, BLK_BKV)),
            pl.BlockSpec((1, t, 2 * LANES), lambda b, i: (b, i + first_tile, BLK_BIQ // 2)),
            pl.BlockSpec((1, seq, LANES), lambda b, i: (b, 0, BLK_IK)),
            pl.BlockSpec((1, t, LANES), lambda b, i: (b, i + first_tile, BLK_IW)),
            pl.BlockSpec((DSA_HEADS, 3, t, t), lambda b, i: (DIFF_HEADS // DSA_HEADS, 0, 0, 0)),
            pl.BlockSpec((1, DSA_LATENT), lambda b, i: (0, 0)),
            pl.BlockSpec((DSA_HEADS, n_out, DSA_LATENT), lambda b, i: (0, 0, 0)),
        ],
        out_specs=pl.BlockSpec((1, t, n_out), lambda b, i: (b, i, 0)),
        out_shape=jax.ShapeDtypeStruct((bsz, n_q * t, n_out), BF16),
        scratch_shapes=[
            pltpu.VMEM((1, seq, DSA_LATENT), BF16),
            pltpu.VMEM((1, n_tiles, DSA_LATENT + SUM_ROWS, t), BF16),
            pltpu.VMEM((n_tiles, t, t), jnp.int32),
            pltpu.VMEM((3, n_tiles, t, t), BF16),
            pltpu.VMEM((n_tiles, t, t), BF16),
            pltpu.VMEM((1, t), jnp.int32),
            pltpu.VMEM((1, t), jnp.int32),
            pltpu.VMEM((1, t), F32),
            pltpu.VMEM((1, t), F32),
            pltpu.VMEM((IDX_HEADS, LANES, t), BF16),
            pltpu.VMEM((DSA_HEADS, LANES, t), BF16),
        ] + _attend_scratch(DSA_HEADS, DSA_LATENT),
        compiler_params=pltpu.CompilerParams(
            dimension_semantics=("parallel", "arbitrary"), vmem_limit_bytes=VMEM_LIMIT),
        name="dsa_attn",
    )(p3, p3, p3, p3, p3, bias_tiles, kv_norm_g.reshape(1, DSA_LATENT), wuv_t_pad)


def _moba_kernel(q_ref, k_ref, v_ref, bias_ref, o_ref,
                 vt_ref, kmean_ref, selb_ref, qs_ref,
                 s_ref, m_ref, a_ref, acc_ref, *, n_tiles, topb):
    t = ATT_TILE
    i = pl.program_id(1)
    pairs = MOBA_HEADS // 2

    @pl.when(i == 0)
    def _():
        _transpose_tiles(v_ref, vt_ref, n_tiles, pairs)
        for n in range(n_tiles):
            kb = k_ref[0, n * t:(n + 1) * t, :].astype(F32)
            kmean_ref[n:n + 1, :] = jnp.mean(kb, axis=0, keepdims=True)

    scale = HEAD_DIM ** -0.5
    eye_l = _eye(LANES)
    blk = lax.broadcasted_iota(jnp.int32, (n_tiles, t), 0)
    own = jnp.full((1, t), i, jnp.int32)
    for hd in range(MOBA_HEADS):
        g, half = hd // 2, hd % 2
        q_f = _lane_group_mask(q_ref[0, :, g * LANES:(g + 1) * LANES], half, HEAD_DIM)
        qs_ref[hd] = _transposed(q_f * (scale * LOG2E), eye_l)
        kmean = kmean_ref[:, g * LANES:(g + 1) * LANES].astype(BF16)
        gate = _nt(kmean, q_f.astype(BF16))
        for n in range(n_tiles):
            gn = gate[n:n + 1, :]
            ahead = (gate > gn) | ((gate == gn) & (blk < n))
            ahead = ahead & (blk < i)
            rank = jnp.sum(jnp.where(ahead, 1.0, 0.0), axis=0, keepdims=True)
            chosen = ((rank < topb) & (own > n)) | (own == n)
            selb_ref[hd, n] = jnp.broadcast_to(jnp.where(chosen, 0.0, NEG), (8, t))

    def logits(j, hd, ctx, kind):
        g = hd // 2
        kblk = _key_tile(k_ref, j, slice(g * LANES, (g + 1) * LANES))
        return _nn(kblk, qs_ref[hd]) + bias_ref[hd, kind] + selb_ref[hd, j][0:1, :]

    _attend(i + 1, range(1, n_tiles + 1), MOBA_HEADS, lambda j: None, logits, lambda j, hd: vt_ref[hd // 2, j],
            s_ref, m_ref, a_ref, acc_ref, unroll=True)

    rows = lax.broadcasted_iota(jnp.int32, (LANES, t), 0)
    eye = _eye(t)
    for g in range(pairs):
        o_lo = _attend_result(acc_ref, 2 * g, LANES)
        o_hi = _attend_result(acc_ref, 2 * g + 1, LANES)
        o = jnp.where(rows < HEAD_DIM, o_lo, o_hi)
        o_ref[0, :, g * LANES:(g + 1) * LANES] = _nt(eye, o.astype(BF16)).astype(o_ref.dtype)


def _moba_attention(p3, bias_tiles):
    bsz, seq, _ = p3.shape
    t = ATT_TILE
    n_tiles = seq // t
    topb = min(MOBA_TOPK_MAX, n_tiles)
    kern = functools.partial(_moba_kernel, n_tiles=n_tiles, topb=float(topb))
    w = MOBA_HEADS * HEAD_DIM
    blocks = w // LANES
    first = (DIFF_HEADS + DSA_HEADS) // MOBA_HEADS
    return pl.pallas_call(
        kern,
        grid=(bsz, n_tiles),
        in_specs=[
            pl.BlockSpec((1, t, w), lambda b, i: (b, i, BLK_CQ // blocks)),
            pl.BlockSpec((1, seq, w), lambda b, i: (b, 0, BLK_CK // blocks)),
            pl.BlockSpec((1, seq, w), lambda b, i: (b, 0, BLK_CV // blocks)),
            pl.BlockSpec((MOBA_HEADS, 3, t, t), lambda b, i: (first, 0, 0, 0)),
        ],
        out_specs=pl.BlockSpec((1, t, w), lambda b, i: (b, i, 0)),
        out_shape=jax.ShapeDtypeStruct((bsz, seq, w), BF16),
        scratch_shapes=[
            pltpu.VMEM((blocks, n_tiles, LANES + SUM_ROWS, t), BF16),
            pltpu.VMEM((n_tiles, w), F32),
            pltpu.VMEM((MOBA_HEADS, n_tiles, 8, t), F32),
            pltpu.VMEM((MOBA_HEADS, LANES, t), BF16),
        ] + _attend_scratch(MOBA_HEADS, LANES),
        compiler_params=pltpu.CompilerParams(
            dimension_semantics=("parallel", "arbitrary"), vmem_limit_bytes=VMEM_LIMIT),
        name="moba_attn",
    )(p3, p3, p3, bias_tiles)


def _merge_kernel(x_ref, g_ref, gb_ref, oa_ref, ob_ref, oc_ref, wa_ref, wb_ref, wc_ref, wo_ref,
                  mod_ref, o_ref):
    d = D_MODEL
    gates = jax.nn.sigmoid(g_ref[...].astype(F32) + gb_ref[...])
    merged = (gates[:, 0:d] * _nn(oa_ref[...], wa_ref[...])
              + gates[:, d:2 * d] * _nn(ob_ref[...], wb_ref[...])
              + gates[:, 2 * d:3 * d] * _nn(oc_ref[...], wc_ref[...]))
    z = _nn(merged.astype(BF16), wo_ref[...])
    o_ref[...] = x_ref[...] + mod_ref[0, 2:3, :] * z


def _merge(x2, p2, gate_b, oa, ob, oc, wa, wb, wc, wo, mod_l, seq):
    m, d = x2.shape
    tm = min(512, seq)
    full = lambda a: pl.BlockSpec(a.shape, lambda i: (0, 0))
    return pl.pallas_call(
        _merge_kernel,
        grid=(m // tm,),
        in_specs=[
            pl.BlockSpec((tm, d), lambda i: (i, 0)),
            pl.BlockSpec((tm, 3 * d), lambda i: (i, BLK_G)),
            pl.BlockSpec((1, 3 * d), lambda i: (0, 0)),
            pl.BlockSpec((tm, oa.shape[1]), lambda i: (i, 0)),
            pl.BlockSpec((tm, ob.shape[1]), lambda i: (i, 0)),
            pl.BlockSpec((tm, oc.shape[1]), lambda i: (i, 0)),
            full(wa), full(wb), full(wc), full(wo),
            pl.BlockSpec((1, 6, d), lambda i: ((i * tm) // seq, 0, 0)),
        ],
        out_specs=pl.BlockSpec((tm, d), lambda i: (i, 0)),
        out_shape=jax.ShapeDtypeStruct((m, d), F32),
        compiler_params=pltpu.CompilerParams(
            dimension_semantics=("parallel",), vmem_limit_bytes=VMEM_LIMIT),
        name="merge",
    )(x2, p2, gate_b.reshape(1, 3 * d), oa, ob, oc, wa, wb, wc, wo, mod_l)


def _mlp_kernel(x_ref, mod_ref, g_ref, w1_ref, w2_ref, gf_ref, o_ref, *, final, tf):
    x = x_ref[...]
    u = _norm_modulate(x, g_ref[...], mod_ref[0, 3:4, :], mod_ref[0, 4:5, :]).astype(BF16)
    acc = None
    for f in range(w1_ref.shape[1] // tf):
        h = jnp.square(jnp.maximum(_nn(u, w1_ref[:, f * tf:(f + 1) * tf]), 0.0))
        part = _nn(h.astype(BF16), w2_ref[f * tf:(f + 1) * tf, :])
        acc = part if acc is None else acc + part
    y = x + mod_ref[0, 5:6, :] * acc
    if final:
        ms = jnp.mean(y * y, axis=-1, keepdims=True)
        y = y * lax.rsqrt(ms + EPS) * gf_ref[...]
    o_ref[...] = y


def _mlp(x2, mod_l, g, w1, w2, g_final, seq, final):
    m, d = x2.shape
    dff = w1.shape[1]
    tm = min(512, seq)
    resident = lambda shape: pl.BlockSpec(shape, lambda i: (0, 0), pipeline_mode=pl.Buffered(1))
    return pl.pallas_call(
        functools.partial(_mlp_kernel, final=final, tf=1024),
        grid=(m // tm,),
        in_specs=[
            pl.BlockSpec((tm, d), lambda i: (i, 0)),
            pl.BlockSpec((1, 6, d), lambda i: ((i * tm) // seq, 0, 0)),
            pl.BlockSpec((1, d), lambda i: (0, 0)),
            resident((d, dff)),
            resident((dff, d)),
            pl.BlockSpec((1, d), lambda i: (0, 0)),
        ],
        out_specs=pl.BlockSpec((tm, d), lambda i: (i, 0)),
        out_shape=jax.ShapeDtypeStruct((m, d), F32),
        compiler_params=pltpu.CompilerParams(
            dimension_semantics=("parallel",), vmem_limit_bytes=VMEM_LIMIT),
        name="mlp",
    )(x2, mod_l, g.reshape(1, d), w1, w2, g_final.reshape(1, d))


def kernel(x, c, rel_bias, ada_w, ada_b, norm_mix, w_in, gate_b, diff_lambda, diff_subln,
           dsa_kv_norm, dsa_w_uv, w_br_a, w_br_b, w_br_c, w_o, norm_mlp, w_ff1, w_ff2,
           norm_final):
    bsz, seq, d = x.shape
    depth = w_in.shape[0]
    assert seq % ATT_TILE == 0 and 2 * ATT_TILE > MAX_DISTANCE

    w_in_p = _pack_columns(w_in.astype(BF16))
    n_out = DSA_HEADS * DSA_V_DIM
    wuv_t = jnp.transpose(dsa_w_uv, (0, 1, 3, 2))
    wuv_t_pad = jnp.zeros((depth, DSA_HEADS, n_out, DSA_LATENT), F32)
    for h in range(DSA_HEADS):
        wuv_t_pad = wuv_t_pad.at[:, h, h * DSA_V_DIM:(h + 1) * DSA_V_DIM, :].set(wuv_t[:, h])
    wuv_t_pad = wuv_t_pad.astype(BF16)
    wa, wb, wc, wo = (w.astype(BF16) for w in (w_br_a, w_br_b, w_br_c, w_o))
    w1, w2 = w_ff1.astype(BF16), w_ff2.astype(BF16)

    bias_tiles = _bias_tiles(rel_bias)
    mod = _ada(c, ada_w, ada_b).reshape(depth, bsz, 6, d)

    x2 = x.reshape(bsz * seq, d)
    for l in range(depth):
        lam_init = 0.8 - 0.6 * math.exp(-0.3 * l)
        p2 = _inproj(x2, mod[l], norm_mix[l], w_in_p[l], seq)
        p3 = p2.reshape(bsz, seq, PACKED_COLS)
        oa = _diff_attention(p3, bias_tiles, diff_lambda[l], diff_subln[l], lam_init)
        ob = _dsa_attention(p3, bias_tiles, dsa_kv_norm[l], wuv_t_pad[l])
        oc = _moba_attention(p3, bias_tiles)
        x2 = _merge(x2, p2, gate_b[l],
                    oa.reshape(bsz * seq, -1), ob.reshape(bsz * seq, -1), oc.reshape(bsz * seq, -1),
                    wa[l], wb[l], wc[l], wo[l], mod[l], seq)
        x2 = _mlp(x2, mod[l], norm_mlp[l], w1[l], w2[l], norm_final, seq, final=(l == depth - 1))
    return x2.reshape(bsz, seq, d)
```

```python
import functools
import math

import jax
import jax.numpy as jnp
from jax import lax
from jax.experimental import pallas as pl
from jax.experimental.pallas import tpu as pltpu

F32 = jnp.float32
BF16 = jnp.bfloat16

D_MODEL = 1024
HEAD_DIM = 64
DIFF_HEADS = 4
DIFF_V_DIM = 2 * HEAD_DIM
DSA_HEADS = 4
DSA_LATENT = 128
DSA_V_DIM = 64
IDX_HEADS = 8
IDX_DIM = 32
DSA_TOPK_MAX = 256
MOBA_HEADS = 4
MOBA_BLOCK = 256
MOBA_TOPK_MAX = 3
N_BUCKETS = 32
MAX_DISTANCE = 128
N_BIAS_HEADS = DIFF_HEADS + DSA_HEADS + MOBA_HEADS
D_FF = 4 * D_MODEL
EPS = 1e-6

LANES = 128
ATT_TILE = MOBA_BLOCK
NEG = -1e30
INT_MIN = -2 ** 31
VMEM_LIMIT = 52 * 1024 * 1024
DIFF_HEADS_PER_STEP = 4
SUM_ROWS = 16
LOG2E = math.log2(math.e)

_O_AQ, _O_AK, _O_AV, _O_BQ, _O_BKV, _O_BIQ, _O_BIK, _O_BIW, _O_CQ, _O_CK, _O_CV, _O_G = (
    0, 512, 1024, 1536, 2048, 2176, 2432, 2464, 2472, 2728, 2984, 3240)

BLK_G, BLK_AQ, BLK_AK, BLK_AV, BLK_BQ, BLK_BIQ, BLK_BKV, BLK_IK, BLK_CQ, BLK_CK, BLK_CV, BLK_IW = (
    0, 24, 28, 32, 36, 40, 42, 43, 44, 46, 48, 50)
N_BLKS = 51
PACKED_COLS = N_BLKS * LANES


def _pack_columns(w):
    seg = lambda off, n: w[..., off:off + n]
    ik = seg(_O_BIK, IDX_DIM)
    parts = [
        seg(_O_G, 3 * D_MODEL), seg(_O_AQ, 512), seg(_O_AK, 512), seg(_O_AV, 512),
        seg(_O_BQ, 512), seg(_O_BIQ, 256), seg(_O_BKV, 128),
        ik, ik, ik, ik,
        seg(_O_CQ, 256), seg(_O_CK, 256), seg(_O_CV, 256),
        seg(_O_BIW, IDX_HEADS),
        jnp.zeros(w.shape[:-1] + (LANES - IDX_HEADS,), w.dtype),
    ]
    out = jnp.concatenate(parts, axis=-1)
    assert out.shape[-1] == PACKED_COLS
    return out


def _nt(a, b):
    return lax.dot_general(a, b, (((1,), (1,)), ((), ())), preferred_element_type=F32)


def _nn(a, b):
    return jnp.dot(a, b, preferred_element_type=F32)


def _eye(n):
    r = lax.broadcasted_iota(jnp.int32, (n, n), 0)
    c = lax.broadcasted_iota(jnp.int32, (n, n), 1)
    return jnp.where(r == c, 1.0, 0.0).astype(BF16)


def _transposed(x_f32, eye):
    return _nt(eye, x_f32.astype(BF16)).astype(BF16)


def _lane_group_mask(x_bf16, group, width):
    lane = lax.broadcasted_iota(jnp.int32, x_bf16.shape, 1)
    keep = (lane >= group * width) & (lane < (group + 1) * width)
    return jnp.where(keep, x_bf16.astype(F32), 0.0)


def _ada_kernel(c_ref, w_ref, b_ref, o_ref):
    c = c_ref[...]
    cond = c * jax.nn.sigmoid(c)
    o_ref[0] = _nn(cond, w_ref[0]) + b_ref[0]


def _ada(c, ada_w, ada_b):
    depth, d, n = ada_w.shape
    bsz = c.shape[0]
    tn = 1536
    return pl.pallas_call(
        _ada_kernel,
        grid=(depth, n // tn),
        in_specs=[
            pl.BlockSpec((bsz, d), lambda l, j: (0, 0)),
            pl.BlockSpec((1, d, tn), lambda l, j: (l, 0, j)),
            pl.BlockSpec((1, 1, tn), lambda l, j: (l, 0, j)),
        ],
        out_specs=pl.BlockSpec((1, bsz, tn), lambda l, j: (l, 0, j)),
        out_shape=jax.ShapeDtypeStruct((depth, bsz, n), F32),
        compiler_params=pltpu.CompilerParams(
            dimension_semantics=("arbitrary", "arbitrary"), vmem_limit_bytes=VMEM_LIMIT),
        name="ada_mod",
    )(c, ada_w, ada_b.reshape(depth, 1, n))


def _t5_bucket(dist):
    max_exact = N_BUCKETS // 2
    n = jnp.maximum(dist, 0)
    nf = jnp.maximum(n, 1).astype(F32)
    large = max_exact + (jnp.log(nf / max_exact) / math.log(MAX_DISTANCE / max_exact)
                         * (N_BUCKETS - max_exact)).astype(jnp.int32)
    large = jnp.minimum(large, N_BUCKETS - 1)
    return jnp.where(n < max_exact, n, large)


def _bias_kernel(rb_ref, o_ref):
    t = ATT_TILE
    h = pl.program_id(0)
    kk = lax.broadcasted_iota(jnp.int32, (t, t), 0)
    qq = lax.broadcasted_iota(jnp.int32, (t, t), 1)
    for kind in range(3):
        dist = jnp.full((t, t), 2 * t, jnp.int32) if kind == 2 else kind * t + qq - kk
        bucket = _t5_bucket(dist)
        tile = jnp.zeros((t, t), F32)
        for b in range(N_BUCKETS):
            tile = jnp.where(bucket == b, rb_ref[b, h], tile)
        o_ref[0, kind] = jnp.where(dist >= 0, tile * LOG2E, NEG)


def _bias_tiles(rel_bias):
    t = ATT_TILE
    heads = rel_bias.shape[1]
    return pl.pallas_call(
        _bias_kernel,
        grid=(heads,),
        in_specs=[pl.BlockSpec(memory_space=pltpu.SMEM)],
        out_specs=pl.BlockSpec((1, 3, t, t), lambda h: (h, 0, 0, 0)),
        out_shape=jax.ShapeDtypeStruct((heads, 3, t, t), F32),
        compiler_params=pltpu.CompilerParams(dimension_semantics=("arbitrary",)),
        name="bias_tiles",
    )(rel_bias.astype(F32))


def _norm_modulate(x, g, shift, scale):
    ms = jnp.mean(x * x, axis=-1, keepdims=True)
    y = x * lax.rsqrt(ms + EPS) * g
    return y * (1.0 + scale) + shift


def _inproj_kernel(x_ref, mod_ref, g_ref, w_ref, o_ref, *, tn):
    u = _norm_modulate(x_ref[...], g_ref[...], mod_ref[0, 0:1, :], mod_ref[0, 1:2, :])
    u = u.astype(BF16)
    for j in range(w_ref.shape[1] // tn):
        cols = slice(j * tn, (j + 1) * tn)
        o_ref[:, cols] = _nn(u, w_ref[:, cols]).astype(o_ref.dtype)


def _inproj(x2, mod_l, g, w_packed, layer, seq):
    m, d = x2.shape
    n = w_packed.shape[2]
    tm = min(512, seq)
    return pl.pallas_call(
        functools.partial(_inproj_kernel, tn=n // 3),
        grid=(m // tm,),
        in_specs=[
            pl.BlockSpec((tm, d), lambda i: (i, 0)),
            pl.BlockSpec((1, 6, d), lambda i: ((i * tm) // seq, 0, 0)),
            pl.BlockSpec((1, d), lambda i: (0, 0)),
            pl.BlockSpec((None, d, n), lambda i: (layer, 0, 0), pipeline_mode=pl.Buffered(1)),
        ],
        out_specs=pl.BlockSpec((tm, n), lambda i: (i, 0)),
        out_shape=jax.ShapeDtypeStruct((m, n), BF16),
        compiler_params=pltpu.CompilerParams(
            dimension_semantics=("parallel",), vmem_limit_bytes=VMEM_LIMIT),
        name="in_proj",
    )(x2, mod_l, g.reshape(1, d), w_packed)


def _attend(n_live, lives, chains, tile_ctx, logits, values, s_ref, m_ref, a_ref, acc_ref,
            *, unroll):
    def init():
        m_ref[...] = jnp.full(m_ref.shape, NEG, F32)
        acc_ref[...] = jnp.zeros(acc_ref.shape, F32)

    def step(j, slot, kind, start=True, finish=True):
        ctx = tile_ctx(j) if start else None
        for c in range(chains):
            if start:
                s = logits(j, c, ctx, kind)
                m_old = m_ref[1 - slot, c]
                m_new = jnp.maximum(m_old, jnp.max(s, axis=0, keepdims=True))
                s_ref[slot, c] = s
                a_ref[slot, c] = jnp.exp2(m_old - m_new)
                m_ref[slot, c] = m_new
            if finish:
                p = jnp.exp2(s_ref[1 - slot, c] - m_ref[1 - slot, c])
                acc_ref[c] = (a_ref[1 - slot, c] * acc_ref[c]
                              + _nn(values(j - 1, c), p.astype(BF16)))

    if unroll:
        def variant(live):
            init()
            for j in range(live + 1):
                step(j, j % 2, min(live - 1 - j, 2), start=j < live, finish=j > 0)

        for live in lives:
            pl.when(n_live == live)(functools.partial(variant, live))
        return

    init()
    step(0, 0, jnp.minimum(n_live - 1, 2), finish=False)

    def body(j, carry):
        for slot in range(2):
            pl.when(j % 2 == slot)(
                functools.partial(step, j, slot, jnp.minimum(n_live - 1 - j, 2)))
        return carry

    lax.fori_loop(1, n_live, body, 0)
    for slot in range(2):
        pl.when(n_live % 2 == slot)(functools.partial(step, n_live, slot, None, start=False))


def _attend_scratch(chains, dv):
    t = ATT_TILE
    return [
        pltpu.VMEM((2, chains, t, t), F32),
        pltpu.VMEM((2, chains, 1, t), F32),
        pltpu.VMEM((2, chains, 1, t), F32),
        pltpu.VMEM((chains, dv + SUM_ROWS, t), F32),
    ]


def _attend_result(acc_ref, c, dv):
    return acc_ref[c, 0:dv, :] / acc_ref[c, dv:dv + 1, :]


def _key_tile(ref, j, cols=None):
    t = ATT_TILE
    rows = slice(j * t, (j + 1) * t) if isinstance(j, int) else pl.ds(pl.multiple_of(j * t, t), t)
    return ref[0, rows, :] if cols is None else ref[0, rows, cols]


def _transpose_tiles(src_ref, dst_ref, n_tiles, groups):
    t = ATT_TILE
    eye = _eye(LANES)
    for g in range(groups):
        for j in range(n_tiles):
            blk = src_ref[0, j * t:(j + 1) * t, g * LANES:(g + 1) * LANES]
            dst_ref[g, j, 0:LANES, :] = _nt(eye, blk).astype(BF16)
            dst_ref[g, j, LANES:LANES + SUM_ROWS, :] = jnp.ones((SUM_ROWS, t), BF16)


def _diff_kernel(q_ref, k_ref, v_ref, bias_ref, dl_ref, g_ref, o_ref,
                 vt_ref, qs_ref, s_ref, m_ref, a_ref, acc_ref, *, lam_init, n_tiles, hp):
    t = ATT_TILE
    i = pl.program_id(2)

    @pl.when(i == 0)
    def _():
        _transpose_tiles(v_ref, vt_ref, n_tiles, hp)

    scale = HEAD_DIM ** -0.5 * LOG2E
    eye_l = _eye(LANES)
    for h in range(hp):
        q = q_ref[0, :, h * LANES:(h + 1) * LANES]
        for half in range(2):
            qs_ref[2 * h + half] = _transposed(_lane_group_mask(q, half, HEAD_DIM) * scale, eye_l)

    def logits(j, c, ctx, kind):
        h = c // 2
        kblk = _key_tile(k_ref, j, slice(h * LANES, (h + 1) * LANES))
        return _nn(kblk, qs_ref[c]) + bias_ref[h, kind]

    _attend(i + 1, range(1, n_tiles + 1), 2 * hp, lambda j: None, logits, lambda j, c: vt_ref[c // 2, j],
            s_ref, m_ref, a_ref, acc_ref, unroll=True)

    dl = dl_ref[...]
    lam = (jnp.exp(jnp.sum(dl[0:1] * dl[1:2], keepdims=True))
           - jnp.exp(jnp.sum(dl[2:3] * dl[3:4], keepdims=True)) + lam_init)
    eye = _eye(t)
    for h in range(hp):
        o = (_attend_result(acc_ref, 2 * h, DIFF_V_DIM)
             - lam * _attend_result(acc_ref, 2 * h + 1, DIFF_V_DIM))
        ms = jnp.mean(o * o, axis=0, keepdims=True)
        o = o * lax.rsqrt(ms + EPS) * g_ref[...] * (1.0 - lam_init)
        o_ref[0, :, h * LANES:(h + 1) * LANES] = _nt(eye, o.astype(BF16)).astype(o_ref.dtype)


def _diff_attention(p3, bias_tiles, diff_lambda_l, subln_g, lam_init):
    bsz, seq, _ = p3.shape
    t = ATT_TILE
    n_tiles = seq // t
    hp = DIFF_HEADS_PER_STEP
    w = hp * LANES
    kern = functools.partial(_diff_kernel, lam_init=lam_init, n_tiles=n_tiles, hp=hp)
    return pl.pallas_call(
        kern,
        grid=(bsz, DIFF_HEADS // hp, n_tiles),
        in_specs=[
            pl.BlockSpec((1, t, w), lambda b, h, i: (b, i, BLK_AQ // hp + h)),
            pl.BlockSpec((1, seq, w), lambda b, h, i: (b, 0, BLK_AK // hp + h)),
            pl.BlockSpec((1, seq, w), lambda b, h, i: (b, 0, BLK_AV // hp + h)),
            pl.BlockSpec((hp, 3, t, t), lambda b, h, i: (h, 0, 0, 0)),
            pl.BlockSpec((4, HEAD_DIM), lambda b, h, i: (0, 0)),
            pl.BlockSpec((DIFF_V_DIM, 1), lambda b, h, i: (0, 0)),
        ],
        out_specs=pl.BlockSpec((1, t, w), lambda b, h, i: (b, i, h)),
        out_shape=jax.ShapeDtypeStruct((bsz, seq, DIFF_HEADS * DIFF_V_DIM), BF16),
        scratch_shapes=[
            pltpu.VMEM((hp, n_tiles, LANES + SUM_ROWS, t), BF16),
            pltpu.VMEM((2 * hp, LANES, t), BF16),
        ] + _attend_scratch(2 * hp, DIFF_V_DIM),
        compiler_params=pltpu.CompilerParams(
            dimension_semantics=("parallel", "parallel", "arbitrary"),
            vmem_limit_bytes=VMEM_LIMIT),
        name="diff_attn",
    )(p3, p3, p3, bias_tiles, diff_lambda_l, subln_g.reshape(DIFF_V_DIM, 1))


def _dsa_kernel(q_ref, kv_ref, iq_ref, ik_ref, iw_ref, bias_ref, g_ref, wuv_ref, o_ref,
                kvn_ref, kvt_ref, keys_ref, byte_ref, cand_ref, thr_ref, cut_ref, need_ref, tied_ref,
                iqh_ref, qt_ref,
                s_ref, m_ref, a_ref, acc_ref, *, n_tiles, first_tile, last_tile, topk):
    t = ATT_TILE
    i = pl.program_id(1) + first_tile
    n_live = i + 1
    lives = range(first_tile + 1, last_tile + 2)

    @pl.when(pl.program_id(1) == 0)
    def _():
        kv = kv_ref[0].astype(F32)
        ms = jnp.mean(kv * kv, axis=-1, keepdims=True)
        kvn_ref[0] = (kv * lax.rsqrt(ms + EPS) * g_ref[...]).astype(BF16)
        _transpose_tiles(kvn_ref, kvt_ref, n_tiles, 1)

    groups = LANES // IDX_DIM
    eye_l = _eye(LANES)
    for h in range(IDX_HEADS):
        blk = iq_ref[0, :, (h // groups) * LANES:(h // groups + 1) * LANES]
        iqh_ref[h] = _transposed(_lane_group_mask(blk, h % groups, IDX_DIM), eye_l)
    for h in range(DSA_HEADS):
        qt_ref[h] = _nt(eye_l, q_ref[0, :, h * LANES:(h + 1) * LANES]).astype(BF16)
    sel_rows = lax.broadcasted_iota(jnp.int32, (IDX_HEADS, LANES), 0)
    sel_cols = lax.broadcasted_iota(jnp.int32, (IDX_HEADS, LANES), 1)
    pick = jnp.where(sel_rows == sel_cols, 1.0, 0.0).astype(BF16)
    iw_t = _nt(pick, iw_ref[0])
    idx_scale = (IDX_HEADS ** -0.5) * (IDX_DIM ** -0.5)
    row = lax.broadcasted_iota(jnp.int32, (t, t), 0)
    col = lax.broadcasted_iota(jnp.int32, (t, t), 1)

    def score_tile(j, diagonal):
        ik = _key_tile(ik_ref, j)
        sc = jnp.zeros((t, t), F32)
        for h in range(IDX_HEADS):
            sc = sc + jnp.maximum(_nn(ik, iqh_ref[h]), 0.0) * iw_t[h:h + 1, :]
        sc = sc * idx_scale
        bits = pltpu.bitcast(sc, jnp.int32)
        key = bits ^ ((bits >> 31) & 0x7FFFFFFF)
        top = (key >> 24) + 128
        if diagonal:
            valid = col >= row
            key = jnp.where(valid, key, INT_MIN)
            top = jnp.where(valid, top, -1)
        keys_ref[j] = key
        cand_ref[j] = top.astype(F32).astype(BF16)
        for b in range(3):
            byte_ref[b, j] = ((key >> (16 - 8 * b)) & 0xFF).astype(F32).astype(BF16)

    one_b, zero_b = jnp.ones((), BF16), jnp.zeros((), BF16)
    packed_rows = 16

    def radix_select(live):
        for j in range(live):
            score_tile(j, diagonal=(j == live - 1))

        def count(hit_fn):
            part = None
            for j in range(live):
                hit = jnp.where(hit_fn(cand_ref[j]), one_b, zero_b)
                for r in range(t // packed_rows):
                    rows = hit[packed_rows * r:packed_rows * (r + 1)]
                    part = rows if part is None else part + rows
            return jnp.sum(part.astype(F32), axis=0, keepdims=True)

        if live * t <= topk:
            thr_ref[...] = jnp.full((1, t), INT_MIN, jnp.int32)
            need_ref[...] = jnp.zeros((1, t), F32)
            tied_ref[...] = jnp.zeros((1, t), F32)
            return

        need = jnp.full((1, t), topk, F32)
        thr = jnp.zeros((1, t), jnp.int32)
        for b in range(4):
            def bit_body(it, carry, need=need):
                val, n_above = carry
                cand = val | lax.shift_left(jnp.int32(1), 7 - it)
                cand_b = cand.astype(F32).astype(BF16)
                cnt = count(lambda x: x >= cand_b)
                ok = cnt >= need
                return jnp.where(ok, cand, val), jnp.where(ok, n_above, cnt)

            val, n_above = lax.fori_loop(
                0, 8, bit_body, (jnp.zeros((1, t), jnp.int32), jnp.zeros((1, t), F32)))
            val_b = val.astype(F32).astype(BF16)
            need = need - n_above
            if b < 3:
                for j in range(live):
                    cand_ref[j] = jnp.where(cand_ref[j] == val_b, byte_ref[b, j], -one_b)
            else:
                tied_ref[...] = count(lambda x: x == val_b)
            piece = val - (128 if b == 0 else 0)
            thr = thr | lax.shift_left(piece, 24 - 8 * b)
        thr_ref[...] = thr
        need_ref[...] = need

    assert (t // packed_rows) * n_tiles <= 256
    for live in lives:
        pl.when(n_live == live)(functools.partial(radix_select, live))

    thr = thr_ref[...]
    need = need_ref[...]
    has_rank = thr > INT_MIN
    excess = jnp.where(has_rank & (tied_ref[...] > need), 1.0, 0.0)
    cut_ref[...] = jnp.where(has_rank, jnp.int32(2 * n_tiles * t), jnp.int32(0))

    @pl.when(jnp.max(excess) > 0.0)
    def _():
        past_end = float(2 * n_tiles * t)
        earlier = jnp.where(col < row, 1.0, 0.0).astype(BF16)

        def tie_body(j, carry):
            seen, cut = carry
            tie = keys_ref[j] == thr
            before = seen + _nn(earlier, jnp.where(tie, 1.0, 0.0).astype(BF16))
            pos = (j * t + row).astype(F32)
            dropped = jnp.where(tie & (before >= need), pos, past_end)
            cut = jnp.minimum(cut, jnp.min(dropped, axis=0, keepdims=True))
            seen = seen + jnp.sum(jnp.where(tie, 1.0, 0.0), axis=0, keepdims=True)
            return seen, cut

        _, cut = lax.fori_loop(0, n_live, tie_body,
                               (jnp.zeros((1, t), F32), jnp.full((1, t), past_end, F32)))
        cut_ref[...] = jnp.where(has_rank, cut.astype(jnp.int32), jnp.int32(0))

    scale = DSA_LATENT ** -0.5 * LOG2E

    def tile_ctx(j):
        key = keys_ref[j]
        thr_v = thr_ref[...]
        sel = (key > thr_v) | ((key == thr_v) & ((j * t + row) < cut_ref[...]))
        return sel, _key_tile(kvn_ref, j)

    def logits(j, h, ctx, kind):
        sel, kvb = ctx
        return jnp.where(sel, _nn(kvb, qt_ref[h]) * scale + bias_ref[h, kind], NEG)

    _attend(n_live, lives, DSA_HEADS, tile_ctx, logits, lambda j, h: kvt_ref[0, j],
            s_ref, m_ref, a_ref, acc_ref, unroll=True)

    y_t = jnp.zeros((DSA_HEADS * DSA_V_DIM, t), F32)
    for h in range(DSA_HEADS):
        o_h = _attend_result(acc_ref, h, DSA_LATENT).astype(BF16)
        y_t = y_t + _nn(wuv_ref[h], o_h)
    o_ref[0] = _nt(_eye(t), y_t.astype(BF16)).astype(o_ref.dtype)


def _dsa_attention(p3, bias_tiles, kv_norm_g, wuv_t_pad):
    n_tiles = p3.shape[1] // ATT_TILE
    split = n_tiles - (3 * n_tiles) // 8
    parts = [_dsa_attention_part(p3, bias_tiles, kv_norm_g, wuv_t_pad, lo, hi)
             for lo, hi in ((0, split - 1), (split, n_tiles - 1)) if hi >= lo]
    return jnp.concatenate(parts, axis=1)


def _dsa_attention_part(p3, bias_tiles, kv_norm_g, wuv_t_pad, first_tile, last_tile):
    bsz, seq, _ = p3.shape
    t = ATT_TILE
    n_tiles = seq // t
    n_q = last_tile - first_tile + 1
    topk = min(DSA_TOPK_MAX, seq // 4)
    kern = functools.partial(_dsa_kernel, n_tiles=n_tiles, first_tile=first_tile,
                             last_tile=last_tile, topk=float(topk))
    n_out = DSA_HEADS * DSA_V_DIM
    return pl.pallas_call(
        kern,
        grid=(bsz, n_q),
        in_specs=[
            pl.BlockSpec((1, t, 4 * LANES), lambda b, i: (b, i + first_tile, BLK_BQ // 4)),
            pl.BlockSpec((1, seq, LANES), lambda b, i: (b, 0, BLK_BKV)),
            pl.BlockSpec((1, t, 2 * LANES), lambda b, i: (b, i + first_tile, BLK_BIQ // 2)),
            pl.BlockSpec((1, seq, LANES), lambda b, i: (b, 0, BLK_IK)),
            pl.BlockSpec((1, t, LANES), lambda b, i: (b, i + first_tile, BLK_IW)),
            pl.BlockSpec((DSA_HEADS, 3, t, t), lambda b, i: (DIFF_HEADS // DSA_HEADS, 0, 0, 0)),
            pl.BlockSpec((1, DSA_LATENT), lambda b, i: (0, 0)),
            pl.BlockSpec((DSA_HEADS, n_out, DSA_LATENT), lambda b, i: (0, 0, 0)),
        ],
        out_specs=pl.BlockSpec((1, t, n_out), lambda b, i: (b, i, 0)),
        out_shape=jax.ShapeDtypeStruct((bsz, n_q * t, n_out), BF16),
        scratch_shapes=[
            pltpu.VMEM((1, seq, DSA_LATENT), BF16),
            pltpu.VMEM((1, n_tiles, DSA_LATENT + SUM_ROWS, t), BF16),
            pltpu.VMEM((n_tiles, t, t), jnp.int32),
            pltpu.VMEM((3, n_tiles, t, t), BF16),
            pltpu.VMEM((n_tiles, t, t), BF16),
            pltpu.VMEM((1, t), jnp.int32),
            pltpu.VMEM((1, t), jnp.int32),
            pltpu.VMEM((1, t), F32),
            pltpu.VMEM((1, t), F32),
            pltpu.VMEM((IDX_HEADS, LANES, t), BF16),
            pltpu.VMEM((DSA_HEADS, LANES, t), BF16),
        ] + _attend_scratch(DSA_HEADS, DSA_LATENT),
        compiler_params=pltpu.CompilerParams(
            dimension_semantics=("parallel", "arbitrary"), vmem_limit_bytes=VMEM_LIMIT),
        name="dsa_attn",
    )(p3, p3, p3, p3, p3, bias_tiles, kv_norm_g.reshape(1, DSA_LATENT), wuv_t_pad)


def _moba_kernel(q_ref, k_ref, v_ref, bias_ref, o_ref,
                 vt_ref, kmean_ref, selb_ref, qs_ref,
                 s_ref, m_ref, a_ref, acc_ref, *, n_tiles, topb):
    t = ATT_TILE
    i = pl.program_id(1)
    pairs = MOBA_HEADS // 2

    @pl.when(i == 0)
    def _():
        _transpose_tiles(v_ref, vt_ref, n_tiles, pairs)
        for n in range(n_tiles):
            kb = k_ref[0, n * t:(n + 1) * t, :].astype(F32)
            kmean_ref[n:n + 1, :] = jnp.mean(kb, axis=0, keepdims=True)

    scale = HEAD_DIM ** -0.5
    eye_l = _eye(LANES)
    blk = lax.broadcasted_iota(jnp.int32, (n_tiles, t), 0)
    own = jnp.full((1, t), i, jnp.int32)
    for hd in range(MOBA_HEADS):
        g, half = hd // 2, hd % 2
        q_f = _lane_group_mask(q_ref[0, :, g * LANES:(g + 1) * LANES], half, HEAD_DIM)
        qs_ref[hd] = _transposed(q_f * (scale * LOG2E), eye_l)
        kmean = kmean_ref[:, g * LANES:(g + 1) * LANES].astype(BF16)
        gate = _nt(kmean, q_f.astype(BF16))
        for n in range(n_tiles):
            gn = gate[n:n + 1, :]
            ahead = (gate > gn) | ((gate == gn) & (blk < n))
            ahead = ahead & (blk < i)
            rank = jnp.sum(jnp.where(ahead, 1.0, 0.0), axis=0, keepdims=True)
            chosen = ((rank < topb) & (own > n)) | (own == n)
            selb_ref[hd, n] = jnp.broadcast_to(jnp.where(chosen, 0.0, NEG), (8, t))

    def logits(j, hd, ctx, kind):
        g = hd // 2
        kblk = _key_tile(k_ref, j, slice(g * LANES, (g + 1) * LANES))
        return _nn(kblk, qs_ref[hd]) + bias_ref[hd, kind] + selb_ref[hd, j][0:1, :]

    _attend(i + 1, range(1, n_tiles + 1), MOBA_HEADS, lambda j: None, logits, lambda j, hd: vt_ref[hd // 2, j],
            s_ref, m_ref, a_ref, acc_ref, unroll=True)

    rows = lax.broadcasted_iota(jnp.int32, (LANES, t), 0)
    eye = _eye(t)
    for g in range(pairs):
        o_lo = _attend_result(acc_ref, 2 * g, LANES)
        o_hi = _attend_result(acc_ref, 2 * g + 1, LANES)
        o = jnp.where(rows < HEAD_DIM, o_lo, o_hi)
        o_ref[0, :, g * LANES:(g + 1) * LANES] = _nt(eye, o.astype(BF16)).astype(o_ref.dtype)


def _moba_attention(p3, bias_tiles):
    bsz, seq, _ = p3.shape
    t = ATT_TILE
    n_tiles = seq // t
    topb = min(MOBA_TOPK_MAX, n_tiles)
    kern = functools.partial(_moba_kernel, n_tiles=n_tiles, topb=float(topb))
    w = MOBA_HEADS * HEAD_DIM
    blocks = w // LANES
    first = (DIFF_HEADS + DSA_HEADS) // MOBA_HEADS
    return pl.pallas_call(
        kern,
        grid=(bsz, n_tiles),
        in_specs=[
            pl.BlockSpec((1, t, w), lambda b, i: (b, i, BLK_CQ // blocks)),
            pl.BlockSpec((1, seq, w), lambda b, i: (b, 0, BLK_CK // blocks)),
            pl.BlockSpec((1, seq, w), lambda b, i: (b, 0, BLK_CV // blocks)),
            pl.BlockSpec((MOBA_HEADS, 3, t, t), lambda b, i: (first, 0, 0, 0)),
        ],
        out_specs=pl.BlockSpec((1, t, w), lambda b, i: (b, i, 0)),
        out_shape=jax.ShapeDtypeStruct((bsz, seq, w), BF16),
        scratch_shapes=[
            pltpu.VMEM((blocks, n_tiles, LANES + SUM_ROWS, t), BF16),
            pltpu.VMEM((n_tiles, w), F32),
            pltpu.VMEM((MOBA_HEADS, n_tiles, 8, t), F32),
            pltpu.VMEM((MOBA_HEADS, LANES, t), BF16),
        ] + _attend_scratch(MOBA_HEADS, LANES),
        compiler_params=pltpu.CompilerParams(
            dimension_semantics=("parallel", "arbitrary"), vmem_limit_bytes=VMEM_LIMIT),
        name="moba_attn",
    )(p3, p3, p3, bias_tiles)


def _merge_kernel(x_ref, g_ref, gb_ref, oa_ref, ob_ref, oc_ref, wa_ref, wb_ref, wc_ref, wo_ref,
                  mod_ref, o_ref):
    d = D_MODEL
    gates = jax.nn.sigmoid(g_ref[...].astype(F32) + gb_ref[...])
    merged = (gates[:, 0:d] * _nn(oa_ref[...], wa_ref[...])
              + gates[:, d:2 * d] * _nn(ob_ref[...], wb_ref[...])
              + gates[:, 2 * d:3 * d] * _nn(oc_ref[...], wc_ref[...]))
    z = _nn(merged.astype(BF16), wo_ref[...])
    o_ref[...] = x_ref[...] + mod_ref[0, 2:3, :] * z


def _merge(x2, p2, gate_b, oa, ob, oc, wa, wb, wc, wo, layer, mod_l, seq):
    m, d = x2.shape
    tm = min(512, seq)
    full = lambda a: pl.BlockSpec((None,) + a.shape[1:], lambda i: (layer, 0, 0))
    return pl.pallas_call(
        _merge_kernel,
        grid=(m // tm,),
        in_specs=[
            pl.BlockSpec((tm, d), lambda i: (i, 0)),
            pl.BlockSpec((tm, 3 * d), lambda i: (i, BLK_G)),
            pl.BlockSpec((1, 3 * d), lambda i: (0, 0)),
            pl.BlockSpec((tm, oa.shape[1]), lambda i: (i, 0)),
            pl.BlockSpec((tm, ob.shape[1]), lambda i: (i, 0)),
            pl.BlockSpec((tm, oc.shape[1]), lambda i: (i, 0)),
            full(wa), full(wb), full(wc), full(wo),
            pl.BlockSpec((1, 6, d), lambda i: ((i * tm) // seq, 0, 0)),
        ],
        out_specs=pl.BlockSpec((tm, d), lambda i: (i, 0)),
        out_shape=jax.ShapeDtypeStruct((m, d), F32),
        compiler_params=pltpu.CompilerParams(
            dimension_semantics=("parallel",), vmem_limit_bytes=VMEM_LIMIT),
        name="merge",
    )(x2, p2, gate_b.reshape(1, 3 * d), oa, ob, oc, wa, wb, wc, wo, mod_l)


def _mlp_kernel(x_ref, mod_ref, g_ref, w1_ref, w2_ref, gf_ref, o_ref, *, final, tf):
    x = x_ref[...]
    u = _norm_modulate(x, g_ref[...], mod_ref[0, 3:4, :], mod_ref[0, 4:5, :]).astype(BF16)
    acc = None
    for f in range(w1_ref.shape[1] // tf):
        h = jnp.square(jnp.maximum(_nn(u, w1_ref[:, f * tf:(f + 1) * tf]), 0.0))
        part = _nn(h.astype(BF16), w2_ref[f * tf:(f + 1) * tf, :])
        acc = part if acc is None else acc + part
    y = x + mod_ref[0, 5:6, :] * acc
    if final:
        ms = jnp.mean(y * y, axis=-1, keepdims=True)
        y = y * lax.rsqrt(ms + EPS) * gf_ref[...]
    o_ref[...] = y


def _mlp(x2, mod_l, g, w1, w2, layer, g_final, seq, final):
    m, d = x2.shape
    dff = w1.shape[2]
    tm = min(512, seq)
    resident = lambda shape: pl.BlockSpec((None,) + shape, lambda i: (layer, 0, 0),
                                          pipeline_mode=pl.Buffered(1))
    return pl.pallas_call(
        functools.partial(_mlp_kernel, final=final, tf=1024),
        grid=(m // tm,),
        in_specs=[
            pl.BlockSpec((tm, d), lambda i: (i, 0)),
            pl.BlockSpec((1, 6, d), lambda i: ((i * tm) // seq, 0, 0)),
            pl.BlockSpec((1, d), lambda i: (0, 0)),
            resident((d, dff)),
            resident((dff, d)),
            pl.BlockSpec((1, d), lambda i: (0, 0)),
        ],
        out_specs=pl.BlockSpec((tm, d), lambda i: (i, 0)),
        out_shape=jax.ShapeDtypeStruct((m, d), F32),
        compiler_params=pltpu.CompilerParams(
            dimension_semantics=("parallel",), vmem_limit_bytes=VMEM_LIMIT),
        name="mlp",
    )(x2, mod_l, g.reshape(1, d), w1, w2, g_final.reshape(1, d))


def kernel(x, c, rel_bias, ada_w, ada_b, norm_mix, w_in, gate_b, diff_lambda, diff_subln,
           dsa_kv_norm, dsa_w_uv, w_br_a, w_br_b, w_br_c, w_o, norm_mlp, w_ff1, w_ff2,
           norm_final):
    bsz, seq, d = x.shape
    depth = w_in.shape[0]
    assert seq % ATT_TILE == 0 and 2 * ATT_TILE > MAX_DISTANCE

    w_in_p = _pack_columns(w_in.astype(BF16))
    n_out = DSA_HEADS * DSA_V_DIM
    wuv_t = jnp.transpose(dsa_w_uv, (0, 1, 3, 2))
    wuv_t_pad = jnp.zeros((depth, DSA_HEADS, n_out, DSA_LATENT), F32)
    for h in range(DSA_HEADS):
        wuv_t_pad = wuv_t_pad.at[:, h, h * DSA_V_DIM:(h + 1) * DSA_V_DIM, :].set(wuv_t[:, h])
    wuv_t_pad = wuv_t_pad.astype(BF16)
    wa, wb, wc, wo = (w.astype(BF16) for w in (w_br_a, w_br_b, w_br_c, w_o))
    w1, w2 = w_ff1.astype(BF16), w_ff2.astype(BF16)

    bias_tiles = _bias_tiles(rel_bias)
    mod = _ada(c, ada_w, ada_b).reshape(depth, bsz, 6, d)

    x2 = x.reshape(bsz * seq, d)
    for l in range(depth):
        lam_init = 0.8 - 0.6 * math.exp(-0.3 * l)
        p2 = _inproj(x2, mod[l], norm_mix[l], w_in_p, l, seq)
        p3 = p2.reshape(bsz, seq, PACKED_COLS)
        oa = _diff_attention(p3, bias_tiles, diff_lambda[l], diff_subln[l], lam_init)
        ob = _dsa_attention(p3, bias_tiles, dsa_kv_norm[l], wuv_t_pad[l])
        oc = _moba_attention(p3, bias_tiles)
        x2 = _merge(x2, p2, gate_b[l],
                    oa.reshape(bsz * seq, -1), ob.reshape(bsz * seq, -1), oc.reshape(bsz * seq, -1),
                    wa, wb, wc, wo, l, mod[l], seq)
        x2 = _mlp(x2, mod[l], norm_mlp[l], w1, w2, l, norm_final, seq, final=(l == depth - 1))
    return x2.reshape(bsz, seq, d)
```

```python
import functools
import math

import jax
import jax.numpy as jnp
from jax import lax
from jax.experimental import pallas as pl
from jax.experimental.pallas import tpu as pltpu

F32 = jnp.float32
BF16 = jnp.bfloat16

D_MODEL = 1024
HEAD_DIM = 64
DIFF_HEADS = 4
DIFF_V_DIM = 2 * HEAD_DIM
DSA_HEADS = 4
DSA_LATENT = 128
DSA_V_DIM = 64
IDX_HEADS = 8
IDX_DIM = 32
DSA_TOPK_MAX = 256
MOBA_HEADS = 4
MOBA_BLOCK = 256
MOBA_TOPK_MAX = 3
N_BUCKETS = 32
MAX_DISTANCE = 128
N_BIAS_HEADS = DIFF_HEADS + DSA_HEADS + MOBA_HEADS
D_FF = 4 * D_MODEL
EPS = 1e-6

LANES = 128
ATT_TILE = MOBA_BLOCK
NEG = -1e30
INT_MIN = -2 ** 31
VMEM_LIMIT = 52 * 1024 * 1024
DIFF_HEADS_PER_STEP = 4
SUM_ROWS = 16
LOG2E = math.log2(math.e)

_O_AQ, _O_AK, _O_AV, _O_BQ, _O_BKV, _O_BIQ, _O_BIK, _O_BIW, _O_CQ, _O_CK, _O_CV, _O_G = (
    0, 512, 1024, 1536, 2048, 2176, 2432, 2464, 2472, 2728, 2984, 3240)

BLK_G, BLK_AQ, BLK_AK, BLK_AV, BLK_BQ, BLK_BIQ, BLK_BKV, BLK_IK, BLK_CQ, BLK_CK, BLK_CV, BLK_IW = (
    0, 24, 28, 32, 36, 40, 42, 43, 44, 46, 48, 50)
N_BLKS = 51
PACKED_COLS = N_BLKS * LANES


def _pack_columns(w):
    seg = lambda off, n: w[..., off:off + n]
    ik = seg(_O_BIK, IDX_DIM)
    parts = [
        seg(_O_G, 3 * D_MODEL), seg(_O_AQ, 512), seg(_O_AK, 512), seg(_O_AV, 512),
        seg(_O_BQ, 512), seg(_O_BIQ, 256), seg(_O_BKV, 128),
        ik, ik, ik, ik,
        seg(_O_CQ, 256), seg(_O_CK, 256), seg(_O_CV, 256),
        seg(_O_BIW, IDX_HEADS),
        jnp.zeros(w.shape[:-1] + (LANES - IDX_HEADS,), w.dtype),
    ]
    out = jnp.concatenate(parts, axis=-1)
    assert out.shape[-1] == PACKED_COLS
    return out


def _nt(a, b):
    return lax.dot_general(a, b, (((1,), (1,)), ((), ())), preferred_element_type=F32)


def _nn(a, b):
    return jnp.dot(a, b, preferred_element_type=F32)


def _eye(n):
    r = lax.broadcasted_iota(jnp.int32, (n, n), 0)
    c = lax.broadcasted_iota(jnp.int32, (n, n), 1)
    return jnp.where(r == c, 1.0, 0.0).astype(BF16)


def _transposed(x_f32, eye):
    return _nt(eye, x_f32.astype(BF16)).astype(BF16)


def _lane_group_mask(x_bf16, group, width):
    lane = lax.broadcasted_iota(jnp.int32, x_bf16.shape, 1)
    keep = (lane >= group * width) & (lane < (group + 1) * width)
    return jnp.where(keep, x_bf16.astype(F32), 0.0)


def _ada_kernel(c_ref, w_ref, b_ref, o_ref):
    c = c_ref[...]
    cond = c * jax.nn.sigmoid(c)
    o_ref[0] = _nn(cond, w_ref[0]) + b_ref[0]


def _ada(c, ada_w, ada_b):
    depth, d, n = ada_w.shape
    bsz = c.shape[0]
    tn = 1536
    return pl.pallas_call(
        _ada_kernel,
        grid=(depth, n // tn),
        in_specs=[
            pl.BlockSpec((bsz, d), lambda l, j: (0, 0)),
            pl.BlockSpec((1, d, tn), lambda l, j: (l, 0, j)),
            pl.BlockSpec((1, 1, tn), lambda l, j: (l, 0, j)),
        ],
        out_specs=pl.BlockSpec((1, bsz, tn), lambda l, j: (l, 0, j)),
        out_shape=jax.ShapeDtypeStruct((depth, bsz, n), F32),
        compiler_params=pltpu.CompilerParams(
            dimension_semantics=("arbitrary", "arbitrary"), vmem_limit_bytes=VMEM_LIMIT),
        name="ada_mod",
    )(c, ada_w, ada_b.reshape(depth, 1, n))


def _t5_bucket(dist):
    max_exact = N_BUCKETS // 2
    n = jnp.maximum(dist, 0)
    nf = jnp.maximum(n, 1).astype(F32)
    large = max_exact + (jnp.log(nf / max_exact) / math.log(MAX_DISTANCE / max_exact)
                         * (N_BUCKETS - max_exact)).astype(jnp.int32)
    large = jnp.minimum(large, N_BUCKETS - 1)
    return jnp.where(n < max_exact, n, large)


def _bias_kernel(rb_ref, o_ref):
    t = ATT_TILE
    h = pl.program_id(0)
    kk = lax.broadcasted_iota(jnp.int32, (t, t), 0)
    qq = lax.broadcasted_iota(jnp.int32, (t, t), 1)
    for kind in range(3):
        dist = jnp.full((t, t), 2 * t, jnp.int32) if kind == 2 else kind * t + qq - kk
        bucket = _t5_bucket(dist)
        tile = jnp.zeros((t, t), F32)
        for b in range(N_BUCKETS):
            tile = jnp.where(bucket == b, rb_ref[b, h], tile)
        o_ref[0, kind] = jnp.where(dist >= 0, tile * LOG2E, NEG)


def _bias_tiles(rel_bias):
    t = ATT_TILE
    heads = rel_bias.shape[1]
    return pl.pallas_call(
        _bias_kernel,
        grid=(heads,),
        in_specs=[pl.BlockSpec(memory_space=pltpu.SMEM)],
        out_specs=pl.BlockSpec((1, 3, t, t), lambda h: (h, 0, 0, 0)),
        out_shape=jax.ShapeDtypeStruct((heads, 3, t, t), F32),
        compiler_params=pltpu.CompilerParams(dimension_semantics=("arbitrary",)),
        name="bias_tiles",
    )(rel_bias.astype(F32))


def _norm_modulate(x, g, shift, scale):
    ms = jnp.mean(x * x, axis=-1, keepdims=True)
    y = x * lax.rsqrt(ms + EPS) * g
    return y * (1.0 + scale) + shift


def _inproj_kernel(x_ref, mod_ref, g_ref, w_ref, o_ref, *, tn):
    u = _norm_modulate(x_ref[...], g_ref[...], mod_ref[0, 0:1, :], mod_ref[0, 1:2, :])
    u = u.astype(BF16)
    for j in range(w_ref.shape[1] // tn):
        cols = slice(j * tn, (j + 1) * tn)
        o_ref[:, cols] = _nn(u, w_ref[:, cols]).astype(o_ref.dtype)


def _inproj(x2, mod_l, g, w_packed, layer, seq):
    m, d = x2.shape
    n = w_packed.shape[2]
    tm = min(512, seq)
    return pl.pallas_call(
        functools.partial(_inproj_kernel, tn=n // 3),
        grid=(m // tm,),
        in_specs=[
            pl.BlockSpec((tm, d), lambda i: (i, 0)),
            pl.BlockSpec((1, 6, d), lambda i: ((i * tm) // seq, 0, 0)),
            pl.BlockSpec((1, d), lambda i: (0, 0)),
            pl.BlockSpec((None, d, n), lambda i: (layer, 0, 0), pipeline_mode=pl.Buffered(1)),
        ],
        out_specs=pl.BlockSpec((tm, n), lambda i: (i, 0)),
        out_shape=jax.ShapeDtypeStruct((m, n), BF16),
        compiler_params=pltpu.CompilerParams(
            dimension_semantics=("parallel",), vmem_limit_bytes=VMEM_LIMIT),
        name="in_proj",
    )(x2, mod_l, g.reshape(1, d), w_packed)


def _attend(n_live, lives, chains, tile_ctx, logits, values, s_ref, m_ref, a_ref, acc_ref,
            *, unroll, before=None, after=None):
    def init():
        m_ref[...] = jnp.full(m_ref.shape, NEG, F32)
        acc_ref[...] = jnp.zeros(acc_ref.shape, F32)

    def step(j, slot, kind, start=True, finish=True):
        ctx = tile_ctx(j) if start else None
        for c in range(chains):
            if start:
                s = logits(j, c, ctx, kind)
                m_old = m_ref[1 - slot, c]
                m_new = jnp.maximum(m_old, jnp.max(s, axis=0, keepdims=True))
                s_ref[slot, c] = s
                a_ref[slot, c] = jnp.exp2(m_old - m_new)
                m_ref[slot, c] = m_new
            if finish:
                p = jnp.exp2(s_ref[1 - slot, c] - m_ref[1 - slot, c])
                acc_ref[c] = (a_ref[1 - slot, c] * acc_ref[c]
                              + _nn(values(j - 1, c), p.astype(BF16)))

    if unroll:
        def variant(live):
            if before is not None:
                before()
            init()
            for j in range(live + 1):
                step(j, j % 2, min(live - 1 - j, 2), start=j < live, finish=j > 0)
            if after is not None:
                after()

        for live in lives:
            pl.when(n_live == live)(functools.partial(variant, live))
        return

    if before is not None:
        before()
    init()
    step(0, 0, jnp.minimum(n_live - 1, 2), finish=False)

    def body(j, carry):
        for slot in range(2):
            pl.when(j % 2 == slot)(
                functools.partial(step, j, slot, jnp.minimum(n_live - 1 - j, 2)))
        return carry

    lax.fori_loop(1, n_live, body, 0)
    for slot in range(2):
        pl.when(n_live % 2 == slot)(functools.partial(step, n_live, slot, None, start=False))
    if after is not None:
        after()


def _attend_scratch(chains, dv):
    t = ATT_TILE
    return [
        pltpu.VMEM((2, chains, t, t), F32),
        pltpu.VMEM((2, chains, 1, t), F32),
        pltpu.VMEM((2, chains, 1, t), F32),
        pltpu.VMEM((chains, dv + SUM_ROWS, t), F32),
    ]


def _attend_result(acc_ref, c, dv):
    return acc_ref[c, 0:dv, :] / acc_ref[c, dv:dv + 1, :]


def _key_tile(ref, j, cols=None):
    t = ATT_TILE
    rows = slice(j * t, (j + 1) * t) if isinstance(j, int) else pl.ds(pl.multiple_of(j * t, t), t)
    return ref[0, rows, :] if cols is None else ref[0, rows, cols]


def _transpose_tiles(src_ref, dst_ref, n_tiles, groups):
    t = ATT_TILE
    eye = _eye(LANES)
    for g in range(groups):
        for j in range(n_tiles):
            blk = src_ref[0, j * t:(j + 1) * t, g * LANES:(g + 1) * LANES]
            dst_ref[g, j, 0:LANES, :] = _nt(eye, blk).astype(BF16)
            dst_ref[g, j, LANES:LANES + SUM_ROWS, :] = jnp.ones((SUM_ROWS, t), BF16)


def _diff_kernel(q_ref, k_ref, v_ref, bias_ref, dl_ref, g_ref, o_ref,
                 vt_ref, qs_ref, s_ref, m_ref, a_ref, acc_ref, *, lam_init, n_tiles, hp):
    t = ATT_TILE
    i = pl.program_id(2)

    @pl.when(i == 0)
    def _():
        _transpose_tiles(v_ref, vt_ref, n_tiles, hp)

    def prepare_queries():
        scale = HEAD_DIM ** -0.5 * LOG2E
        eye_l = _eye(LANES)
        for h in range(hp):
            q = q_ref[0, :, h * LANES:(h + 1) * LANES]
            for half in range(2):
                qs_ref[2 * h + half] = _transposed(
                    _lane_group_mask(q, half, HEAD_DIM) * scale, eye_l)

    def logits(j, c, ctx, kind):
        h = c // 2
        kblk = _key_tile(k_ref, j, slice(h * LANES, (h + 1) * LANES))
        return _nn(kblk, qs_ref[c]) + bias_ref[h, kind]

    def write_output():
        dl = dl_ref[...]
        lam = (jnp.exp(jnp.sum(dl[0:1] * dl[1:2], keepdims=True))
               - jnp.exp(jnp.sum(dl[2:3] * dl[3:4], keepdims=True)) + lam_init)
        eye = _eye(t)
        for h in range(hp):
            o = (_attend_result(acc_ref, 2 * h, DIFF_V_DIM)
                 - lam * _attend_result(acc_ref, 2 * h + 1, DIFF_V_DIM))
            ms = jnp.mean(o * o, axis=0, keepdims=True)
            o = o * lax.rsqrt(ms + EPS) * g_ref[...] * (1.0 - lam_init)
            o_ref[0, :, h * LANES:(h + 1) * LANES] = _nt(eye, o.astype(BF16)).astype(o_ref.dtype)

    _attend(i + 1, range(1, n_tiles + 1), 2 * hp, lambda j: None, logits,
            lambda j, c: vt_ref[c // 2, j], s_ref, m_ref, a_ref, acc_ref,
            unroll=True, before=prepare_queries, after=write_output)


def _diff_attention(p3, bias_tiles, diff_lambda_l, subln_g, lam_init):
    bsz, seq, _ = p3.shape
    t = ATT_TILE
    n_tiles = seq // t
    hp = DIFF_HEADS_PER_STEP
    w = hp * LANES
    kern = functools.partial(_diff_kernel, lam_init=lam_init, n_tiles=n_tiles, hp=hp)
    return pl.pallas_call(
        kern,
        grid=(bsz, DIFF_HEADS // hp, n_tiles),
        in_specs=[
            pl.BlockSpec((1, t, w), lambda b, h, i: (b, i, BLK_AQ // hp + h)),
            pl.BlockSpec((1, seq, w), lambda b, h, i: (b, 0, BLK_AK // hp + h)),
            pl.BlockSpec((1, seq, w), lambda b, h, i: (b, 0, BLK_AV // hp + h)),
            pl.BlockSpec((hp, 3, t, t), lambda b, h, i: (h, 0, 0, 0)),
            pl.BlockSpec((4, HEAD_DIM), lambda b, h, i: (0, 0)),
            pl.BlockSpec((DIFF_V_DIM, 1), lambda b, h, i: (0, 0)),
        ],
        out_specs=pl.BlockSpec((1, t, w), lambda b, h, i: (b, i, h)),
        out_shape=jax.ShapeDtypeStruct((bsz, seq, DIFF_HEADS * DIFF_V_DIM), BF16),
        scratch_shapes=[
            pltpu.VMEM((hp, n_tiles, LANES + SUM_ROWS, t), BF16),
            pltpu.VMEM((2 * hp, LANES, t), BF16),
        ] + _attend_scratch(2 * hp, DIFF_V_DIM),
        compiler_params=pltpu.CompilerParams(
            dimension_semantics=("parallel", "parallel", "arbitrary"),
            vmem_limit_bytes=VMEM_LIMIT),
        name="diff_attn",
    )(p3, p3, p3, bias_tiles, diff_lambda_l, subln_g.reshape(DIFF_V_DIM, 1))


def _dsa_kernel(q_ref, kv_ref, iq_ref, ik_ref, iw_ref, bias_ref, g_ref, wuv_ref, o_ref,
                kvn_ref, kvt_ref, keys_ref, byte_ref, cand_ref, thr_ref, cut_ref, need_ref, tied_ref,
                iqh_ref, qt_ref,
                s_ref, m_ref, a_ref, acc_ref, *, n_tiles, first_tile, last_tile, topk):
    t = ATT_TILE
    i = pl.program_id(1) + first_tile
    n_live = i + 1
    lives = range(first_tile + 1, last_tile + 2)

    @pl.when(pl.program_id(1) == 0)
    def _():
        kv = kv_ref[0].astype(F32)
        ms = jnp.mean(kv * kv, axis=-1, keepdims=True)
        kvn_ref[0] = (kv * lax.rsqrt(ms + EPS) * g_ref[...]).astype(BF16)
        _transpose_tiles(kvn_ref, kvt_ref, n_tiles, 1)

    groups = LANES // IDX_DIM
    eye_l = _eye(LANES)
    for h in range(IDX_HEADS):
        blk = iq_ref[0, :, (h // groups) * LANES:(h // groups + 1) * LANES]
        iqh_ref[h] = _transposed(_lane_group_mask(blk, h % groups, IDX_DIM), eye_l)
    for h in range(DSA_HEADS):
        qt_ref[h] = _nt(eye_l, q_ref[0, :, h * LANES:(h + 1) * LANES]).astype(BF16)
    sel_rows = lax.broadcasted_iota(jnp.int32, (IDX_HEADS, LANES), 0)
    sel_cols = lax.broadcasted_iota(jnp.int32, (IDX_HEADS, LANES), 1)
    pick = jnp.where(sel_rows == sel_cols, 1.0, 0.0).astype(BF16)
    iw_t = _nt(pick, iw_ref[0])
    idx_scale = (IDX_HEADS ** -0.5) * (IDX_DIM ** -0.5)
    row = lax.broadcasted_iota(jnp.int32, (t, t), 0)
    col = lax.broadcasted_iota(jnp.int32, (t, t), 1)

    def score_tile(j, diagonal):
        ik = _key_tile(ik_ref, j)
        sc = jnp.zeros((t, t), F32)
        for h in range(IDX_HEADS):
            sc = sc + jnp.maximum(_nn(ik, iqh_ref[h]), 0.0) * iw_t[h:h + 1, :]
        sc = sc * idx_scale
        bits = pltpu.bitcast(sc, jnp.int32)
        key = bits ^ ((bits >> 31) & 0x7FFFFFFF)
        top = (key >> 24) + 128
        if diagonal:
            valid = col >= row
            key = jnp.where(valid, key, INT_MIN)
            top = jnp.where(valid, top, -1)
        keys_ref[j] = key
        cand_ref[j] = top.astype(F32).astype(BF16)
        for b in range(3):
            byte_ref[b, j] = ((key >> (16 - 8 * b)) & 0xFF).astype(F32).astype(BF16)

    one_b, zero_b = jnp.ones((), BF16), jnp.zeros((), BF16)
    packed_rows = 16

    def radix_select(live):
        for j in range(live):
            score_tile(j, diagonal=(j == live - 1))

        def count(hit_fn):
            part = None
            for j in range(live):
                hit = jnp.where(hit_fn(cand_ref[j]), one_b, zero_b)
                for r in range(t // packed_rows):
                    rows = hit[packed_rows * r:packed_rows * (r + 1)]
                    part = rows if part is None else part + rows
            return jnp.sum(part.astype(F32), axis=0, keepdims=True)

        if live * t <= topk:
            thr_ref[...] = jnp.full((1, t), INT_MIN, jnp.int32)
            need_ref[...] = jnp.zeros((1, t), F32)
            tied_ref[...] = jnp.zeros((1, t), F32)
            return

        need = jnp.full((1, t), topk, F32)
        thr = jnp.zeros((1, t), jnp.int32)
        for b in range(4):
            def bit_body(it, carry, need=need):
                val, n_above = carry
                cand = val | lax.shift_left(jnp.int32(1), 7 - it)
                cand_b = cand.astype(F32).astype(BF16)
                cnt = count(lambda x: x >= cand_b)
                ok = cnt >= need
                return jnp.where(ok, cand, val), jnp.where(ok, n_above, cnt)

            val, n_above = lax.fori_loop(
                0, 8, bit_body, (jnp.zeros((1, t), jnp.int32), jnp.zeros((1, t), F32)))
            val_b = val.astype(F32).astype(BF16)
            need = need - n_above
            if b < 3:
                for j in range(live):
                    cand_ref[j] = jnp.where(cand_ref[j] == val_b, byte_ref[b, j], -one_b)
            else:
                tied_ref[...] = count(lambda x: x == val_b)
            piece = val - (128 if b == 0 else 0)
            thr = thr | lax.shift_left(piece, 24 - 8 * b)
        thr_ref[...] = thr
        need_ref[...] = need

    assert (t // packed_rows) * n_tiles <= 256
    for live in lives:
        pl.when(n_live == live)(functools.partial(radix_select, live))

    thr = thr_ref[...]
    need = need_ref[...]
    has_rank = thr > INT_MIN
    excess = jnp.where(has_rank & (tied_ref[...] > need), 1.0, 0.0)
    cut_ref[...] = jnp.where(has_rank, jnp.int32(2 * n_tiles * t), jnp.int32(0))

    @pl.when(jnp.max(excess) > 0.0)
    def _():
        past_end = float(2 * n_tiles * t)
        earlier = jnp.where(col < row, 1.0, 0.0).astype(BF16)

        def tie_body(j, carry):
            seen, cut = carry
            tie = keys_ref[j] == thr
            before = seen + _nn(earlier, jnp.where(tie, 1.0, 0.0).astype(BF16))
            pos = (j * t + row).astype(F32)
            dropped = jnp.where(tie & (before >= need), pos, past_end)
            cut = jnp.minimum(cut, jnp.min(dropped, axis=0, keepdims=True))
            seen = seen + jnp.sum(jnp.where(tie, 1.0, 0.0), axis=0, keepdims=True)
            return seen, cut

        _, cut = lax.fori_loop(0, n_live, tie_body,
                               (jnp.zeros((1, t), F32), jnp.full((1, t), past_end, F32)))
        cut_ref[...] = jnp.where(has_rank, cut.astype(jnp.int32), jnp.int32(0))

    scale = DSA_LATENT ** -0.5 * LOG2E

    def tile_ctx(j):
        key = keys_ref[j]
        thr_v = thr_ref[...]
        sel = (key > thr_v) | ((key == thr_v) & ((j * t + row) < cut_ref[...]))
        return sel, _key_tile(kvn_ref, j)

    def logits(j, h, ctx, kind):
        sel, kvb = ctx
        return jnp.where(sel, _nn(kvb, qt_ref[h]) * scale + bias_ref[h, kind], NEG)

    _attend(n_live, lives, DSA_HEADS, tile_ctx, logits, lambda j, h: kvt_ref[0, j],
            s_ref, m_ref, a_ref, acc_ref, unroll=True)

    y_t = jnp.zeros((DSA_HEADS * DSA_V_DIM, t), F32)
    for h in range(DSA_HEADS):
        o_h = _attend_result(acc_ref, h, DSA_LATENT).astype(BF16)
        y_t = y_t + _nn(wuv_ref[h], o_h)
    o_ref[0] = _nt(_eye(t), y_t.astype(BF16)).astype(o_ref.dtype)


def _dsa_attention(p3, bias_tiles, kv_norm_g, wuv_t_pad):
    n_tiles = p3.shape[1] // ATT_TILE
    split = n_tiles - (3 * n_tiles) // 8
    parts = [_dsa_attention_part(p3, bias_tiles, kv_norm_g, wuv_t_pad, lo, hi)
             for lo, hi in ((0, split - 1), (split, n_tiles - 1)) if hi >= lo]
    return jnp.concatenate(parts, axis=1)


def _dsa_attention_part(p3, bias_tiles, kv_norm_g, wuv_t_pad, first_tile, last_tile):
    bsz, seq, _ = p3.shape
    t = ATT_TILE
    n_tiles = seq // t
    n_q = last_tile - first_tile + 1
    topk = min(DSA_TOPK_MAX, seq // 4)
    kern = functools.partial(_dsa_kernel, n_tiles=n_tiles, first_tile=first_tile,
                             last_tile=last_tile, topk=float(topk))
    n_out = DSA_HEADS * DSA_V_DIM
    return pl.pallas_call(
        kern,
        grid=(bsz, n_q),
        in_specs=[
            pl.BlockSpec((1, t, 4 * LANES), lambda b, i: (b, i + first_tile, BLK_BQ // 4)),
            pl.BlockSpec((1, seq, LANES), lambda b, i: (b, 0, BLK_BKV)),
            pl.BlockSpec((1, t, 2 * LANES), lambda b, i: (b, i + first_tile, BLK_BIQ // 2)),
            pl.BlockSpec((1, seq, LANES), lambda b, i: (b, 0, BLK_IK)),
            pl.BlockSpec((1, t, LANES), lambda b, i: (b, i + first_tile, BLK_IW)),
            pl.BlockSpec((DSA_HEADS, 3, t, t), lambda b, i: (DIFF_HEADS // DSA_HEADS, 0, 0, 0)),
            pl.BlockSpec((1, DSA_LATENT), lambda b, i: (0, 0)),
            pl.BlockSpec((DSA_HEADS, n_out, DSA_LATENT), lambda b, i: (0, 0, 0)),
        ],
        out_specs=pl.BlockSpec((1, t, n_out), lambda b, i: (b, i, 0)),
        out_shape=jax.ShapeDtypeStruct((bsz, n_q * t, n_out), BF16),
        scratch_shapes=[
            pltpu.VMEM((1, seq, DSA_LATENT), BF16),
            pltpu.VMEM((1, n_tiles, DSA_LATENT + SUM_ROWS, t), BF16),
            pltpu.VMEM((n_tiles, t, t), jnp.int32),
            pltpu.VMEM((3, n_tiles, t, t), BF16),
            pltpu.VMEM((n_tiles, t, t), BF16),
            pltpu.VMEM((1, t), jnp.int32),
            pltpu.VMEM((1, t), jnp.int32),
            pltpu.VMEM((1, t), F32),
            pltpu.VMEM((1, t), F32),
            pltpu.VMEM((IDX_HEADS, LANES, t), BF16),
            pltpu.VMEM((DSA_HEADS, LANES, t), BF16),
        ] + _attend_scratch(DSA_HEADS, DSA_LATENT),
        compiler_params=pltpu.CompilerParams(
            dimension_semantics=("parallel", "arbitrary"), vmem_limit_bytes=VMEM_LIMIT),
        name="dsa_attn",
    )(p3, p3, p3, p3, p3, bias_tiles, kv_norm_g.reshape(1, DSA_LATENT), wuv_t_pad)


def _moba_kernel(q_ref, k_ref, v_ref, bias_ref, o_ref,
                 vt_ref, kmean_ref, selb_ref, qs_ref,
                 s_ref, m_ref, a_ref, acc_ref, *, n_tiles, topb):
    t = ATT_TILE
    i = pl.program_id(1)
    pairs = MOBA_HEADS // 2

    @pl.when(i == 0)
    def _():
        _transpose_tiles(v_ref, vt_ref, n_tiles, pairs)
        for n in range(n_tiles):
            kb = k_ref[0, n * t:(n + 1) * t, :].astype(F32)
            kmean_ref[n:n + 1, :] = jnp.mean(kb, axis=0, keepdims=True)

    def prepare_queries_and_gates():
        scale = HEAD_DIM ** -0.5
        eye_l = _eye(LANES)
        blk = lax.broadcasted_iota(jnp.int32, (n_tiles, t), 0)
        own = jnp.full((1, t), i, jnp.int32)
        for hd in range(MOBA_HEADS):
            g, half = hd // 2, hd % 2
            q_f = _lane_group_mask(q_ref[0, :, g * LANES:(g + 1) * LANES], half, HEAD_DIM)
            qs_ref[hd] = _transposed(q_f * (scale * LOG2E), eye_l)
            kmean = kmean_ref[:, g * LANES:(g + 1) * LANES].astype(BF16)
            gate = _nt(kmean, q_f.astype(BF16))
            for n in range(n_tiles):
                gn = gate[n:n + 1, :]
                ahead = (gate > gn) | ((gate == gn) & (blk < n))
                ahead = ahead & (blk < i)
                rank = jnp.sum(jnp.where(ahead, 1.0, 0.0), axis=0, keepdims=True)
                chosen = ((rank < topb) & (own > n)) | (own == n)
                selb_ref[hd, n] = jnp.broadcast_to(jnp.where(chosen, 0.0, NEG), (8, t))

    def logits(j, hd, ctx, kind):
        g = hd // 2
        kblk = _key_tile(k_ref, j, slice(g * LANES, (g + 1) * LANES))
        return _nn(kblk, qs_ref[hd]) + bias_ref[hd, kind] + selb_ref[hd, j][0:1, :]

    def write_output():
        rows = lax.broadcasted_iota(jnp.int32, (LANES, t), 0)
        eye = _eye(t)
        for g in range(pairs):
            o_lo = _attend_result(acc_ref, 2 * g, LANES)
            o_hi = _attend_result(acc_ref, 2 * g + 1, LANES)
            o = jnp.where(rows < HEAD_DIM, o_lo, o_hi)
            o_ref[0, :, g * LANES:(g + 1) * LANES] = _nt(eye, o.astype(BF16)).astype(o_ref.dtype)

    _attend(i + 1, range(1, n_tiles + 1), MOBA_HEADS, lambda j: None, logits,
            lambda j, hd: vt_ref[hd // 2, j], s_ref, m_ref, a_ref, acc_ref,
            unroll=True, before=prepare_queries_and_gates, after=write_output)


def _moba_attention(p3, bias_tiles):
    bsz, seq, _ = p3.shape
    t = ATT_TILE
    n_tiles = seq // t
    topb = min(MOBA_TOPK_MAX, n_tiles)
    kern = functools.partial(_moba_kernel, n_tiles=n_tiles, topb=float(topb))
    w = MOBA_HEADS * HEAD_DIM
    blocks = w // LANES
    first = (DIFF_HEADS + DSA_HEADS) // MOBA_HEADS
    return pl.pallas_call(
        kern,
        grid=(bsz, n_tiles),
        in_specs=[
            pl.BlockSpec((1, t, w), lambda b, i: (b, i, BLK_CQ // blocks)),
            pl.BlockSpec((1, seq, w), lambda b, i: (b, 0, BLK_CK // blocks)),
            pl.BlockSpec((1, seq, w), lambda b, i: (b, 0, BLK_CV // blocks)),
            pl.BlockSpec((MOBA_HEADS, 3, t, t), lambda b, i: (first, 0, 0, 0)),
        ],
        out_specs=pl.BlockSpec((1, t, w), lambda b, i: (b, i, 0)),
        out_shape=jax.ShapeDtypeStruct((bsz, seq, w), BF16),
        scratch_shapes=[
            pltpu.VMEM((blocks, n_tiles, LANES + SUM_ROWS, t), BF16),
            pltpu.VMEM((n_tiles, w), F32),
            pltpu.VMEM((MOBA_HEADS, n_tiles, 8, t), F32),
            pltpu.VMEM((MOBA_HEADS, LANES, t), BF16),
        ] + _attend_scratch(MOBA_HEADS, LANES),
        compiler_params=pltpu.CompilerParams(
            dimension_semantics=("parallel", "arbitrary"), vmem_limit_bytes=VMEM_LIMIT),
        name="moba_attn",
    )(p3, p3, p3, bias_tiles)


def _merge_kernel(x_ref, g_ref, gb_ref, oa_ref, ob_ref, oc_ref, wa_ref, wb_ref, wc_ref, wo_ref,
                  mod_ref, o_ref):
    d = D_MODEL
    gates = jax.nn.sigmoid(g_ref[...].astype(F32) + gb_ref[...])
    merged = (gates[:, 0:d] * _nn(oa_ref[...], wa_ref[...])
              + gates[:, d:2 * d] * _nn(ob_ref[...], wb_ref[...])
              + gates[:, 2 * d:3 * d] * _nn(oc_ref[...], wc_ref[...]))
    z = _nn(merged.astype(BF16), wo_ref[...])
    o_ref[...] = x_ref[...] + mod_ref[0, 2:3, :] * z


def _merge(x2, p2, gate_b, oa, ob, oc, wa, wb, wc, wo, layer, mod_l, seq):
    m, d = x2.shape
    tm = min(512, seq)
    full = lambda a: pl.BlockSpec((None,) + a.shape[1:], lambda i: (layer, 0, 0))
    return pl.pallas_call(
        _merge_kernel,
        grid=(m // tm,),
        in_specs=[
            pl.BlockSpec((tm, d), lambda i: (i, 0)),
            pl.BlockSpec((tm, 3 * d), lambda i: (i, BLK_G)),
            pl.BlockSpec((1, 3 * d), lambda i: (0, 0)),
            pl.BlockSpec((tm, oa.shape[1]), lambda i: (i, 0)),
            pl.BlockSpec((tm, ob.shape[1]), lambda i: (i, 0)),
            pl.BlockSpec((tm, oc.shape[1]), lambda i: (i, 0)),
            full(wa), full(wb), full(wc), full(wo),
            pl.BlockSpec((1, 6, d), lambda i: ((i * tm) // seq, 0, 0)),
        ],
        out_specs=pl.BlockSpec((tm, d), lambda i: (i, 0)),
        out_shape=jax.ShapeDtypeStruct((m, d), F32),
        compiler_params=pltpu.CompilerParams(
            dimension_semantics=("parallel",), vmem_limit_bytes=VMEM_LIMIT),
        name="merge",
    )(x2, p2, gate_b.reshape(1, 3 * d), oa, ob, oc, wa, wb, wc, wo, mod_l)


def _mlp_kernel(x_ref, mod_ref, g_ref, w1_ref, w2_ref, gf_ref, o_ref, *, final, tf):
    x = x_ref[...]
    u = _norm_modulate(x, g_ref[...], mod_ref[0, 3:4, :], mod_ref[0, 4:5, :]).astype(BF16)
    acc = None
    for f in range(w1_ref.shape[1] // tf):
        h = jnp.square(jnp.maximum(_nn(u, w1_ref[:, f * tf:(f + 1) * tf]), 0.0))
        part = _nn(h.astype(BF16), w2_ref[f * tf:(f + 1) * tf, :])
        acc = part if acc is None else acc + part
    y = x + mod_ref[0, 5:6, :] * acc
    if final:
        ms = jnp.mean(y * y, axis=-1, keepdims=True)
        y = y * lax.rsqrt(ms + EPS) * gf_ref[...]
    o_ref[...] = y


def _mlp(x2, mod_l, g, w1, w2, layer, g_final, seq, final):
    m, d = x2.shape
    dff = w1.shape[2]
    tm = min(512, seq)
    resident = lambda shape: pl.BlockSpec((None,) + shape, lambda i: (layer, 0, 0),
                                          pipeline_mode=pl.Buffered(1))
    return pl.pallas_call(
        functools.partial(_mlp_kernel, final=final, tf=1024),
        grid=(m // tm,),
        in_specs=[
            pl.BlockSpec((tm, d), lambda i: (i, 0)),
            pl.BlockSpec((1, 6, d), lambda i: ((i * tm) // seq, 0, 0)),
            pl.BlockSpec((1, d), lambda i: (0, 0)),
            resident((d, dff)),
            resident((dff, d)),
            pl.BlockSpec((1, d), lambda i: (0, 0)),
        ],
        out_specs=pl.BlockSpec((tm, d), lambda i: (i, 0)),
        out_shape=jax.ShapeDtypeStruct((m, d), F32),
        compiler_params=pltpu.CompilerParams(
            dimension_semantics=("parallel",), vmem_limit_bytes=VMEM_LIMIT),
        name="mlp",
    )(x2, mod_l, g.reshape(1, d), w1, w2, g_final.reshape(1, d))


def kernel(x, c, rel_bias, ada_w, ada_b, norm_mix, w_in, gate_b, diff_lambda, diff_subln,
           dsa_kv_norm, dsa_w_uv, w_br_a, w_br_b, w_br_c, w_o, norm_mlp, w_ff1, w_ff2,
           norm_final):
    bsz, seq, d = x.shape
    depth = w_in.shape[0]
    assert seq % ATT_TILE == 0 and 2 * ATT_TILE > MAX_DISTANCE

    w_in_p = _pack_columns(w_in.astype(BF16))
    n_out = DSA_HEADS * DSA_V_DIM
    wuv_t = jnp.transpose(dsa_w_uv, (0, 1, 3, 2))
    wuv_t_pad = jnp.zeros((depth, DSA_HEADS, n_out, DSA_LATENT), F32)
    for h in range(DSA_HEADS):
        wuv_t_pad = wuv_t_pad.at[:, h, h * DSA_V_DIM:(h + 1) * DSA_V_DIM, :].set(wuv_t[:, h])
    wuv_t_pad = wuv_t_pad.astype(BF16)
    wa, wb, wc, wo = (w.astype(BF16) for w in (w_br_a, w_br_b, w_br_c, w_o))
    w1, w2 = w_ff1.astype(BF16), w_ff2.astype(BF16)

    bias_tiles = _bias_tiles(rel_bias)
    mod = _ada(c, ada_w, ada_b).reshape(depth, bsz, 6, d)

    x2 = x.reshape(bsz * seq, d)
    for l in range(depth):
        lam_init = 0.8 - 0.6 * math.exp(-0.3 * l)
        p2 = _inproj(x2, mod[l], norm_mix[l], w_in_p, l, seq)
        p3 = p2.reshape(bsz, seq, PACKED_COLS)
        oa = _diff_attention(p3, bias_tiles, diff_lambda[l], diff_subln[l], lam_init)
        ob = _dsa_attention(p3, bias_tiles, dsa_kv_norm[l], wuv_t_pad[l])
        oc = _moba_attention(p3, bias_tiles)
        x2 = _merge(x2, p2, gate_b[l],
                    oa.reshape(bsz * seq, -1), ob.reshape(bsz * seq, -1), oc.reshape(bsz * seq, -1),
                    wa, wb, wc, wo, l, mod[l], seq)
        x2 = _mlp(x2, mod[l], norm_mlp[l], w1, w2, l, norm_final, seq, final=(l == depth - 1))
    return x2.reshape(bsz, seq, d)
```

```python
import functools
import math

import jax
import jax.numpy as jnp
from jax import lax
from jax.experimental import pallas as pl
from jax.experimental.pallas import tpu as pltpu

F32 = jnp.float32
BF16 = jnp.bfloat16

D_MODEL = 1024
HEAD_DIM = 64
DIFF_HEADS = 4
DIFF_V_DIM = 2 * HEAD_DIM
DSA_HEADS = 4
DSA_LATENT = 128
DSA_V_DIM = 64
IDX_HEADS = 8
IDX_DIM = 32
DSA_TOPK_MAX = 256
MOBA_HEADS = 4
MOBA_BLOCK = 256
MOBA_TOPK_MAX = 3
N_BUCKETS = 32
MAX_DISTANCE = 128
N_BIAS_HEADS = DIFF_HEADS + DSA_HEADS + MOBA_HEADS
D_FF = 4 * D_MODEL
EPS = 1e-6

LANES = 128
ATT_TILE = MOBA_BLOCK
NEG = -1e30
INT_MIN = -2 ** 31
VMEM_LIMIT = 52 * 1024 * 1024
DIFF_HEADS_PER_STEP = 4
SUM_ROWS = 16
LOG2E = math.log2(math.e)

_O_AQ, _O_AK, _O_AV, _O_BQ, _O_BKV, _O_BIQ, _O_BIK, _O_BIW, _O_CQ, _O_CK, _O_CV, _O_G = (
    0, 512, 1024, 1536, 2048, 2176, 2432, 2464, 2472, 2728, 2984, 3240)

BLK_G, BLK_AQ, BLK_AK, BLK_AV, BLK_BQ, BLK_BIQ, BLK_BKV, BLK_IK, BLK_CQ, BLK_CK, BLK_CV, BLK_IW = (
    0, 24, 28, 32, 36, 40, 42, 43, 44, 46, 48, 50)
N_BLKS = 51
PACKED_COLS = N_BLKS * LANES


def _pack_columns(w):
    seg = lambda off, n: w[..., off:off + n]
    ik = seg(_O_BIK, IDX_DIM)
    parts = [
        seg(_O_G, 3 * D_MODEL), seg(_O_AQ, 512), seg(_O_AK, 512), seg(_O_AV, 512),
        seg(_O_BQ, 512), seg(_O_BIQ, 256), seg(_O_BKV, 128),
        ik, ik, ik, ik,
        seg(_O_CQ, 256), seg(_O_CK, 256), seg(_O_CV, 256),
        seg(_O_BIW, IDX_HEADS),
        jnp.zeros(w.shape[:-1] + (LANES - IDX_HEADS,), w.dtype),
    ]
    out = jnp.concatenate(parts, axis=-1)
    assert out.shape[-1] == PACKED_COLS
    return out


def _nt(a, b):
    return lax.dot_general(a, b, (((1,), (1,)), ((), ())), preferred_element_type=F32)


def _nn(a, b):
    return jnp.dot(a, b, preferred_element_type=F32)


def _eye(n):
    r = lax.broadcasted_iota(jnp.int32, (n, n), 0)
    c = lax.broadcasted_iota(jnp.int32, (n, n), 1)
    return jnp.where(r == c, 1.0, 0.0).astype(BF16)


def _transposed(x_f32, eye):
    return _nt(eye, x_f32.astype(BF16)).astype(BF16)


def _lane_group_mask(x_bf16, group, width):
    lane = lax.broadcasted_iota(jnp.int32, x_bf16.shape, 1)
    keep = (lane >= group * width) & (lane < (group + 1) * width)
    return jnp.where(keep, x_bf16.astype(F32), 0.0)


def _ada_kernel(c_ref, w_ref, b_ref, o_ref):
    c = c_ref[...]
    cond = c * jax.nn.sigmoid(c)
    o_ref[0] = _nn(cond, w_ref[0]) + b_ref[0]


def _ada(c, ada_w, ada_b):
    depth, d, n = ada_w.shape
    bsz = c.shape[0]
    tn = 1536
    return pl.pallas_call(
        _ada_kernel,
        grid=(depth, n // tn),
        in_specs=[
            pl.BlockSpec((bsz, d), lambda l, j: (0, 0)),
            pl.BlockSpec((1, d, tn), lambda l, j: (l, 0, j)),
            pl.BlockSpec((1, 1, tn), lambda l, j: (l, 0, j)),
        ],
        out_specs=pl.BlockSpec((1, bsz, tn), lambda l, j: (l, 0, j)),
        out_shape=jax.ShapeDtypeStruct((depth, bsz, n), F32),
        compiler_params=pltpu.CompilerParams(
            dimension_semantics=("arbitrary", "arbitrary"), vmem_limit_bytes=VMEM_LIMIT),
        name="ada_mod",
    )(c, ada_w, ada_b.reshape(depth, 1, n))


def _t5_bucket(dist):
    max_exact = N_BUCKETS // 2
    n = jnp.maximum(dist, 0)
    nf = jnp.maximum(n, 1).astype(F32)
    large = max_exact + (jnp.log(nf / max_exact) / math.log(MAX_DISTANCE / max_exact)
                         * (N_BUCKETS - max_exact)).astype(jnp.int32)
    large = jnp.minimum(large, N_BUCKETS - 1)
    return jnp.where(n < max_exact, n, large)


def _bias_kernel(rb_ref, o_ref):
    t = ATT_TILE
    h = pl.program_id(0)
    kk = lax.broadcasted_iota(jnp.int32, (t, t), 0)
    qq = lax.broadcasted_iota(jnp.int32, (t, t), 1)
    for kind in range(3):
        dist = jnp.full((t, t), 2 * t, jnp.int32) if kind == 2 else kind * t + qq - kk
        bucket = _t5_bucket(dist)
        tile = jnp.zeros((t, t), F32)
        for b in range(N_BUCKETS):
            tile = jnp.where(bucket == b, rb_ref[b, h], tile)
        o_ref[0, kind] = jnp.where(dist >= 0, tile * LOG2E, NEG)


def _bias_tiles(rel_bias):
    t = ATT_TILE
    heads = rel_bias.shape[1]
    return pl.pallas_call(
        _bias_kernel,
        grid=(heads,),
        in_specs=[pl.BlockSpec(memory_space=pltpu.SMEM)],
        out_specs=pl.BlockSpec((1, 3, t, t), lambda h: (h, 0, 0, 0)),
        out_shape=jax.ShapeDtypeStruct((heads, 3, t, t), F32),
        compiler_params=pltpu.CompilerParams(dimension_semantics=("arbitrary",)),
        name="bias_tiles",
    )(rel_bias.astype(F32))


def _norm_modulate(x, g, shift, scale):
    ms = jnp.mean(x * x, axis=-1, keepdims=True)
    y = x * lax.rsqrt(ms + EPS) * g
    return y * (1.0 + scale) + shift


def _inproj_kernel(x_ref, mod_ref, g_ref, w_ref, o_ref, *, tn):
    u = _norm_modulate(x_ref[...], g_ref[...], mod_ref[0, 0:1, :], mod_ref[0, 1:2, :])
    u = u.astype(BF16)
    for j in range(w_ref.shape[1] // tn):
        cols = slice(j * tn, (j + 1) * tn)
        o_ref[:, cols] = _nn(u, w_ref[:, cols]).astype(o_ref.dtype)


def _inproj(x2, mod_l, g, w_packed, layer, seq):
    m, d = x2.shape
    n = w_packed.shape[2]
    tm = min(512, seq)
    return pl.pallas_call(
        functools.partial(_inproj_kernel, tn=n // 3),
        grid=(m // tm,),
        in_specs=[
            pl.BlockSpec((tm, d), lambda i: (i, 0)),
            pl.BlockSpec((1, 6, d), lambda i: ((i * tm) // seq, 0, 0)),
            pl.BlockSpec((1, d), lambda i: (0, 0)),
            pl.BlockSpec((None, d, n), lambda i: (layer, 0, 0), pipeline_mode=pl.Buffered(1)),
        ],
        out_specs=pl.BlockSpec((tm, n), lambda i: (i, 0)),
        out_shape=jax.ShapeDtypeStruct((m, n), BF16),
        compiler_params=pltpu.CompilerParams(
            dimension_semantics=("parallel",), vmem_limit_bytes=VMEM_LIMIT),
        name="in_proj",
    )(x2, mod_l, g.reshape(1, d), w_packed)


def _attend(n_live, lives, chains, tile_ctx, logits, values, s_ref, m_ref, a_ref, acc_ref,
            *, unroll, before=None, after=None):
    def init():
        m_ref[...] = jnp.full(m_ref.shape, NEG, F32)
        acc_ref[...] = jnp.zeros(acc_ref.shape, F32)

    def step(j, slot, kind, start=True, finish=True):
        ctx = tile_ctx(j) if start else None
        for c in range(chains):
            if start:
                s = logits(j, c, ctx, kind)
                m_old = m_ref[1 - slot, c]
                m_new = jnp.maximum(m_old, jnp.max(s, axis=0, keepdims=True))
                s_ref[slot, c] = s
                a_ref[slot, c] = jnp.exp2(m_old - m_new)
                m_ref[slot, c] = m_new
            if finish:
                p = jnp.exp2(s_ref[1 - slot, c] - m_ref[1 - slot, c])
                acc_ref[c] = (a_ref[1 - slot, c] * acc_ref[c]
                              + _nn(values(j - 1, c), p.astype(BF16)))

    if unroll:
        def variant(live):
            if before is not None:
                before()
            init()
            for j in range(live + 1):
                step(j, j % 2, min(live - 1 - j, 2), start=j < live, finish=j > 0)
            if after is not None:
                after()

        for live in lives:
            pl.when(n_live == live)(functools.partial(variant, live))
        return

    if before is not None:
        before()
    init()
    step(0, 0, jnp.minimum(n_live - 1, 2), finish=False)

    def body(j, carry):
        for slot in range(2):
            pl.when(j % 2 == slot)(
                functools.partial(step, j, slot, jnp.minimum(n_live - 1 - j, 2)))
        return carry

    lax.fori_loop(1, n_live, body, 0)
    for slot in range(2):
        pl.when(n_live % 2 == slot)(functools.partial(step, n_live, slot, None, start=False))
    if after is not None:
        after()


def _attend_scratch(chains, dv):
    t = ATT_TILE
    return [
        pltpu.VMEM((2, chains, t, t), F32),
        pltpu.VMEM((2, chains, 1, t), F32),
        pltpu.VMEM((2, chains, 1, t), F32),
        pltpu.VMEM((chains, dv + SUM_ROWS, t), F32),
    ]


def _attend_result(acc_ref, c, dv):
    return acc_ref[c, 0:dv, :] / acc_ref[c, dv:dv + 1, :]


def _key_tile(ref, j, cols=None):
    t = ATT_TILE
    rows = slice(j * t, (j + 1) * t) if isinstance(j, int) else pl.ds(pl.multiple_of(j * t, t), t)
    return ref[0, rows, :] if cols is None else ref[0, rows, cols]


def _transpose_tiles(src_ref, dst_ref, n_tiles, groups):
    t = ATT_TILE
    eye = _eye(LANES)
    for g in range(groups):
        for j in range(n_tiles):
            blk = src_ref[0, j * t:(j + 1) * t, g * LANES:(g + 1) * LANES]
            dst_ref[g, j, 0:LANES, :] = _nt(eye, blk).astype(BF16)
            dst_ref[g, j, LANES:LANES + SUM_ROWS, :] = jnp.ones((SUM_ROWS, t), BF16)


def _diff_kernel(q_ref, k_ref, v_ref, bias_ref, dl_ref, g_ref, o_ref,
                 vt_ref, qs_ref, s_ref, m_ref, a_ref, acc_ref, *, lam_init, n_tiles, hp):
    t = ATT_TILE
    i = pl.program_id(2)

    @pl.when(i == 0)
    def _():
        _transpose_tiles(v_ref, vt_ref, n_tiles, hp)

    def prepare_queries():
        scale = HEAD_DIM ** -0.5 * LOG2E
        eye_l = _eye(LANES)
        for h in range(hp):
            q = q_ref[0, :, h * LANES:(h + 1) * LANES]
            for half in range(2):
                qs_ref[2 * h + half] = _transposed(
                    _lane_group_mask(q, half, HEAD_DIM) * scale, eye_l)

    def logits(j, c, ctx, kind):
        h = c // 2
        kblk = _key_tile(k_ref, j, slice(h * LANES, (h + 1) * LANES))
        return _nn(kblk, qs_ref[c]) + bias_ref[h, kind]

    def write_output():
        dl = dl_ref[...]
        lam = (jnp.exp(jnp.sum(dl[0:1] * dl[1:2], keepdims=True))
               - jnp.exp(jnp.sum(dl[2:3] * dl[3:4], keepdims=True)) + lam_init)
        eye = _eye(t)
        for h in range(hp):
            o = (_attend_result(acc_ref, 2 * h, DIFF_V_DIM)
                 - lam * _attend_result(acc_ref, 2 * h + 1, DIFF_V_DIM))
            ms = jnp.mean(o * o, axis=0, keepdims=True)
            o = o * lax.rsqrt(ms + EPS) * g_ref[...] * (1.0 - lam_init)
            o_ref[0, :, h * LANES:(h + 1) * LANES] = _nt(eye, o.astype(BF16)).astype(o_ref.dtype)

    _attend(i + 1, range(1, n_tiles + 1), 2 * hp, lambda j: None, logits,
            lambda j, c: vt_ref[c // 2, j], s_ref, m_ref, a_ref, acc_ref,
            unroll=True, before=prepare_queries, after=write_output)


def _diff_attention(p3, bias_tiles, diff_lambda_l, subln_g, lam_init):
    bsz, seq, _ = p3.shape
    t = ATT_TILE
    n_tiles = seq // t
    hp = DIFF_HEADS_PER_STEP
    w = hp * LANES
    kern = functools.partial(_diff_kernel, lam_init=lam_init, n_tiles=n_tiles, hp=hp)
    return pl.pallas_call(
        kern,
        grid=(bsz, DIFF_HEADS // hp, n_tiles),
        in_specs=[
            pl.BlockSpec((1, t, w), lambda b, h, i: (b, i, BLK_AQ // hp + h)),
            pl.BlockSpec((1, seq, w), lambda b, h, i: (b, 0, BLK_AK // hp + h)),
            pl.BlockSpec((1, seq, w), lambda b, h, i: (b, 0, BLK_AV // hp + h)),
            pl.BlockSpec((hp, 3, t, t), lambda b, h, i: (h, 0, 0, 0)),
            pl.BlockSpec((4, HEAD_DIM), lambda b, h, i: (0, 0)),
            pl.BlockSpec((DIFF_V_DIM, 1), lambda b, h, i: (0, 0)),
        ],
        out_specs=pl.BlockSpec((1, t, w), lambda b, h, i: (b, i, h)),
        out_shape=jax.ShapeDtypeStruct((bsz, seq, DIFF_HEADS * DIFF_V_DIM), BF16),
        scratch_shapes=[
            pltpu.VMEM((hp, n_tiles, LANES + SUM_ROWS, t), BF16),
            pltpu.VMEM((2 * hp, LANES, t), BF16),
        ] + _attend_scratch(2 * hp, DIFF_V_DIM),
        compiler_params=pltpu.CompilerParams(
            dimension_semantics=("parallel", "parallel", "arbitrary"),
            vmem_limit_bytes=VMEM_LIMIT),
        name="diff_attn",
    )(p3, p3, p3, bias_tiles, diff_lambda_l, subln_g.reshape(DIFF_V_DIM, 1))


def _dsa_kernel(q_ref, kv_ref, iq_ref, ik_ref, iw_ref, bias_ref, g_ref, wuv_ref, o_ref,
                kvn_ref, kvt_ref, keys_ref, byte_ref, cand_ref, thr_ref, cut_ref, need_ref, tied_ref,
                iqh_ref, qt_ref, iwt_ref,
                s_ref, m_ref, a_ref, acc_ref, *, n_tiles, first_tile, last_tile, topk):
    t = ATT_TILE
    i = pl.program_id(1) + first_tile
    n_live = i + 1
    lives = range(first_tile + 1, last_tile + 2)

    @pl.when(pl.program_id(1) == 0)
    def _():
        kv = kv_ref[0].astype(F32)
        ms = jnp.mean(kv * kv, axis=-1, keepdims=True)
        kvn_ref[0] = (kv * lax.rsqrt(ms + EPS) * g_ref[...]).astype(BF16)
        _transpose_tiles(kvn_ref, kvt_ref, n_tiles, 1)

    def prepare_queries():
        groups = LANES // IDX_DIM
        eye_l = _eye(LANES)
        for h in range(IDX_HEADS):
            blk = iq_ref[0, :, (h // groups) * LANES:(h // groups + 1) * LANES]
            iqh_ref[h] = _transposed(_lane_group_mask(blk, h % groups, IDX_DIM), eye_l)
        for h in range(DSA_HEADS):
            qt_ref[h] = _nt(eye_l, q_ref[0, :, h * LANES:(h + 1) * LANES]).astype(BF16)
        sel_rows = lax.broadcasted_iota(jnp.int32, (IDX_HEADS, LANES), 0)
        sel_cols = lax.broadcasted_iota(jnp.int32, (IDX_HEADS, LANES), 1)
        pick = jnp.where(sel_rows == sel_cols, 1.0, 0.0).astype(BF16)
        iwt_ref[...] = _nt(pick, iw_ref[0])

    idx_scale = (IDX_HEADS ** -0.5) * (IDX_DIM ** -0.5)
    row = lax.broadcasted_iota(jnp.int32, (t, t), 0)
    col = lax.broadcasted_iota(jnp.int32, (t, t), 1)

    def score_tile(j, diagonal):
        ik = _key_tile(ik_ref, j)
        sc = jnp.zeros((t, t), F32)
        for h in range(IDX_HEADS):
            sc = sc + jnp.maximum(_nn(ik, iqh_ref[h]), 0.0) * iwt_ref[h:h + 1, :]
        sc = sc * idx_scale
        bits = pltpu.bitcast(sc, jnp.int32)
        key = bits ^ ((bits >> 31) & 0x7FFFFFFF)
        top = (key >> 24) + 128
        if diagonal:
            valid = col >= row
            key = jnp.where(valid, key, INT_MIN)
            top = jnp.where(valid, top, -1)
        keys_ref[j] = key
        cand_ref[j] = top.astype(F32).astype(BF16)
        for b in range(3):
            byte_ref[b, j] = ((key >> (16 - 8 * b)) & 0xFF).astype(F32).astype(BF16)

    one_b, zero_b = jnp.ones((), BF16), jnp.zeros((), BF16)
    packed_rows = 16

    def radix_select(live):
        prepare_queries()
        for j in range(live):
            score_tile(j, diagonal=(j == live - 1))

        def count(hit_fn):
            part = None
            for j in range(live):
                hit = jnp.where(hit_fn(cand_ref[j]), one_b, zero_b)
                for r in range(t // packed_rows):
                    rows = hit[packed_rows * r:packed_rows * (r + 1)]
                    part = rows if part is None else part + rows
            return jnp.sum(part.astype(F32), axis=0, keepdims=True)

        if live * t <= topk:
            thr_ref[...] = jnp.full((1, t), INT_MIN, jnp.int32)
            need_ref[...] = jnp.zeros((1, t), F32)
            tied_ref[...] = jnp.zeros((1, t), F32)
            return

        need = jnp.full((1, t), topk, F32)
        thr = jnp.zeros((1, t), jnp.int32)
        for b in range(4):
            def bit_body(it, carry, need=need):
                val, n_above = carry
                cand = val | lax.shift_left(jnp.int32(1), 7 - it)
                cand_b = cand.astype(F32).astype(BF16)
                cnt = count(lambda x: x >= cand_b)
                ok = cnt >= need
                return jnp.where(ok, cand, val), jnp.where(ok, n_above, cnt)

            val, n_above = lax.fori_loop(
                0, 8, bit_body, (jnp.zeros((1, t), jnp.int32), jnp.zeros((1, t), F32)))
            val_b = val.astype(F32).astype(BF16)
            need = need - n_above
            if b < 3:
                for j in range(live):
                    cand_ref[j] = jnp.where(cand_ref[j] == val_b, byte_ref[b, j], -one_b)
            else:
                tied_ref[...] = count(lambda x: x == val_b)
            piece = val - (128 if b == 0 else 0)
            thr = thr | lax.shift_left(piece, 24 - 8 * b)
        thr_ref[...] = thr
        need_ref[...] = need

    assert (t // packed_rows) * n_tiles <= 256
    for live in lives:
        pl.when(n_live == live)(functools.partial(radix_select, live))

    thr = thr_ref[...]
    need = need_ref[...]
    has_rank = thr > INT_MIN
    excess = jnp.where(has_rank & (tied_ref[...] > need), 1.0, 0.0)
    cut_ref[...] = jnp.where(has_rank, jnp.int32(2 * n_tiles * t), jnp.int32(0))

    @pl.when(jnp.max(excess) > 0.0)
    def _():
        past_end = float(2 * n_tiles * t)
        earlier = jnp.where(col < row, 1.0, 0.0).astype(BF16)

        def tie_body(j, carry):
            seen, cut = carry
            tie = keys_ref[j] == thr
            before = seen + _nn(earlier, jnp.where(tie, 1.0, 0.0).astype(BF16))
            pos = (j * t + row).astype(F32)
            dropped = jnp.where(tie & (before >= need), pos, past_end)
            cut = jnp.minimum(cut, jnp.min(dropped, axis=0, keepdims=True))
            seen = seen + jnp.sum(jnp.where(tie, 1.0, 0.0), axis=0, keepdims=True)
            return seen, cut

        _, cut = lax.fori_loop(0, n_live, tie_body,
                               (jnp.zeros((1, t), F32), jnp.full((1, t), past_end, F32)))
        cut_ref[...] = jnp.where(has_rank, cut.astype(jnp.int32), jnp.int32(0))

    scale = DSA_LATENT ** -0.5 * LOG2E

    def tile_ctx(j):
        key = keys_ref[j]
        thr_v = thr_ref[...]
        sel = (key > thr_v) | ((key == thr_v) & ((j * t + row) < cut_ref[...]))
        return sel, _key_tile(kvn_ref, j)

    def logits(j, h, ctx, kind):
        sel, kvb = ctx
        return jnp.where(sel, _nn(kvb, qt_ref[h]) * scale + bias_ref[h, kind], NEG)

    def write_output():
        y_t = jnp.zeros((DSA_HEADS * DSA_V_DIM, t), F32)
        for h in range(DSA_HEADS):
            o_h = _attend_result(acc_ref, h, DSA_LATENT).astype(BF16)
            y_t = y_t + _nn(wuv_ref[h], o_h)
        o_ref[0] = _nt(_eye(t), y_t.astype(BF16)).astype(o_ref.dtype)

    _attend(n_live, lives, DSA_HEADS, tile_ctx, logits, lambda j, h: kvt_ref[0, j],
            s_ref, m_ref, a_ref, acc_ref, unroll=True, after=write_output)


def _dsa_attention(p3, bias_tiles, kv_norm_g, wuv_t_pad):
    n_tiles = p3.shape[1] // ATT_TILE
    cuts = [0, n_tiles // 2, (3 * n_tiles) // 4, n_tiles]
    parts = [_dsa_attention_part(p3, bias_tiles, kv_norm_g, wuv_t_pad, lo, hi - 1)
             for lo, hi in zip(cuts[:-1], cuts[1:]) if hi > lo]
    return jnp.concatenate(parts, axis=1)


def _dsa_attention_part(p3, bias_tiles, kv_norm_g, wuv_t_pad, first_tile, last_tile):
    bsz, seq, _ = p3.shape
    t = ATT_TILE
    n_tiles = seq // t
    n_q = last_tile - first_tile + 1
    topk = min(DSA_TOPK_MAX, seq // 4)
    kern = functools.partial(_dsa_kernel, n_tiles=n_tiles, first_tile=first_tile,
                             last_tile=last_tile, topk=float(topk))
    n_out = DSA_HEADS * DSA_V_DIM
    return pl.pallas_call(
        kern,
        grid=(bsz, n_q),
        in_specs=[
            pl.BlockSpec((1, t, 4 * LANES), lambda b, i: (b, i + first_tile, BLK_BQ // 4)),
            pl.BlockSpec((1, seq, LANES), lambda b, i: (b, 0, BLK_BKV)),
            pl.BlockSpec((1, t, 2 * LANES), lambda b, i: (b, i + first_tile, BLK_BIQ // 2)),
            pl.BlockSpec((1, seq, LANES), lambda b, i: (b, 0, BLK_IK)),
            pl.BlockSpec((1, t, LANES), lambda b, i: (b, i + first_tile, BLK_IW)),
            pl.BlockSpec((DSA_HEADS, 3, t, t), lambda b, i: (DIFF_HEADS // DSA_HEADS, 0, 0, 0)),
            pl.BlockSpec((1, DSA_LATENT), lambda b, i: (0, 0)),
            pl.BlockSpec((DSA_HEADS, n_out, DSA_LATENT), lambda b, i: (0, 0, 0)),
        ],
        out_specs=pl.BlockSpec((1, t, n_out), lambda b, i: (b, i, 0)),
        out_shape=jax.ShapeDtypeStruct((bsz, n_q * t, n_out), BF16),
        scratch_shapes=[
            pltpu.VMEM((1, seq, DSA_LATENT), BF16),
            pltpu.VMEM((1, n_tiles, DSA_LATENT + SUM_ROWS, t), BF16),
            pltpu.VMEM((n_tiles, t, t), jnp.int32),
            pltpu.VMEM((3, n_tiles, t, t), BF16),
            pltpu.VMEM((n_tiles, t, t), BF16),
            pltpu.VMEM((1, t), jnp.int32),
            pltpu.VMEM((1, t), jnp.int32),
            pltpu.VMEM((1, t), F32),
            pltpu.VMEM((1, t), F32),
            pltpu.VMEM((IDX_HEADS, LANES, t), BF16),
            pltpu.VMEM((DSA_HEADS, LANES, t), BF16),
            pltpu.VMEM((IDX_HEADS, t), F32),
        ] + _attend_scratch(DSA_HEADS, DSA_LATENT),
        compiler_params=pltpu.CompilerParams(
            dimension_semantics=("parallel", "arbitrary"), vmem_limit_bytes=VMEM_LIMIT),
        name="dsa_attn",
    )(p3, p3, p3, p3, p3, bias_tiles, kv_norm_g.reshape(1, DSA_LATENT), wuv_t_pad)


def _moba_kernel(q_ref, k_ref, v_ref, bias_ref, o_ref,
                 vt_ref, kmean_ref, selb_ref, qs_ref,
                 s_ref, m_ref, a_ref, acc_ref, *, n_tiles, topb):
    t = ATT_TILE
    i = pl.program_id(1)
    pairs = MOBA_HEADS // 2

    @pl.when(i == 0)
    def _():
        _transpose_tiles(v_ref, vt_ref, n_tiles, pairs)
        for n in range(n_tiles):
            kb = k_ref[0, n * t:(n + 1) * t, :].astype(F32)
            kmean_ref[n:n + 1, :] = jnp.mean(kb, axis=0, keepdims=True)

    def prepare_queries_and_gates():
        scale = HEAD_DIM ** -0.5
        eye_l = _eye(LANES)
        blk = lax.broadcasted_iota(jnp.int32, (n_tiles, t), 0)
        own = jnp.full((1, t), i, jnp.int32)
        for hd in range(MOBA_HEADS):
            g, half = hd // 2, hd % 2
            q_f = _lane_group_mask(q_ref[0, :, g * LANES:(g + 1) * LANES], half, HEAD_DIM)
            qs_ref[hd] = _transposed(q_f * (scale * LOG2E), eye_l)
            kmean = kmean_ref[:, g * LANES:(g + 1) * LANES].astype(BF16)
            gate = _nt(kmean, q_f.astype(BF16))
            for n in range(n_tiles):
                gn = gate[n:n + 1, :]
                ahead = (gate > gn) | ((gate == gn) & (blk < n))
                ahead = ahead & (blk < i)
                rank = jnp.sum(jnp.where(ahead, 1.0, 0.0), axis=0, keepdims=True)
                chosen = ((rank < topb) & (own > n)) | (own == n)
                selb_ref[hd, n] = jnp.broadcast_to(jnp.where(chosen, 0.0, NEG), (8, t))

    def logits(j, hd, ctx, kind):
        g = hd // 2
        kblk = _key_tile(k_ref, j, slice(g * LANES, (g + 1) * LANES))
        return _nn(kblk, qs_ref[hd]) + bias_ref[hd, kind] + selb_ref[hd, j][0:1, :]

    def write_output():
        rows = lax.broadcasted_iota(jnp.int32, (LANES, t), 0)
        eye = _eye(t)
        for g in range(pairs):
            o_lo = _attend_result(acc_ref, 2 * g, LANES)
            o_hi = _attend_result(acc_ref, 2 * g + 1, LANES)
            o = jnp.where(rows < HEAD_DIM, o_lo, o_hi)
            o_ref[0, :, g * LANES:(g + 1) * LANES] = _nt(eye, o.astype(BF16)).astype(o_ref.dtype)

    _attend(i + 1, range(1, n_tiles + 1), MOBA_HEADS, lambda j: None, logits,
            lambda j, hd: vt_ref[hd // 2, j], s_ref, m_ref, a_ref, acc_ref,
            unroll=True, before=prepare_queries_and_gates, after=write_output)


def _moba_attention(p3, bias_tiles):
    bsz, seq, _ = p3.shape
    t = ATT_TILE
    n_tiles = seq // t
    topb = min(MOBA_TOPK_MAX, n_tiles)
    kern = functools.partial(_moba_kernel, n_tiles=n_tiles, topb=float(topb))
    w = MOBA_HEADS * HEAD_DIM
    blocks = w // LANES
    first = (DIFF_HEADS + DSA_HEADS) // MOBA_HEADS
    return pl.pallas_call(
        kern,
        grid=(bsz, n_tiles),
        in_specs=[
            pl.BlockSpec((1, t, w), lambda b, i: (b, i, BLK_CQ // blocks)),
            pl.BlockSpec((1, seq, w), lambda b, i: (b, 0, BLK_CK // blocks)),
            pl.BlockSpec((1, seq, w), lambda b, i: (b, 0, BLK_CV // blocks)),
            pl.BlockSpec((MOBA_HEADS, 3, t, t), lambda b, i: (first, 0, 0, 0)),
        ],
        out_specs=pl.BlockSpec((1, t, w), lambda b, i: (b, i, 0)),
        out_shape=jax.ShapeDtypeStruct((bsz, seq, w), BF16),
        scratch_shapes=[
            pltpu.VMEM((blocks, n_tiles, LANES + SUM_ROWS, t), BF16),
            pltpu.VMEM((n_tiles, w), F32),
            pltpu.VMEM((MOBA_HEADS, n_tiles, 8, t), F32),
            pltpu.VMEM((MOBA_HEADS, LANES, t), BF16),
        ] + _attend_scratch(MOBA_HEADS, LANES),
        compiler_params=pltpu.CompilerParams(
            dimension_semantics=("parallel", "arbitrary"), vmem_limit_bytes=VMEM_LIMIT),
        name="moba_attn",
    )(p3, p3, p3, bias_tiles)


def _merge_kernel(x_ref, g_ref, gb_ref, oa_ref, ob_ref, oc_ref, wa_ref, wb_ref, wc_ref, wo_ref,
                  mod_ref, o_ref):
    d = D_MODEL
    gates = jax.nn.sigmoid(g_ref[...].astype(F32) + gb_ref[...])
    merged = (gates[:, 0:d] * _nn(oa_ref[...], wa_ref[...])
              + gates[:, d:2 * d] * _nn(ob_ref[...], wb_ref[...])
              + gates[:, 2 * d:3 * d] * _nn(oc_ref[...], wc_ref[...]))
    z = _nn(merged.astype(BF16), wo_ref[...])
    o_ref[...] = x_ref[...] + mod_ref[0, 2:3, :] * z


def _merge(x2, p2, gate_b, oa, ob, oc, wa, wb, wc, wo, layer, mod_l, seq):
    m, d = x2.shape
    tm = min(512, seq)
    full = lambda a: pl.BlockSpec((None,) + a.shape[1:], lambda i: (layer, 0, 0))
    return pl.pallas_call(
        _merge_kernel,
        grid=(m // tm,),
        in_specs=[
            pl.BlockSpec((tm, d), lambda i: (i, 0)),
            pl.BlockSpec((tm, 3 * d), lambda i: (i, BLK_G)),
            pl.BlockSpec((1, 3 * d), lambda i: (0, 0)),
            pl.BlockSpec((tm, oa.shape[1]), lambda i: (i, 0)),
            pl.BlockSpec((tm, ob.shape[1]), lambda i: (i, 0)),
            pl.BlockSpec((tm, oc.shape[1]), lambda i: (i, 0)),
            full(wa), full(wb), full(wc), full(wo),
            pl.BlockSpec((1, 6, d), lambda i: ((i * tm) // seq, 0, 0)),
        ],
        out_specs=pl.BlockSpec((tm, d), lambda i: (i, 0)),
        out_shape=jax.ShapeDtypeStruct((m, d), F32),
        compiler_params=pltpu.CompilerParams(
            dimension_semantics=("parallel",), vmem_limit_bytes=VMEM_LIMIT),
        name="merge",
    )(x2, p2, gate_b.reshape(1, 3 * d), oa, ob, oc, wa, wb, wc, wo, mod_l)


def _mlp_kernel(x_ref, mod_ref, g_ref, w1_ref, w2_ref, gf_ref, o_ref, *, final, tf):
    x = x_ref[...]
    u = _norm_modulate(x, g_ref[...], mod_ref[0, 3:4, :], mod_ref[0, 4:5, :]).astype(BF16)
    acc = None
    for f in range(w1_ref.shape[1] // tf):
        h = jnp.square(jnp.maximum(_nn(u, w1_ref[:, f * tf:(f + 1) * tf]), 0.0))
        part = _nn(h.astype(BF16), w2_ref[f * tf:(f + 1) * tf, :])
        acc = part if acc is None else acc + part
    y = x + mod_ref[0, 5:6, :] * acc
    if final:
        ms = jnp.mean(y * y, axis=-1, keepdims=True)
        y = y * lax.rsqrt(ms + EPS) * gf_ref[...]
    o_ref[...] = y


def _mlp(x2, mod_l, g, w1, w2, layer, g_final, seq, final):
    m, d = x2.shape
    dff = w1.shape[2]
    tm = min(512, seq)
    resident = lambda shape: pl.BlockSpec((None,) + shape, lambda i: (layer, 0, 0),
                                          pipeline_mode=pl.Buffered(1))
    return pl.pallas_call(
        functools.partial(_mlp_kernel, final=final, tf=1024),
        grid=(m // tm,),
        in_specs=[
            pl.BlockSpec((tm, d), lambda i: (i, 0)),
            pl.BlockSpec((1, 6, d), lambda i: ((i * tm) // seq, 0, 0)),
            pl.BlockSpec((1, d), lambda i: (0, 0)),
            resident((d, dff)),
            resident((dff, d)),
            pl.BlockSpec((1, d), lambda i: (0, 0)),
        ],
        out_specs=pl.BlockSpec((tm, d), lambda i: (i, 0)),
        out_shape=jax.ShapeDtypeStruct((m, d), F32),
        compiler_params=pltpu.CompilerParams(
            dimension_semantics=("parallel",), vmem_limit_bytes=VMEM_LIMIT),
        name="mlp",
    )(x2, mod_l, g.reshape(1, d), w1, w2, g_final.reshape(1, d))


def kernel(x, c, rel_bias, ada_w, ada_b, norm_mix, w_in, gate_b, diff_lambda, diff_subln,
           dsa_kv_norm, dsa_w_uv, w_br_a, w_br_b, w_br_c, w_o, norm_mlp, w_ff1, w_ff2,
           norm_final):
    bsz, seq, d = x.shape
    depth = w_in.shape[0]
    assert seq % ATT_TILE == 0 and 2 * ATT_TILE > MAX_DISTANCE

    w_in_p = _pack_columns(w_in.astype(BF16))
    n_out = DSA_HEADS * DSA_V_DIM
    wuv_t = jnp.transpose(dsa_w_uv, (0, 1, 3, 2))
    wuv_t_pad = jnp.zeros((depth, DSA_HEADS, n_out, DSA_LATENT), F32)
    for h in range(DSA_HEADS):
        wuv_t_pad = wuv_t_pad.at[:, h, h * DSA_V_DIM:(h + 1) * DSA_V_DIM, :].set(wuv_t[:, h])
    wuv_t_pad = wuv_t_pad.astype(BF16)
    wa, wb, wc, wo = (w.astype(BF16) for w in (w_br_a, w_br_b, w_br_c, w_o))
    w1, w2 = w_ff1.astype(BF16), w_ff2.astype(BF16)

    bias_tiles = _bias_tiles(rel_bias)
    mod = _ada(c, ada_w, ada_b).reshape(depth, bsz, 6, d)

    x2 = x.reshape(bsz * seq, d)
    for l in range(depth):
        lam_init = 0.8 - 0.6 * math.exp(-0.3 * l)
        p2 = _inproj(x2, mod[l], norm_mix[l], w_in_p, l, seq)
        p3 = p2.reshape(bsz, seq, PACKED_COLS)
        oa = _diff_attention(p3, bias_tiles, diff_lambda[l], diff_subln[l], lam_init)
        ob = _dsa_attention(p3, bias_tiles, dsa_kv_norm[l], wuv_t_pad[l])
        oc = _moba_attention(p3, bias_tiles)
        x2 = _merge(x2, p2, gate_b[l],
                    oa.reshape(bsz * seq, -1), ob.reshape(bsz * seq, -1), oc.reshape(bsz * seq, -1),
                    wa, wb, wc, wo, l, mod[l], seq)
        x2 = _mlp(x2, mod[l], norm_mlp[l], w1, w2, l, norm_final, seq, final=(l == depth - 1))
    return x2.reshape(bsz, seq, d)
```

```python
import functools
import math

import jax
import jax.numpy as jnp
from jax import lax
from jax.experimental import pallas as pl
from jax.experimental.pallas import tpu as pltpu

F32 = jnp.float32
BF16 = jnp.bfloat16

D_MODEL = 1024
HEAD_DIM = 64
DIFF_HEADS = 4
DIFF_V_DIM = 2 * HEAD_DIM
DSA_HEADS = 4
DSA_LATENT = 128
DSA_V_DIM = 64
IDX_HEADS = 8
IDX_DIM = 32
DSA_TOPK_MAX = 256
MOBA_HEADS = 4
MOBA_BLOCK = 256
MOBA_TOPK_MAX = 3
N_BUCKETS = 32
MAX_DISTANCE = 128
N_BIAS_HEADS = DIFF_HEADS + DSA_HEADS + MOBA_HEADS
D_FF = 4 * D_MODEL
EPS = 1e-6

LANES = 128
ATT_TILE = MOBA_BLOCK
NEG = -1e30
INT_MIN = -2 ** 31
VMEM_LIMIT = 52 * 1024 * 1024
DIFF_HEADS_PER_STEP = 4
SUM_ROWS = 16
LOG2E = math.log2(math.e)

_O_AQ, _O_AK, _O_AV, _O_BQ, _O_BKV, _O_BIQ, _O_BIK, _O_BIW, _O_CQ, _O_CK, _O_CV, _O_G = (
    0, 512, 1024, 1536, 2048, 2176, 2432, 2464, 2472, 2728, 2984, 3240)

BLK_G, BLK_AQ, BLK_AK, BLK_AV, BLK_BQ, BLK_BIQ, BLK_BKV, BLK_IK, BLK_CQ, BLK_CK, BLK_CV, BLK_IW = (
    0, 24, 28, 32, 36, 40, 42, 43, 44, 46, 48, 50)
N_BLKS = 51
PACKED_COLS = N_BLKS * LANES


def _pack_columns(w):
    seg = lambda off, n: w[..., off:off + n]
    ik = seg(_O_BIK, IDX_DIM)
    parts = [
        seg(_O_G, 3 * D_MODEL), seg(_O_AQ, 512), seg(_O_AK, 512), seg(_O_AV, 512),
        seg(_O_BQ, 512), seg(_O_BIQ, 256), seg(_O_BKV, 128),
        ik, ik, ik, ik,
        seg(_O_CQ, 256), seg(_O_CK, 256), seg(_O_CV, 256),
        seg(_O_BIW, IDX_HEADS),
        jnp.zeros(w.shape[:-1] + (LANES - IDX_HEADS,), w.dtype),
    ]
    out = jnp.concatenate(parts, axis=-1)
    assert out.shape[-1] == PACKED_COLS
    return out


def _nt(a, b):
    return lax.dot_general(a, b, (((1,), (1,)), ((), ())), preferred_element_type=F32)


def _nn(a, b):
    return jnp.dot(a, b, preferred_element_type=F32)


def _eye(n):
    r = lax.broadcasted_iota(jnp.int32, (n, n), 0)
    c = lax.broadcasted_iota(jnp.int32, (n, n), 1)
    return jnp.where(r == c, 1.0, 0.0).astype(BF16)


def _transposed(x_f32, eye):
    return _nt(eye, x_f32.astype(BF16)).astype(BF16)


def _lane_group_mask(x_bf16, group, width):
    lane = lax.broadcasted_iota(jnp.int32, x_bf16.shape, 1)
    keep = (lane >= group * width) & (lane < (group + 1) * width)
    return jnp.where(keep, x_bf16.astype(F32), 0.0)


def _ada_kernel(c_ref, w_ref, b_ref, o_ref):
    c = c_ref[...]
    cond = c * jax.nn.sigmoid(c)
    o_ref[0] = _nn(cond, w_ref[0]) + b_ref[0]


def _ada(c, ada_w, ada_b):
    depth, d, n = ada_w.shape
    bsz = c.shape[0]
    tn = 1536
    return pl.pallas_call(
        _ada_kernel,
        grid=(depth, n // tn),
        in_specs=[
            pl.BlockSpec((bsz, d), lambda l, j: (0, 0)),
            pl.BlockSpec((1, d, tn), lambda l, j: (l, 0, j)),
            pl.BlockSpec((1, 1, tn), lambda l, j: (l, 0, j)),
        ],
        out_specs=pl.BlockSpec((1, bsz, tn), lambda l, j: (l, 0, j)),
        out_shape=jax.ShapeDtypeStruct((depth, bsz, n), F32),
        compiler_params=pltpu.CompilerParams(
            dimension_semantics=("arbitrary", "arbitrary"), vmem_limit_bytes=VMEM_LIMIT),
        name="ada_mod",
    )(c, ada_w, ada_b.reshape(depth, 1, n))


def _t5_bucket(dist):
    max_exact = N_BUCKETS // 2
    n = jnp.maximum(dist, 0)
    nf = jnp.maximum(n, 1).astype(F32)
    large = max_exact + (jnp.log(nf / max_exact) / math.log(MAX_DISTANCE / max_exact)
                         * (N_BUCKETS - max_exact)).astype(jnp.int32)
    large = jnp.minimum(large, N_BUCKETS - 1)
    return jnp.where(n < max_exact, n, large)


def _bias_kernel(rb_ref, o_ref):
    t = ATT_TILE
    h = pl.program_id(0)
    kk = lax.broadcasted_iota(jnp.int32, (t, t), 0)
    qq = lax.broadcasted_iota(jnp.int32, (t, t), 1)
    for kind in range(3):
        dist = jnp.full((t, t), 2 * t, jnp.int32) if kind == 2 else kind * t + qq - kk
        bucket = _t5_bucket(dist)
        tile = jnp.zeros((t, t), F32)
        for b in range(N_BUCKETS):
            tile = jnp.where(bucket == b, rb_ref[b, h], tile)
        o_ref[0, kind] = jnp.where(dist >= 0, tile * LOG2E, NEG)


def _bias_tiles(rel_bias):
    t = ATT_TILE
    heads = rel_bias.shape[1]
    return pl.pallas_call(
        _bias_kernel,
        grid=(heads,),
        in_specs=[pl.BlockSpec(memory_space=pltpu.SMEM)],
        out_specs=pl.BlockSpec((1, 3, t, t), lambda h: (h, 0, 0, 0)),
        out_shape=jax.ShapeDtypeStruct((heads, 3, t, t), F32),
        compiler_params=pltpu.CompilerParams(dimension_semantics=("arbitrary",)),
        name="bias_tiles",
    )(rel_bias.astype(F32))


def _norm_modulate(x, g, shift, scale):
    ms = jnp.mean(x * x, axis=-1, keepdims=True)
    y = x * lax.rsqrt(ms + EPS) * g
    return y * (1.0 + scale) + shift


def _inproj_kernel(x_ref, mod_ref, g_ref, w_ref, o_ref, *, tn):
    u = _norm_modulate(x_ref[...], g_ref[...], mod_ref[0, 0:1, :], mod_ref[0, 1:2, :])
    u = u.astype(BF16)
    for j in range(w_ref.shape[1] // tn):
        cols = slice(j * tn, (j + 1) * tn)
        o_ref[:, cols] = _nn(u, w_ref[:, cols]).astype(o_ref.dtype)


def _inproj(x2, mod_l, g, w_packed, layer, seq):
    m, d = x2.shape
    n = w_packed.shape[2]
    tm = min(512, seq)
    return pl.pallas_call(
        functools.partial(_inproj_kernel, tn=n // 3),
        grid=(m // tm,),
        in_specs=[
            pl.BlockSpec((tm, d), lambda i: (i, 0)),
            pl.BlockSpec((1, 6, d), lambda i: ((i * tm) // seq, 0, 0)),
            pl.BlockSpec((1, d), lambda i: (0, 0)),
            pl.BlockSpec((None, d, n), lambda i: (layer, 0, 0), pipeline_mode=pl.Buffered(1)),
        ],
        out_specs=pl.BlockSpec((tm, n), lambda i: (i, 0)),
        out_shape=jax.ShapeDtypeStruct((m, n), BF16),
        compiler_params=pltpu.CompilerParams(
            dimension_semantics=("parallel",), vmem_limit_bytes=VMEM_LIMIT),
        name="in_proj",
    )(x2, mod_l, g.reshape(1, d), w_packed)


def _attend(n_live, lives, chains, tile_ctx, logits, values, s_ref, m_ref, a_ref, acc_ref,
            *, unroll, before=None, after=None):
    def init():
        m_ref[...] = jnp.full(m_ref.shape, NEG, F32)
        acc_ref[...] = jnp.zeros(acc_ref.shape, F32)

    def step(j, slot, kind, start=True, finish=True):
        ctx = tile_ctx(j) if start else None
        for c in range(chains):
            if start:
                s = logits(j, c, ctx, kind)
                m_old = m_ref[1 - slot, c]
                m_new = jnp.maximum(m_old, jnp.max(s, axis=0, keepdims=True))
                s_ref[slot, c] = s
                a_ref[slot, c] = jnp.exp2(m_old - m_new)
                m_ref[slot, c] = m_new
            if finish:
                p = jnp.exp2(s_ref[1 - slot, c] - m_ref[1 - slot, c])
                acc_ref[c] = (a_ref[1 - slot, c] * acc_ref[c]
                              + _nn(values(j - 1, c), p.astype(BF16)))

    if unroll:
        def variant(live):
            if before is not None:
                before()
            init()
            for j in range(live + 1):
                step(j, j % 2, min(live - 1 - j, 2), start=j < live, finish=j > 0)
            if after is not None:
                after()

        for live in lives:
            pl.when(n_live == live)(functools.partial(variant, live))
        return

    if before is not None:
        before()
    init()
    step(0, 0, jnp.minimum(n_live - 1, 2), finish=False)

    def body(j, carry):
        for slot in range(2):
            pl.when(j % 2 == slot)(
                functools.partial(step, j, slot, jnp.minimum(n_live - 1 - j, 2)))
        return carry

    lax.fori_loop(1, n_live, body, 0)
    for slot in range(2):
        pl.when(n_live % 2 == slot)(functools.partial(step, n_live, slot, None, start=False))
    if after is not None:
        after()


def _attend_scratch(chains, dv):
    t = ATT_TILE
    return [
        pltpu.VMEM((2, chains, t, t), F32),
        pltpu.VMEM((2, chains, 1, t), F32),
        pltpu.VMEM((2, chains, 1, t), F32),
        pltpu.VMEM((chains, dv + SUM_ROWS, t), F32),
    ]


def _attend_result(acc_ref, c, dv):
    return acc_ref[c, 0:dv, :] / acc_ref[c, dv:dv + 1, :]


def _key_tile(ref, j, cols=None):
    t = ATT_TILE
    rows = slice(j * t, (j + 1) * t) if isinstance(j, int) else pl.ds(pl.multiple_of(j * t, t), t)
    return ref[0, rows, :] if cols is None else ref[0, rows, cols]


def _transpose_tiles(src_ref, dst_ref, n_tiles, groups):
    t = ATT_TILE
    eye = _eye(LANES)
    for g in range(groups):
        for j in range(n_tiles):
            blk = src_ref[0, j * t:(j + 1) * t, g * LANES:(g + 1) * LANES]
            dst_ref[g, j, 0:LANES, :] = _nt(eye, blk).astype(BF16)
            dst_ref[g, j, LANES:LANES + SUM_ROWS, :] = jnp.ones((SUM_ROWS, t), BF16)


def _diff_kernel(q_ref, k_ref, v_ref, bias_ref, dl_ref, g_ref, o_ref,
                 vt_ref, qs_ref, s_ref, m_ref, a_ref, acc_ref, *, lam_init, n_tiles, hp):
    t = ATT_TILE
    i = pl.program_id(2)

    @pl.when(i == 0)
    def _():
        _transpose_tiles(v_ref, vt_ref, n_tiles, hp)

    def prepare_queries():
        scale = HEAD_DIM ** -0.5 * LOG2E
        eye_l = _eye(LANES)
        for h in range(hp):
            q = q_ref[0, :, h * LANES:(h + 1) * LANES]
            for half in range(2):
                qs_ref[2 * h + half] = _transposed(
                    _lane_group_mask(q, half, HEAD_DIM) * scale, eye_l)

    def logits(j, c, ctx, kind):
        h = c // 2
        kblk = _key_tile(k_ref, j, slice(h * LANES, (h + 1) * LANES))
        return _nn(kblk, qs_ref[c]) + bias_ref[h, kind]

    def write_output():
        dl = dl_ref[...]
        lam = (jnp.exp(jnp.sum(dl[0:1] * dl[1:2], keepdims=True))
               - jnp.exp(jnp.sum(dl[2:3] * dl[3:4], keepdims=True)) + lam_init)
        eye = _eye(t)
        for h in range(hp):
            o = (_attend_result(acc_ref, 2 * h, DIFF_V_DIM)
                 - lam * _attend_result(acc_ref, 2 * h + 1, DIFF_V_DIM))
            ms = jnp.mean(o * o, axis=0, keepdims=True)
            o = o * lax.rsqrt(ms + EPS) * g_ref[...] * (1.0 - lam_init)
            o_ref[0, :, h * LANES:(h + 1) * LANES] = _nt(eye, o.astype(BF16)).astype(o_ref.dtype)

    _attend(i + 1, range(1, n_tiles + 1), 2 * hp, lambda j: None, logits,
            lambda j, c: vt_ref[c // 2, j], s_ref, m_ref, a_ref, acc_ref,
            unroll=True, before=prepare_queries, after=write_output)


def _diff_attention(p3, bias_tiles, diff_lambda_l, subln_g, lam_init):
    bsz, seq, _ = p3.shape
    t = ATT_TILE
    n_tiles = seq // t
    hp = DIFF_HEADS_PER_STEP
    w = hp * LANES
    kern = functools.partial(_diff_kernel, lam_init=lam_init, n_tiles=n_tiles, hp=hp)
    return pl.pallas_call(
        kern,
        grid=(bsz, DIFF_HEADS // hp, n_tiles),
        in_specs=[
            pl.BlockSpec((1, t, w), lambda b, h, i: (b, i, BLK_AQ // hp + h)),
            pl.BlockSpec((1, seq, w), lambda b, h, i: (b, 0, BLK_AK // hp + h)),
            pl.BlockSpec((1, seq, w), lambda b, h, i: (b, 0, BLK_AV // hp + h)),
            pl.BlockSpec((hp, 3, t, t), lambda b, h, i: (h, 0, 0, 0)),
            pl.BlockSpec((4, HEAD_DIM), lambda b, h, i: (0, 0)),
            pl.BlockSpec((DIFF_V_DIM, 1), lambda b, h, i: (0, 0)),
        ],
        out_specs=pl.BlockSpec((1, t, w), lambda b, h, i: (b, i, h)),
        out_shape=jax.ShapeDtypeStruct((bsz, seq, DIFF_HEADS * DIFF_V_DIM), BF16),
        scratch_shapes=[
            pltpu.VMEM((hp, n_tiles, LANES + SUM_ROWS, t), BF16),
            pltpu.VMEM((2 * hp, LANES, t), BF16),
        ] + _attend_scratch(2 * hp, DIFF_V_DIM),
        compiler_params=pltpu.CompilerParams(
            dimension_semantics=("parallel", "parallel", "arbitrary"),
            vmem_limit_bytes=VMEM_LIMIT),
        name="diff_attn",
    )(p3, p3, p3, bias_tiles, diff_lambda_l, subln_g.reshape(DIFF_V_DIM, 1))


def _dsa_kernel(q_ref, kv_ref, iq_ref, ik_ref, iw_ref, bias_ref, g_ref, wuv_ref, o_ref,
                kvn_ref, kvt_ref, keys_ref, byte_ref, cand_ref, thr_ref, need_ref, tied_ref,
                iqh_ref, qt_ref, iwt_ref,
                s_ref, m_ref, a_ref, acc_ref, *, n_tiles, first_tile, last_tile, topk):
    t = ATT_TILE
    i = pl.program_id(1) + first_tile
    n_live = i + 1
    lives = range(first_tile + 1, last_tile + 2)

    @pl.when(pl.program_id(1) == 0)
    def _():
        kv = kv_ref[0].astype(F32)
        ms = jnp.mean(kv * kv, axis=-1, keepdims=True)
        kvn_ref[0] = (kv * lax.rsqrt(ms + EPS) * g_ref[...]).astype(BF16)
        _transpose_tiles(kvn_ref, kvt_ref, n_tiles, 1)

    def prepare_queries():
        groups = LANES // IDX_DIM
        eye_l = _eye(LANES)
        for h in range(IDX_HEADS):
            blk = iq_ref[0, :, (h // groups) * LANES:(h // groups + 1) * LANES]
            iqh_ref[h] = _transposed(_lane_group_mask(blk, h % groups, IDX_DIM), eye_l)
        for h in range(DSA_HEADS):
            qt_ref[h] = _nt(eye_l, q_ref[0, :, h * LANES:(h + 1) * LANES]).astype(BF16)
        sel_rows = lax.broadcasted_iota(jnp.int32, (IDX_HEADS, LANES), 0)
        sel_cols = lax.broadcasted_iota(jnp.int32, (IDX_HEADS, LANES), 1)
        pick = jnp.where(sel_rows == sel_cols, 1.0, 0.0).astype(BF16)
        iwt_ref[...] = _nt(pick, iw_ref[0])

    idx_scale = (IDX_HEADS ** -0.5) * (IDX_DIM ** -0.5)
    row = lax.broadcasted_iota(jnp.int32, (t, t), 0)
    col = lax.broadcasted_iota(jnp.int32, (t, t), 1)

    def score_tile(j, diagonal):
        ik = _key_tile(ik_ref, j)
        sc = jnp.zeros((t, t), F32)
        for h in range(IDX_HEADS):
            sc = sc + jnp.maximum(_nn(ik, iqh_ref[h]), 0.0) * iwt_ref[h:h + 1, :]
        sc = sc * idx_scale
        bits = pltpu.bitcast(sc, jnp.int32)
        key = bits ^ ((bits >> 31) & 0x7FFFFFFF)
        top = (key >> 24) + 128
        if diagonal:
            valid = col >= row
            key = jnp.where(valid, key, INT_MIN)
            top = jnp.where(valid, top, -1)
        keys_ref[j] = key
        cand_ref[j] = top.astype(F32).astype(BF16)
        for b in range(3):
            byte_ref[b, j] = ((key >> (16 - 8 * b)) & 0xFF).astype(F32).astype(BF16)

    one_b, zero_b = jnp.ones((), BF16), jnp.zeros((), BF16)
    packed_rows = 16

    def radix_select(live):
        prepare_queries()
        for j in range(live):
            score_tile(j, diagonal=(j == live - 1))

        def count(hit_fn):
            part = None
            for j in range(live):
                hit = jnp.where(hit_fn(cand_ref[j]), one_b, zero_b)
                for r in range(t // packed_rows):
                    rows = hit[packed_rows * r:packed_rows * (r + 1)]
                    part = rows if part is None else part + rows
            return jnp.sum(part.astype(F32), axis=0, keepdims=True)

        if live * t <= topk:
            thr_ref[...] = jnp.full((1, t), INT_MIN, jnp.int32)
            need_ref[...] = jnp.zeros((1, t), F32)
            tied_ref[...] = jnp.zeros((1, t), F32)
            return

        need = jnp.full((1, t), topk, F32)
        thr = jnp.zeros((1, t), jnp.int32)
        for b in range(4):
            def bit_body(it, carry, need=need):
                val, n_above = carry
                cand = val | lax.shift_left(jnp.int32(1), 7 - it)
                cand_b = cand.astype(F32).astype(BF16)
                cnt = count(lambda x: x >= cand_b)
                ok = cnt >= need
                return jnp.where(ok, cand, val), jnp.where(ok, n_above, cnt)

            val, n_above = lax.fori_loop(
                0, 8, bit_body, (jnp.zeros((1, t), jnp.int32), jnp.zeros((1, t), F32)))
            val_b = val.astype(F32).astype(BF16)
            need = need - n_above
            if b < 3:
                for j in range(live):
                    cand_ref[j] = jnp.where(cand_ref[j] == val_b, byte_ref[b, j], -one_b)
            else:
                tied_ref[...] = count(lambda x: x == val_b)
            piece = val - (128 if b == 0 else 0)
            thr = thr | lax.shift_left(piece, 24 - 8 * b)
        thr_ref[...] = thr
        need_ref[...] = need

    assert (t // packed_rows) * n_tiles <= 256
    for live in lives:
        pl.when(n_live == live)(functools.partial(radix_select, live))

    thr = thr_ref[...]
    need = need_ref[...]
    has_rank = thr > INT_MIN
    excess = jnp.where(has_rank & (tied_ref[...] > need), 1.0, 0.0)
    thr_ref[...] = jnp.where(has_rank, thr, jnp.int32(INT_MIN + 1))

    @pl.when(jnp.max(excess) > 0.0)
    def _():
        earlier = jnp.where(col < row, 1.0, 0.0).astype(BF16)

        def tie_body(j, seen):
            key = keys_ref[j]
            tie = jnp.where(key == thr, 1.0, 0.0)
            before = seen + _nn(earlier, tie.astype(BF16))
            late = (key == thr) & has_rank & (before >= need)
            keys_ref[j] = jnp.where(late, thr - 1, key)
            return seen + jnp.sum(tie, axis=0, keepdims=True)

        lax.fori_loop(0, n_live, tie_body, jnp.zeros((1, t), F32))

    scale = DSA_LATENT ** -0.5 * LOG2E

    def tile_ctx(j):
        return keys_ref[j] >= thr_ref[...], _key_tile(kvn_ref, j)

    def logits(j, h, ctx, kind):
        sel, kvb = ctx
        return jnp.where(sel, _nn(kvb, qt_ref[h]) * scale + bias_ref[h, kind], NEG)

    def write_output():
        y_t = jnp.zeros((DSA_HEADS * DSA_V_DIM, t), F32)
        for h in range(DSA_HEADS):
            o_h = _attend_result(acc_ref, h, DSA_LATENT).astype(BF16)
            y_t = y_t + _nn(wuv_ref[h], o_h)
        o_ref[0] = _nt(_eye(t), y_t.astype(BF16)).astype(o_ref.dtype)

    _attend(n_live, lives, DSA_HEADS, tile_ctx, logits, lambda j, h: kvt_ref[0, j],
            s_ref, m_ref, a_ref, acc_ref, unroll=True, after=write_output)


def _dsa_attention(p3, bias_tiles, kv_norm_g, wuv_t_pad):
    n_tiles = p3.shape[1] // ATT_TILE
    cuts = [0, n_tiles // 2, (3 * n_tiles) // 4, n_tiles]
    parts = [_dsa_attention_part(p3, bias_tiles, kv_norm_g, wuv_t_pad, lo, hi - 1)
             for lo, hi in zip(cuts[:-1], cuts[1:]) if hi > lo]
    return jnp.concatenate(parts, axis=1)


def _dsa_attention_part(p3, bias_tiles, kv_norm_g, wuv_t_pad, first_tile, last_tile):
    bsz, seq, _ = p3.shape
    t = ATT_TILE
    n_tiles = seq // t
    n_q = last_tile - first_tile + 1
    topk = min(DSA_TOPK_MAX, seq // 4)
    kern = functools.partial(_dsa_kernel, n_tiles=n_tiles, first_tile=first_tile,
                             last_tile=last_tile, topk=float(topk))
    n_out = DSA_HEADS * DSA_V_DIM
    return pl.pallas_call(
        kern,
        grid=(bsz, n_q),
        in_specs=[
            pl.BlockSpec((1, t, 4 * LANES), lambda b, i: (b, i + first_tile, BLK_BQ // 4)),
            pl.BlockSpec((1, seq, LANES), lambda b, i: (b, 0, BLK_BKV)),
            pl.BlockSpec((1, t, 2 * LANES), lambda b, i: (b, i + first_tile, BLK_BIQ // 2)),
            pl.BlockSpec((1, seq, LANES), lambda b, i: (b, 0, BLK_IK)),
            pl.BlockSpec((1, t, LANES), lambda b, i: (b, i + first_tile, BLK_IW)),
            pl.BlockSpec((DSA_HEADS, 3, t, t), lambda b, i: (DIFF_HEADS // DSA_HEADS, 0, 0, 0)),
            pl.BlockSpec((1, DSA_LATENT), lambda b, i: (0, 0)),
            pl.BlockSpec((DSA_HEADS, n_out, DSA_LATENT), lambda b, i: (0, 0, 0)),
        ],
        out_specs=pl.BlockSpec((1, t, n_out), lambda b, i: (b, i, 0)),
        out_shape=jax.ShapeDtypeStruct((bsz, n_q * t, n_out), BF16),
        scratch_shapes=[
            pltpu.VMEM((1, seq, DSA_LATENT), BF16),
            pltpu.VMEM((1, n_tiles, DSA_LATENT + SUM_ROWS, t), BF16),
            pltpu.VMEM((n_tiles, t, t), jnp.int32),
            pltpu.VMEM((3, n_tiles, t, t), BF16),
            pltpu.VMEM((n_tiles, t, t), BF16),
            pltpu.VMEM((1, t), jnp.int32),
            pltpu.VMEM((1, t), F32),
            pltpu.VMEM((1, t), F32),
            pltpu.VMEM((IDX_HEADS, LANES, t), BF16),
            pltpu.VMEM((DSA_HEADS, LANES, t), BF16),
            pltpu.VMEM((IDX_HEADS, t), F32),
        ] + _attend_scratch(DSA_HEADS, DSA_LATENT),
        compiler_params=pltpu.CompilerParams(
            dimension_semantics=("parallel", "arbitrary"), vmem_limit_bytes=VMEM_LIMIT),
        name="dsa_attn",
    )(p3, p3, p3, p3, p3, bias_tiles, kv_norm_g.reshape(1, DSA_LATENT), wuv_t_pad)


def _moba_kernel(q_ref, k_ref, v_ref, bias_ref, o_ref,
                 vt_ref, kmean_ref, selb_ref, qs_ref,
                 s_ref, m_ref, a_ref, acc_ref, *, n_tiles, topb):
    t = ATT_TILE
    i = pl.program_id(1)
    pairs = MOBA_HEADS // 2

    @pl.when(i == 0)
    def _():
        _transpose_tiles(v_ref, vt_ref, n_tiles, pairs)
        for n in range(n_tiles):
            kb = k_ref[0, n * t:(n + 1) * t, :].astype(F32)
            kmean_ref[n:n + 1, :] = jnp.mean(kb, axis=0, keepdims=True)

    def prepare_queries_and_gates():
        scale = HEAD_DIM ** -0.5
        eye_l = _eye(LANES)
        blk = lax.broadcasted_iota(jnp.int32, (n_tiles, t), 0)
        own = jnp.full((1, t), i, jnp.int32)
        for hd in range(MOBA_HEADS):
            g, half = hd // 2, hd % 2
            q_f = _lane_group_mask(q_ref[0, :, g * LANES:(g + 1) * LANES], half, HEAD_DIM)
            qs_ref[hd] = _transposed(q_f * (scale * LOG2E), eye_l)
            kmean = kmean_ref[:, g * LANES:(g + 1) * LANES].astype(BF16)
            gate = _nt(kmean, q_f.astype(BF16))
            for n in range(n_tiles):
                gn = gate[n:n + 1, :]
                ahead = (gate > gn) | ((gate == gn) & (blk < n))
                ahead = ahead & (blk < i)
                rank = jnp.sum(jnp.where(ahead, 1.0, 0.0), axis=0, keepdims=True)
                chosen = ((rank < topb) & (own > n)) | (own == n)
                selb_ref[hd, n] = jnp.broadcast_to(jnp.where(chosen, 0.0, NEG), (8, t))

    def logits(j, hd, ctx, kind):
        g = hd // 2
        kblk = _key_tile(k_ref, j, slice(g * LANES, (g + 1) * LANES))
        return _nn(kblk, qs_ref[hd]) + bias_ref[hd, kind] + selb_ref[hd, j][0:1, :]

    def write_output():
        rows = lax.broadcasted_iota(jnp.int32, (LANES, t), 0)
        eye = _eye(t)
        for g in range(pairs):
            o_lo = _attend_result(acc_ref, 2 * g, LANES)
            o_hi = _attend_result(acc_ref, 2 * g + 1, LANES)
            o = jnp.where(rows < HEAD_DIM, o_lo, o_hi)
            o_ref[0, :, g * LANES:(g + 1) * LANES] = _nt(eye, o.astype(BF16)).astype(o_ref.dtype)

    _attend(i + 1, range(1, n_tiles + 1), MOBA_HEADS, lambda j: None, logits,
            lambda j, hd: vt_ref[hd // 2, j], s_ref, m_ref, a_ref, acc_ref,
            unroll=True, before=prepare_queries_and_gates, after=write_output)


def _moba_attention(p3, bias_tiles):
    bsz, seq, _ = p3.shape
    t = ATT_TILE
    n_tiles = seq // t
    topb = min(MOBA_TOPK_MAX, n_tiles)
    kern = functools.partial(_moba_kernel, n_tiles=n_tiles, topb=float(topb))
    w = MOBA_HEADS * HEAD_DIM
    blocks = w // LANES
    first = (DIFF_HEADS + DSA_HEADS) // MOBA_HEADS
    return pl.pallas_call(
        kern,
        grid=(bsz, n_tiles),
        in_specs=[
            pl.BlockSpec((1, t, w), lambda b, i: (b, i, BLK_CQ // blocks)),
            pl.BlockSpec((1, seq, w), lambda b, i: (b, 0, BLK_CK // blocks)),
            pl.BlockSpec((1, seq, w), lambda b, i: (b, 0, BLK_CV // blocks)),
            pl.BlockSpec((MOBA_HEADS, 3, t, t), lambda b, i: (first, 0, 0, 0)),
        ],
        out_specs=pl.BlockSpec((1, t, w), lambda b, i: (b, i, 0)),
        out_shape=jax.ShapeDtypeStruct((bsz, seq, w), BF16),
        scratch_shapes=[
            pltpu.VMEM((blocks, n_tiles, LANES + SUM_ROWS, t), BF16),
            pltpu.VMEM((n_tiles, w), F32),
            pltpu.VMEM((MOBA_HEADS, n_tiles, 8, t), F32),
            pltpu.VMEM((MOBA_HEADS, LANES, t), BF16),
        ] + _attend_scratch(MOBA_HEADS, LANES),
        compiler_params=pltpu.CompilerParams(
            dimension_semantics=("parallel", "arbitrary"), vmem_limit_bytes=VMEM_LIMIT),
        name="moba_attn",
    )(p3, p3, p3, bias_tiles)


def _merge_kernel(x_ref, g_ref, gb_ref, oa_ref, ob_ref, oc_ref, wa_ref, wb_ref, wc_ref, wo_ref,
                  mod_ref, o_ref):
    d = D_MODEL
    gates = jax.nn.sigmoid(g_ref[...].astype(F32) + gb_ref[...])
    merged = (gates[:, 0:d] * _nn(oa_ref[...], wa_ref[...])
              + gates[:, d:2 * d] * _nn(ob_ref[...], wb_ref[...])
              + gates[:, 2 * d:3 * d] * _nn(oc_ref[...], wc_ref[...]))
    z = _nn(merged.astype(BF16), wo_ref[...])
    o_ref[...] = x_ref[...] + mod_ref[0, 2:3, :] * z


def _merge(x2, p2, gate_b, oa, ob, oc, wa, wb, wc, wo, layer, mod_l, seq):
    m, d = x2.shape
    tm = min(512, seq)
    full = lambda a: pl.BlockSpec((None,) + a.shape[1:], lambda i: (layer, 0, 0))
    return pl.pallas_call(
        _merge_kernel,
        grid=(m // tm,),
        in_specs=[
            pl.BlockSpec((tm, d), lambda i: (i, 0)),
            pl.BlockSpec((tm, 3 * d), lambda i: (i, BLK_G)),
            pl.BlockSpec((1, 3 * d), lambda i: (0, 0)),
            pl.BlockSpec((tm, oa.shape[1]), lambda i: (i, 0)),
            pl.BlockSpec((tm, ob.shape[1]), lambda i: (i, 0)),
            pl.BlockSpec((tm, oc.shape[1]), lambda i: (i, 0)),
            full(wa), full(wb), full(wc), full(wo),
            pl.BlockSpec((1, 6, d), lambda i: ((i * tm) // seq, 0, 0)),
        ],
        out_specs=pl.BlockSpec((tm, d), lambda i: (i, 0)),
        out_shape=jax.ShapeDtypeStruct((m, d), F32),
        compiler_params=pltpu.CompilerParams(
            dimension_semantics=("parallel",), vmem_limit_bytes=VMEM_LIMIT),
        name="merge",
    )(x2, p2, gate_b.reshape(1, 3 * d), oa, ob, oc, wa, wb, wc, wo, mod_l)


def _mlp_kernel(x_ref, mod_ref, g_ref, w1_ref, w2_ref, gf_ref, o_ref, *, final, tf):
    x = x_ref[...]
    u = _norm_modulate(x, g_ref[...], mod_ref[0, 3:4, :], mod_ref[0, 4:5, :]).astype(BF16)
    acc = None
    for f in range(w1_ref.shape[1] // tf):
        h = jnp.square(jnp.maximum(_nn(u, w1_ref[:, f * tf:(f + 1) * tf]), 0.0))
        part = _nn(h.astype(BF16), w2_ref[f * tf:(f + 1) * tf, :])
        acc = part if acc is None else acc + part
    y = x + mod_ref[0, 5:6, :] * acc
    if final:
        ms = jnp.mean(y * y, axis=-1, keepdims=True)
        y = y * lax.rsqrt(ms + EPS) * gf_ref[...]
    o_ref[...] = y


def _mlp(x2, mod_l, g, w1, w2, layer, g_final, seq, final):
    m, d = x2.shape
    dff = w1.shape[2]
    tm = min(512, seq)
    resident = lambda shape: pl.BlockSpec((None,) + shape, lambda i: (layer, 0, 0),
                                          pipeline_mode=pl.Buffered(1))
    return pl.pallas_call(
        functools.partial(_mlp_kernel, final=final, tf=1024),
        grid=(m // tm,),
        in_specs=[
            pl.BlockSpec((tm, d), lambda i: (i, 0)),
            pl.BlockSpec((1, 6, d), lambda i: ((i * tm) // seq, 0, 0)),
            pl.BlockSpec((1, d), lambda i: (0, 0)),
            resident((d, dff)),
            resident((dff, d)),
            pl.BlockSpec((1, d), lambda i: (0, 0)),
        ],
        out_specs=pl.BlockSpec((tm, d), lambda i: (i, 0)),
        out_shape=jax.ShapeDtypeStruct((m, d), F32),
        compiler_params=pltpu.CompilerParams(
            dimension_semantics=("parallel",), vmem_limit_bytes=VMEM_LIMIT),
        name="mlp",
    )(x2, mod_l, g.reshape(1, d), w1, w2, g_final.reshape(1, d))


def kernel(x, c, rel_bias, ada_w, ada_b, norm_mix, w_in, gate_b, diff_lambda, diff_subln,
           dsa_kv_norm, dsa_w_uv, w_br_a, w_br_b, w_br_c, w_o, norm_mlp, w_ff1, w_ff2,
           norm_final):
    bsz, seq, d = x.shape
    depth = w_in.shape[0]
    assert seq % ATT_TILE == 0 and 2 * ATT_TILE > MAX_DISTANCE

    w_in_p = _pack_columns(w_in.astype(BF16))
    n_out = DSA_HEADS * DSA_V_DIM
    wuv_t = jnp.transpose(dsa_w_uv, (0, 1, 3, 2))
    wuv_t_pad = jnp.zeros((depth, DSA_HEADS, n_out, DSA_LATENT), F32)
    for h in range(DSA_HEADS):
        wuv_t_pad = wuv_t_pad.at[:, h, h * DSA_V_DIM:(h + 1) * DSA_V_DIM, :].set(wuv_t[:, h])
    wuv_t_pad = wuv_t_pad.astype(BF16)
    wa, wb, wc, wo = (w.astype(BF16) for w in (w_br_a, w_br_b, w_br_c, w_o))
    w1, w2 = w_ff1.astype(BF16), w_ff2.astype(BF16)

    bias_tiles = _bias_tiles(rel_bias)
    mod = _ada(c, ada_w, ada_b).reshape(depth, bsz, 6, d)

    x2 = x.reshape(bsz * seq, d)
    for l in range(depth):
        lam_init = 0.8 - 0.6 * math.exp(-0.3 * l)
        p2 = _inproj(x2, mod[l], norm_mix[l], w_in_p, l, seq)
        p3 = p2.reshape(bsz, seq, PACKED_COLS)
        oa = _diff_attention(p3, bias_tiles, diff_lambda[l], diff_subln[l], lam_init)
        ob = _dsa_attention(p3, bias_tiles, dsa_kv_norm[l], wuv_t_pad[l])
        oc = _moba_attention(p3, bias_tiles)
        x2 = _merge(x2, p2, gate_b[l],
                    oa.reshape(bsz * seq, -1), ob.reshape(bsz * seq, -1), oc.reshape(bsz * seq, -1),
                    wa, wb, wc, wo, l, mod[l], seq)
        x2 = _mlp(x2, mod[l], norm_mlp[l], w1, w2, l, norm_final, seq, final=(l == depth - 1))
    return x2.reshape(bsz, seq, d)
```

```python
import functools
import math

import jax
import jax.numpy as jnp
from jax import lax
from jax.experimental import pallas as pl
from jax.experimental.pallas import tpu as pltpu

F32 = jnp.float32
BF16 = jnp.bfloat16

D_MODEL = 1024
HEAD_DIM = 64
DIFF_HEADS = 4
DIFF_V_DIM = 2 * HEAD_DIM
DSA_HEADS = 4
DSA_LATENT = 128
DSA_V_DIM = 64
IDX_HEADS = 8
IDX_DIM = 32
DSA_TOPK_MAX = 256
MOBA_HEADS = 4
MOBA_BLOCK = 256
MOBA_TOPK_MAX = 3
N_BUCKETS = 32
MAX_DISTANCE = 128
N_BIAS_HEADS = DIFF_HEADS + DSA_HEADS + MOBA_HEADS
D_FF = 4 * D_MODEL
EPS = 1e-6

LANES = 128
ATT_TILE = MOBA_BLOCK
NEG = -1e30
INT_MIN = -2 ** 31
VMEM_LIMIT = 52 * 1024 * 1024
DIFF_HEADS_PER_STEP = 4
SUM_ROWS = 16
LOG2E = math.log2(math.e)

_O_AQ, _O_AK, _O_AV, _O_BQ, _O_BKV, _O_BIQ, _O_BIK, _O_BIW, _O_CQ, _O_CK, _O_CV, _O_G = (
    0, 512, 1024, 1536, 2048, 2176, 2432, 2464, 2472, 2728, 2984, 3240)

BLK_G, BLK_AQ, BLK_AK, BLK_AV, BLK_BQ, BLK_BIQ, BLK_BKV, BLK_IK, BLK_CQ, BLK_CK, BLK_CV, BLK_IW = (
    0, 24, 28, 32, 36, 40, 42, 43, 44, 46, 48, 50)
N_BLKS = 51
PACKED_COLS = N_BLKS * LANES


def _pack_columns(w):
    seg = lambda off, n: w[..., off:off + n]
    ik = seg(_O_BIK, IDX_DIM)
    parts = [
        seg(_O_G, 3 * D_MODEL), seg(_O_AQ, 512), seg(_O_AK, 512), seg(_O_AV, 512),
        seg(_O_BQ, 512), seg(_O_BIQ, 256), seg(_O_BKV, 128),
        ik, ik, ik, ik,
        seg(_O_CQ, 256), seg(_O_CK, 256), seg(_O_CV, 256),
        seg(_O_BIW, IDX_HEADS),
        jnp.zeros(w.shape[:-1] + (LANES - IDX_HEADS,), w.dtype),
    ]
    out = jnp.concatenate(parts, axis=-1)
    assert out.shape[-1] == PACKED_COLS
    return out


def _nt(a, b):
    return lax.dot_general(a, b, (((1,), (1,)), ((), ())), preferred_element_type=F32)


def _nn(a, b):
    return jnp.dot(a, b, preferred_element_type=F32)


def _eye(n):
    r = lax.broadcasted_iota(jnp.int32, (n, n), 0)
    c = lax.broadcasted_iota(jnp.int32, (n, n), 1)
    return jnp.where(r == c, 1.0, 0.0).astype(BF16)


def _transposed(x_f32, eye):
    return _nt(eye, x_f32.astype(BF16)).astype(BF16)


def _lane_group_mask(x_bf16, group, width):
    lane = lax.broadcasted_iota(jnp.int32, x_bf16.shape, 1)
    keep = (lane >= group * width) & (lane < (group + 1) * width)
    return jnp.where(keep, x_bf16.astype(F32), 0.0)


def _ada_kernel(c_ref, w_ref, b_ref, o_ref):
    c = c_ref[...]
    cond = c * jax.nn.sigmoid(c)
    o_ref[0] = _nn(cond, w_ref[0]) + b_ref[0]


def _ada(c, ada_w, ada_b):
    depth, d, n = ada_w.shape
    bsz = c.shape[0]
    tn = 1536
    return pl.pallas_call(
        _ada_kernel,
        grid=(depth, n // tn),
        in_specs=[
            pl.BlockSpec((bsz, d), lambda l, j: (0, 0)),
            pl.BlockSpec((1, d, tn), lambda l, j: (l, 0, j)),
            pl.BlockSpec((1, 1, tn), lambda l, j: (l, 0, j)),
        ],
        out_specs=pl.BlockSpec((1, bsz, tn), lambda l, j: (l, 0, j)),
        out_shape=jax.ShapeDtypeStruct((depth, bsz, n), F32),
        compiler_params=pltpu.CompilerParams(
            dimension_semantics=("arbitrary", "arbitrary"), vmem_limit_bytes=VMEM_LIMIT),
        name="ada_mod",
    )(c, ada_w, ada_b.reshape(depth, 1, n))


def _t5_bucket(dist):
    max_exact = N_BUCKETS // 2
    n = jnp.maximum(dist, 0)
    nf = jnp.maximum(n, 1).astype(F32)
    large = max_exact + (jnp.log(nf / max_exact) / math.log(MAX_DISTANCE / max_exact)
                         * (N_BUCKETS - max_exact)).astype(jnp.int32)
    large = jnp.minimum(large, N_BUCKETS - 1)
    return jnp.where(n < max_exact, n, large)


def _bias_kernel(rb_ref, o_ref):
    t = ATT_TILE
    h = pl.program_id(0)
    kk = lax.broadcasted_iota(jnp.int32, (t, t), 0)
    qq = lax.broadcasted_iota(jnp.int32, (t, t), 1)
    for kind in range(3):
        dist = jnp.full((t, t), 2 * t, jnp.int32) if kind == 2 else kind * t + qq - kk
        bucket = _t5_bucket(dist)
        tile = jnp.zeros((t, t), F32)
        for b in range(N_BUCKETS):
            tile = jnp.where(bucket == b, rb_ref[b, h], tile)
        o_ref[0, kind] = jnp.where(dist >= 0, tile * LOG2E, NEG)


def _bias_tiles(rel_bias):
    t = ATT_TILE
    heads = rel_bias.shape[1]
    return pl.pallas_call(
        _bias_kernel,
        grid=(heads,),
        in_specs=[pl.BlockSpec(memory_space=pltpu.SMEM)],
        out_specs=pl.BlockSpec((1, 3, t, t), lambda h: (h, 0, 0, 0)),
        out_shape=jax.ShapeDtypeStruct((heads, 3, t, t), F32),
        compiler_params=pltpu.CompilerParams(dimension_semantics=("arbitrary",)),
        name="bias_tiles",
    )(rel_bias.astype(F32))


def _norm_modulate(x, g, shift, scale):
    ms = jnp.mean(x * x, axis=-1, keepdims=True)
    y = x * lax.rsqrt(ms + EPS) * g
    return y * (1.0 + scale) + shift


def _inproj_kernel(x_ref, mod_ref, g_ref, w_ref, o_ref, *, tn):
    u = _norm_modulate(x_ref[...], g_ref[...], mod_ref[0, 0:1, :], mod_ref[0, 1:2, :])
    u = u.astype(BF16)
    for j in range(w_ref.shape[1] // tn):
        cols = slice(j * tn, (j + 1) * tn)
        o_ref[:, cols] = _nn(u, w_ref[:, cols]).astype(o_ref.dtype)


def _inproj(x2, mod_l, g, w_packed, layer, seq):
    m, d = x2.shape
    n = w_packed.shape[2]
    tm = min(512, seq)
    return pl.pallas_call(
        functools.partial(_inproj_kernel, tn=n // 3),
        grid=(m // tm,),
        in_specs=[
            pl.BlockSpec((tm, d), lambda i: (i, 0)),
            pl.BlockSpec((1, 6, d), lambda i: ((i * tm) // seq, 0, 0)),
            pl.BlockSpec((1, d), lambda i: (0, 0)),
            pl.BlockSpec((None, d, n), lambda i: (layer, 0, 0), pipeline_mode=pl.Buffered(1)),
        ],
        out_specs=pl.BlockSpec((tm, n), lambda i: (i, 0)),
        out_shape=jax.ShapeDtypeStruct((m, n), BF16),
        compiler_params=pltpu.CompilerParams(
            dimension_semantics=("parallel",), vmem_limit_bytes=VMEM_LIMIT),
        name="in_proj",
    )(x2, mod_l, g.reshape(1, d), w_packed)


def _attend(n_live, lives, chains, tile_ctx, logits, values, s_ref, m_ref, a_ref, acc_ref,
            *, unroll, before=None, after=None):
    def init():
        m_ref[...] = jnp.full(m_ref.shape, NEG, F32)
        acc_ref[...] = jnp.zeros(acc_ref.shape, F32)

    def step(j, slot, kind, start=True, finish=True):
        ctx = tile_ctx(j) if start else None
        for c in range(chains):
            if start:
                s = logits(j, c, ctx, kind)
                m_old = m_ref[1 - slot, c]
                m_new = jnp.maximum(m_old, jnp.max(s, axis=0, keepdims=True))
                s_ref[slot, c] = s
                a_ref[slot, c] = jnp.exp2(m_old - m_new)
                m_ref[slot, c] = m_new
            if finish:
                p = jnp.exp2(s_ref[1 - slot, c] - m_ref[1 - slot, c])
                acc_ref[c] = (a_ref[1 - slot, c] * acc_ref[c]
                              + _nn(values(j - 1, c), p.astype(BF16)))

    if unroll:
        def variant(live):
            if before is not None:
                before()
            init()
            for j in range(live + 1):
                step(j, j % 2, min(live - 1 - j, 2), start=j < live, finish=j > 0)
            if after is not None:
                after()

        for live in lives:
            pl.when(n_live == live)(functools.partial(variant, live))
        return

    if before is not None:
        before()
    init()
    step(0, 0, jnp.minimum(n_live - 1, 2), finish=False)

    def body(j, carry):
        for slot in range(2):
            pl.when(j % 2 == slot)(
                functools.partial(step, j, slot, jnp.minimum(n_live - 1 - j, 2)))
        return carry

    lax.fori_loop(1, n_live, body, 0)
    for slot in range(2):
        pl.when(n_live % 2 == slot)(functools.partial(step, n_live, slot, None, start=False))
    if after is not None:
        after()


def _attend_scratch(chains, dv):
    t = ATT_TILE
    return [
        pltpu.VMEM((2, chains, t, t), F32),
        pltpu.VMEM((2, chains, 1, t), F32),
        pltpu.VMEM((2, chains, 1, t), F32),
        pltpu.VMEM((chains, dv + SUM_ROWS, t), F32),
    ]


def _attend_result(acc_ref, c, dv):
    return acc_ref[c, 0:dv, :] / acc_ref[c, dv:dv + 1, :]


def _key_tile(ref, j, cols=None):
    t = ATT_TILE
    rows = slice(j * t, (j + 1) * t) if isinstance(j, int) else pl.ds(pl.multiple_of(j * t, t), t)
    return ref[0, rows, :] if cols is None else ref[0, rows, cols]


def _transpose_tiles(src_ref, dst_ref, n_tiles, groups):
    t = ATT_TILE
    eye = _eye(LANES)
    for g in range(groups):
        for j in range(n_tiles):
            blk = src_ref[0, j * t:(j + 1) * t, g * LANES:(g + 1) * LANES]
            dst_ref[g, j, 0:LANES, :] = _nt(eye, blk).astype(BF16)
            dst_ref[g, j, LANES:LANES + SUM_ROWS, :] = jnp.ones((SUM_ROWS, t), BF16)


def _diff_kernel(q_ref, k_ref, v_ref, bias_ref, dl_ref, g_ref, o_ref,
                 vt_ref, qs_ref, s_ref, m_ref, a_ref, acc_ref, *, lam_init, n_tiles, hp):
    t = ATT_TILE
    i = pl.program_id(2)

    @pl.when(i == 0)
    def _():
        _transpose_tiles(v_ref, vt_ref, n_tiles, hp)

    def prepare_queries():
        scale = HEAD_DIM ** -0.5 * LOG2E
        eye_l = _eye(LANES)
        for h in range(hp):
            q = q_ref[0, :, h * LANES:(h + 1) * LANES]
            for half in range(2):
                qs_ref[2 * h + half] = _transposed(
                    _lane_group_mask(q, half, HEAD_DIM) * scale, eye_l)

    def logits(j, c, ctx, kind):
        h = c // 2
        kblk = _key_tile(k_ref, j, slice(h * LANES, (h + 1) * LANES))
        return _nn(kblk, qs_ref[c]) + bias_ref[h, kind]

    def write_output():
        dl = dl_ref[...]
        lam = (jnp.exp(jnp.sum(dl[0:1] * dl[1:2], keepdims=True))
               - jnp.exp(jnp.sum(dl[2:3] * dl[3:4], keepdims=True)) + lam_init)
        eye = _eye(t)
        for h in range(hp):
            o = (_attend_result(acc_ref, 2 * h, DIFF_V_DIM)
                 - lam * _attend_result(acc_ref, 2 * h + 1, DIFF_V_DIM))
            ms = jnp.mean(o * o, axis=0, keepdims=True)
            o = o * lax.rsqrt(ms + EPS) * g_ref[...] * (1.0 - lam_init)
            o_ref[0, :, h * LANES:(h + 1) * LANES] = _nt(eye, o.astype(BF16)).astype(o_ref.dtype)

    _attend(i + 1, range(1, n_tiles + 1), 2 * hp, lambda j: None, logits,
            lambda j, c: vt_ref[c // 2, j], s_ref, m_ref, a_ref, acc_ref,
            unroll=True, before=prepare_queries, after=write_output)


def _diff_attention(p3, bias_tiles, diff_lambda_l, subln_g, lam_init):
    bsz, seq, _ = p3.shape
    t = ATT_TILE
    n_tiles = seq // t
    hp = DIFF_HEADS_PER_STEP
    w = hp * LANES
    kern = functools.partial(_diff_kernel, lam_init=lam_init, n_tiles=n_tiles, hp=hp)
    return pl.pallas_call(
        kern,
        grid=(bsz, DIFF_HEADS // hp, n_tiles),
        in_specs=[
            pl.BlockSpec((1, t, w), lambda b, h, i: (b, i, BLK_AQ // hp + h)),
            pl.BlockSpec((1, seq, w), lambda b, h, i: (b, 0, BLK_AK // hp + h)),
            pl.BlockSpec((1, seq, w), lambda b, h, i: (b, 0, BLK_AV // hp + h)),
            pl.BlockSpec((hp, 3, t, t), lambda b, h, i: (h, 0, 0, 0)),
            pl.BlockSpec((4, HEAD_DIM), lambda b, h, i: (0, 0)),
            pl.BlockSpec((DIFF_V_DIM, 1), lambda b, h, i: (0, 0)),
        ],
        out_specs=pl.BlockSpec((1, t, w), lambda b, h, i: (b, i, h)),
        out_shape=jax.ShapeDtypeStruct((bsz, seq, DIFF_HEADS * DIFF_V_DIM), BF16),
        scratch_shapes=[
            pltpu.VMEM((hp, n_tiles, LANES + SUM_ROWS, t), BF16),
            pltpu.VMEM((2 * hp, LANES, t), BF16),
        ] + _attend_scratch(2 * hp, DIFF_V_DIM),
        compiler_params=pltpu.CompilerParams(
            dimension_semantics=("parallel", "parallel", "arbitrary"),
            vmem_limit_bytes=VMEM_LIMIT),
        name="diff_attn",
    )(p3, p3, p3, bias_tiles, diff_lambda_l, subln_g.reshape(DIFF_V_DIM, 1))


def _dsa_kernel(q_ref, kv_ref, iq_ref, ik_ref, iw_ref, bias_ref, g_ref, wuv_ref, o_ref,
                kvn_ref, kvt_ref, keys_ref, byte_ref, cand_ref, thr_ref, need_ref, tied_ref,
                iqh_ref, qt_ref, iwt_ref,
                s_ref, m_ref, a_ref, acc_ref, *, n_tiles, first_tile, last_tile, topk):
    t = ATT_TILE
    i = pl.program_id(1) + first_tile
    n_live = i + 1
    lives = range(first_tile + 1, last_tile + 2)

    @pl.when(pl.program_id(1) == 0)
    def _():
        kv = kv_ref[0].astype(F32)
        ms = jnp.mean(kv * kv, axis=-1, keepdims=True)
        kvn_ref[0] = (kv * lax.rsqrt(ms + EPS) * g_ref[...]).astype(BF16)
        _transpose_tiles(kvn_ref, kvt_ref, n_tiles, 1)

    def prepare_queries():
        groups = LANES // IDX_DIM
        eye_l = _eye(LANES)
        for h in range(IDX_HEADS):
            blk = iq_ref[0, :, (h // groups) * LANES:(h // groups + 1) * LANES]
            iqh_ref[h] = _transposed(_lane_group_mask(blk, h % groups, IDX_DIM), eye_l)
        for h in range(DSA_HEADS):
            q_h = q_ref[0, :, h * LANES:(h + 1) * LANES].astype(F32)
            qt_ref[h] = _transposed(q_h * (DSA_LATENT ** -0.5 * LOG2E), eye_l)
        sel_rows = lax.broadcasted_iota(jnp.int32, (IDX_HEADS, LANES), 0)
        sel_cols = lax.broadcasted_iota(jnp.int32, (IDX_HEADS, LANES), 1)
        pick = jnp.where(sel_rows == sel_cols, 1.0, 0.0).astype(BF16)
        iwt_ref[...] = _nt(pick, iw_ref[0])

    idx_scale = (IDX_HEADS ** -0.5) * (IDX_DIM ** -0.5)
    row = lax.broadcasted_iota(jnp.int32, (t, t), 0)
    col = lax.broadcasted_iota(jnp.int32, (t, t), 1)

    def score_tile(j, diagonal):
        ik = _key_tile(ik_ref, j)
        sc = jnp.zeros((t, t), F32)
        for h in range(IDX_HEADS):
            sc = sc + jnp.maximum(_nn(ik, iqh_ref[h]), 0.0) * iwt_ref[h:h + 1, :]
        sc = sc * idx_scale
        bits = pltpu.bitcast(sc, jnp.int32)
        key = bits ^ ((bits >> 31) & 0x7FFFFFFF)
        top = (key >> 24) + 128
        if diagonal:
            valid = col >= row
            key = jnp.where(valid, key, INT_MIN)
            top = jnp.where(valid, top, -1)
        keys_ref[j] = key
        cand_ref[j] = top.astype(F32).astype(BF16)
        for b in range(3):
            byte_ref[b, j] = ((key >> (16 - 8 * b)) & 0xFF).astype(F32).astype(BF16)

    one_b, zero_b = jnp.ones((), BF16), jnp.zeros((), BF16)
    packed_rows = 16

    def radix_select(live):
        prepare_queries()
        for j in range(live):
            score_tile(j, diagonal=(j == live - 1))

        def count(hit_fn):
            part = None
            for j in range(live):
                hit = jnp.where(hit_fn(cand_ref[j]), one_b, zero_b)
                for r in range(t // packed_rows):
                    rows = hit[packed_rows * r:packed_rows * (r + 1)]
                    part = rows if part is None else part + rows
            return jnp.sum(part.astype(F32), axis=0, keepdims=True)

        if live * t <= topk:
            thr_ref[...] = jnp.full((1, t), INT_MIN, jnp.int32)
            need_ref[...] = jnp.zeros((1, t), F32)
            tied_ref[...] = jnp.zeros((1, t), F32)
            return

        need = jnp.full((1, t), topk, F32)
        thr = jnp.zeros((1, t), jnp.int32)
        for b in range(4):
            def bit_body(it, carry, need=need):
                val, n_above = carry
                cand = val | lax.shift_left(jnp.int32(1), 7 - it)
                cand_b = cand.astype(F32).astype(BF16)
                cnt = count(lambda x: x >= cand_b)
                ok = cnt >= need
                return jnp.where(ok, cand, val), jnp.where(ok, n_above, cnt)

            val, n_above = lax.fori_loop(
                0, 8, bit_body, (jnp.zeros((1, t), jnp.int32), jnp.zeros((1, t), F32)))
            val_b = val.astype(F32).astype(BF16)
            need = need - n_above
            if b < 3:
                for j in range(live):
                    cand_ref[j] = jnp.where(cand_ref[j] == val_b, byte_ref[b, j], -one_b)
            else:
                tied_ref[...] = count(lambda x: x == val_b)
            piece = val - (128 if b == 0 else 0)
            thr = thr | lax.shift_left(piece, 24 - 8 * b)
        thr_ref[...] = thr
        need_ref[...] = need

    assert (t // packed_rows) * n_tiles <= 256
    for live in lives:
        pl.when(n_live == live)(functools.partial(radix_select, live))

    thr = thr_ref[...]
    need = need_ref[...]
    has_rank = thr > INT_MIN
    excess = jnp.where(has_rank & (tied_ref[...] > need), 1.0, 0.0)
    thr_ref[...] = jnp.where(has_rank, thr, jnp.int32(INT_MIN + 1))

    @pl.when(jnp.max(excess) > 0.0)
    def _():
        earlier = jnp.where(col < row, 1.0, 0.0).astype(BF16)

        def tie_body(j, seen):
            key = keys_ref[j]
            tie = jnp.where(key == thr, 1.0, 0.0)
            before = seen + _nn(earlier, tie.astype(BF16))
            late = (key == thr) & has_rank & (before >= need)
            keys_ref[j] = jnp.where(late, thr - 1, key)
            return seen + jnp.sum(tie, axis=0, keepdims=True)

        lax.fori_loop(0, n_live, tie_body, jnp.zeros((1, t), F32))

    def tile_ctx(j):
        return keys_ref[j] >= thr_ref[...], _key_tile(kvn_ref, j)

    def logits(j, h, ctx, kind):
        sel, kvb = ctx
        return jnp.where(sel, _nn(kvb, qt_ref[h]) + bias_ref[h, kind], NEG)

    def write_output():
        y_t = jnp.zeros((DSA_HEADS * DSA_V_DIM, t), F32)
        for h in range(DSA_HEADS):
            o_h = _attend_result(acc_ref, h, DSA_LATENT).astype(BF16)
            y_t = y_t + _nn(wuv_ref[h], o_h)
        o_ref[0] = _nt(_eye(t), y_t.astype(BF16)).astype(o_ref.dtype)

    _attend(n_live, lives, DSA_HEADS, tile_ctx, logits, lambda j, h: kvt_ref[0, j],
            s_ref, m_ref, a_ref, acc_ref, unroll=True, after=write_output)


def _dsa_attention(p3, bias_tiles, kv_norm_g, wuv_t_pad):
    n_tiles = p3.shape[1] // ATT_TILE
    cuts = [0, n_tiles // 2, (3 * n_tiles) // 4, n_tiles]
    parts = [_dsa_attention_part(p3, bias_tiles, kv_norm_g, wuv_t_pad, lo, hi - 1)
             for lo, hi in zip(cuts[:-1], cuts[1:]) if hi > lo]
    return jnp.concatenate(parts, axis=1)


def _dsa_attention_part(p3, bias_tiles, kv_norm_g, wuv_t_pad, first_tile, last_tile):
    bsz, seq, _ = p3.shape
    t = ATT_TILE
    n_tiles = seq // t
    n_q = last_tile - first_tile + 1
    topk = min(DSA_TOPK_MAX, seq // 4)
    kern = functools.partial(_dsa_kernel, n_tiles=n_tiles, first_tile=first_tile,
                             last_tile=last_tile, topk=float(topk))
    n_out = DSA_HEADS * DSA_V_DIM
    return pl.pallas_call(
        kern,
        grid=(bsz, n_q),
        in_specs=[
            pl.BlockSpec((1, t, 4 * LANES), lambda b, i: (b, i + first_tile, BLK_BQ // 4)),
            pl.BlockSpec((1, seq, LANES), lambda b, i: (b, 0, BLK_BKV)),
            pl.BlockSpec((1, t, 2 * LANES), lambda b, i: (b, i + first_tile, BLK_BIQ // 2)),
            pl.BlockSpec((1, seq, LANES), lambda b, i: (b, 0, BLK_IK)),
            pl.BlockSpec((1, t, LANES), lambda b, i: (b, i + first_tile, BLK_IW)),
            pl.BlockSpec((DSA_HEADS, 3, t, t), lambda b, i: (DIFF_HEADS // DSA_HEADS, 0, 0, 0)),
            pl.BlockSpec((1, DSA_LATENT), lambda b, i: (0, 0)),
            pl.BlockSpec((DSA_HEADS, n_out, DSA_LATENT), lambda b, i: (0, 0, 0)),
        ],
        out_specs=pl.BlockSpec((1, t, n_out), lambda b, i: (b, i, 0)),
        out_shape=jax.ShapeDtypeStruct((bsz, n_q * t, n_out), BF16),
        scratch_shapes=[
            pltpu.VMEM((1, seq, DSA_LATENT), BF16),
            pltpu.VMEM((1, n_tiles, DSA_LATENT + SUM_ROWS, t), BF16),
            pltpu.VMEM((n_tiles, t, t), jnp.int32),
            pltpu.VMEM((3, n_tiles, t, t), BF16),
            pltpu.VMEM((n_tiles, t, t), BF16),
            pltpu.VMEM((1, t), jnp.int32),
            pltpu.VMEM((1, t), F32),
            pltpu.VMEM((1, t), F32),
            pltpu.VMEM((IDX_HEADS, LANES, t), BF16),
            pltpu.VMEM((DSA_HEADS, LANES, t), BF16),
            pltpu.VMEM((IDX_HEADS, t), F32),
        ] + _attend_scratch(DSA_HEADS, DSA_LATENT),
        compiler_params=pltpu.CompilerParams(
            dimension_semantics=("parallel", "arbitrary"), vmem_limit_bytes=VMEM_LIMIT),
        name="dsa_attn",
    )(p3, p3, p3, p3, p3, bias_tiles, kv_norm_g.reshape(1, DSA_LATENT), wuv_t_pad)


def _moba_kernel(q_ref, k_ref, v_ref, bias_ref, o_ref,
                 vt_ref, kmean_ref, selb_ref, qs_ref,
                 s_ref, m_ref, a_ref, acc_ref, *, n_tiles, topb):
    t = ATT_TILE
    i = pl.program_id(1)
    pairs = MOBA_HEADS // 2

    @pl.when(i == 0)
    def _():
        _transpose_tiles(v_ref, vt_ref, n_tiles, pairs)
        for n in range(n_tiles):
            kb = k_ref[0, n * t:(n + 1) * t, :].astype(F32)
            kmean_ref[n:n + 1, :] = jnp.mean(kb, axis=0, keepdims=True)

    def prepare_queries_and_gates():
        scale = HEAD_DIM ** -0.5
        eye_l = _eye(LANES)
        blk = lax.broadcasted_iota(jnp.int32, (n_tiles, t), 0)
        own = jnp.full((1, t), i, jnp.int32)
        for hd in range(MOBA_HEADS):
            g, half = hd // 2, hd % 2
            q_f = _lane_group_mask(q_ref[0, :, g * LANES:(g + 1) * LANES], half, HEAD_DIM)
            qs_ref[hd] = _transposed(q_f * (scale * LOG2E), eye_l)
            kmean = kmean_ref[:, g * LANES:(g + 1) * LANES].astype(BF16)
            gate = _nt(kmean, q_f.astype(BF16))
            for n in range(n_tiles):
                gn = gate[n:n + 1, :]
                ahead = (gate > gn) | ((gate == gn) & (blk < n))
                ahead = ahead & (blk < i)
                rank = jnp.sum(jnp.where(ahead, 1.0, 0.0), axis=0, keepdims=True)
                chosen = ((rank < topb) & (own > n)) | (own == n)
                selb_ref[hd, n] = jnp.broadcast_to(jnp.where(chosen, 0.0, NEG), (8, t))

    def logits(j, hd, ctx, kind):
        g = hd // 2
        kblk = _key_tile(k_ref, j, slice(g * LANES, (g + 1) * LANES))
        return _nn(kblk, qs_ref[hd]) + bias_ref[hd, kind] + selb_ref[hd, j][0:1, :]

    def write_output():
        rows = lax.broadcasted_iota(jnp.int32, (LANES, t), 0)
        eye = _eye(t)
        for g in range(pairs):
            o_lo = _attend_result(acc_ref, 2 * g, LANES)
            o_hi = _attend_result(acc_ref, 2 * g + 1, LANES)
            o = jnp.where(rows < HEAD_DIM, o_lo, o_hi)
            o_ref[0, :, g * LANES:(g + 1) * LANES] = _nt(eye, o.astype(BF16)).astype(o_ref.dtype)

    _attend(i + 1, range(1, n_tiles + 1), MOBA_HEADS, lambda j: None, logits,
            lambda j, hd: vt_ref[hd // 2, j], s_ref, m_ref, a_ref, acc_ref,
            unroll=True, before=prepare_queries_and_gates, after=write_output)


def _moba_attention(p3, bias_tiles):
    bsz, seq, _ = p3.shape
    t = ATT_TILE
    n_tiles = seq // t
    topb = min(MOBA_TOPK_MAX, n_tiles)
    kern = functools.partial(_moba_kernel, n_tiles=n_tiles, topb=float(topb))
    w = MOBA_HEADS * HEAD_DIM
    blocks = w // LANES
    first = (DIFF_HEADS + DSA_HEADS) // MOBA_HEADS
    return pl.pallas_call(
        kern,
        grid=(bsz, n_tiles),
        in_specs=[
            pl.BlockSpec((1, t, w), lambda b, i: (b, i, BLK_CQ // blocks)),
            pl.BlockSpec((1, seq, w), lambda b, i: (b, 0, BLK_CK // blocks)),
            pl.BlockSpec((1, seq, w), lambda b, i: (b, 0, BLK_CV // blocks)),
            pl.BlockSpec((MOBA_HEADS, 3, t, t), lambda b, i: (first, 0, 0, 0)),
        ],
        out_specs=pl.BlockSpec((1, t, w), lambda b, i: (b, i, 0)),
        out_shape=jax.ShapeDtypeStruct((bsz, seq, w), BF16),
        scratch_shapes=[
            pltpu.VMEM((blocks, n_tiles, LANES + SUM_ROWS, t), BF16),
            pltpu.VMEM((n_tiles, w), F32),
            pltpu.VMEM((MOBA_HEADS, n_tiles, 8, t), F32),
            pltpu.VMEM((MOBA_HEADS, LANES, t), BF16),
        ] + _attend_scratch(MOBA_HEADS, LANES),
        compiler_params=pltpu.CompilerParams(
            dimension_semantics=("parallel", "arbitrary"), vmem_limit_bytes=VMEM_LIMIT),
        name="moba_attn",
    )(p3, p3, p3, bias_tiles)


def _merge_kernel(x_ref, g_ref, gb_ref, oa_ref, ob_ref, oc_ref, wa_ref, wb_ref, wc_ref, wo_ref,
                  mod_ref, o_ref):
    d = D_MODEL
    gates = jax.nn.sigmoid(g_ref[...].astype(F32) + gb_ref[...])
    merged = (gates[:, 0:d] * _nn(oa_ref[...], wa_ref[...])
              + gates[:, d:2 * d] * _nn(ob_ref[...], wb_ref[...])
              + gates[:, 2 * d:3 * d] * _nn(oc_ref[...], wc_ref[...]))
    z = _nn(merged.astype(BF16), wo_ref[...])
    o_ref[...] = x_ref[...] + mod_ref[0, 2:3, :] * z


def _merge(x2, p2, gate_b, oa, ob, oc, wa, wb, wc, wo, layer, mod_l, seq):
    m, d = x2.shape
    tm = min(512, seq)
    full = lambda a: pl.BlockSpec((None,) + a.shape[1:], lambda i: (layer, 0, 0))
    return pl.pallas_call(
        _merge_kernel,
        grid=(m // tm,),
        in_specs=[
            pl.BlockSpec((tm, d), lambda i: (i, 0)),
            pl.BlockSpec((tm, 3 * d), lambda i: (i, BLK_G)),
            pl.BlockSpec((1, 3 * d), lambda i: (0, 0)),
            pl.BlockSpec((tm, oa.shape[1]), lambda i: (i, 0)),
            pl.BlockSpec((tm, ob.shape[1]), lambda i: (i, 0)),
            pl.BlockSpec((tm, oc.shape[1]), lambda i: (i, 0)),
            full(wa), full(wb), full(wc), full(wo),
            pl.BlockSpec((1, 6, d), lambda i: ((i * tm) // seq, 0, 0)),
        ],
        out_specs=pl.BlockSpec((tm, d), lambda i: (i, 0)),
        out_shape=jax.ShapeDtypeStruct((m, d), F32),
        compiler_params=pltpu.CompilerParams(
            dimension_semantics=("parallel",), vmem_limit_bytes=VMEM_LIMIT),
        name="merge",
    )(x2, p2, gate_b.reshape(1, 3 * d), oa, ob, oc, wa, wb, wc, wo, mod_l)


def _mlp_kernel(x_ref, mod_ref, g_ref, w1_ref, w2_ref, gf_ref, o_ref, *, final, tf):
    x = x_ref[...]
    u = _norm_modulate(x, g_ref[...], mod_ref[0, 3:4, :], mod_ref[0, 4:5, :]).astype(BF16)
    acc = None
    for f in range(w1_ref.shape[1] // tf):
        h = jnp.square(jnp.maximum(_nn(u, w1_ref[:, f * tf:(f + 1) * tf]), 0.0))
        part = _nn(h.astype(BF16), w2_ref[f * tf:(f + 1) * tf, :])
        acc = part if acc is None else acc + part
    y = x + mod_ref[0, 5:6, :] * acc
    if final:
        ms = jnp.mean(y * y, axis=-1, keepdims=True)
        y = y * lax.rsqrt(ms + EPS) * gf_ref[...]
    o_ref[...] = y


def _mlp(x2, mod_l, g, w1, w2, layer, g_final, seq, final):
    m, d = x2.shape
    dff = w1.shape[2]
    tm = min(512, seq)
    resident = lambda shape: pl.BlockSpec((None,) + shape, lambda i: (layer, 0, 0),
                                          pipeline_mode=pl.Buffered(1))
    return pl.pallas_call(
        functools.partial(_mlp_kernel, final=final, tf=1024),
        grid=(m // tm,),
        in_specs=[
            pl.BlockSpec((tm, d), lambda i: (i, 0)),
            pl.BlockSpec((1, 6, d), lambda i: ((i * tm) // seq, 0, 0)),
            pl.BlockSpec((1, d), lambda i: (0, 0)),
            resident((d, dff)),
            resident((dff, d)),
            pl.BlockSpec((1, d), lambda i: (0, 0)),
        ],
        out_specs=pl.BlockSpec((tm, d), lambda i: (i, 0)),
        out_shape=jax.ShapeDtypeStruct((m, d), F32),
        compiler_params=pltpu.CompilerParams(
            dimension_semantics=("parallel",), vmem_limit_bytes=VMEM_LIMIT),
        name="mlp",
    )(x2, mod_l, g.reshape(1, d), w1, w2, g_final.reshape(1, d))


def kernel(x, c, rel_bias, ada_w, ada_b, norm_mix, w_in, gate_b, diff_lambda, diff_subln,
           dsa_kv_norm, dsa_w_uv, w_br_a, w_br_b, w_br_c, w_o, norm_mlp, w_ff1, w_ff2,
           norm_final):
    bsz, seq, d = x.shape
    depth = w_in.shape[0]
    assert seq % ATT_TILE == 0 and 2 * ATT_TILE > MAX_DISTANCE

    w_in_p = _pack_columns(w_in.astype(BF16))
    n_out = DSA_HEADS * DSA_V_DIM
    wuv_t = jnp.transpose(dsa_w_uv, (0, 1, 3, 2))
    wuv_t_pad = jnp.zeros((depth, DSA_HEADS, n_out, DSA_LATENT), F32)
    for h in range(DSA_HEADS):
        wuv_t_pad = wuv_t_pad.at[:, h, h * DSA_V_DIM:(h + 1) * DSA_V_DIM, :].set(wuv_t[:, h])
    wuv_t_pad = wuv_t_pad.astype(BF16)
    wa, wb, wc, wo = (w.astype(BF16) for w in (w_br_a, w_br_b, w_br_c, w_o))
    w1, w2 = w_ff1.astype(BF16), w_ff2.astype(BF16)

    bias_tiles = _bias_tiles(rel_bias)
    mod = _ada(c, ada_w, ada_b).reshape(depth, bsz, 6, d)

    x2 = x.reshape(bsz * seq, d)
    for l in range(depth):
        lam_init = 0.8 - 0.6 * math.exp(-0.3 * l)
        p2 = _inproj(x2, mod[l], norm_mix[l], w_in_p, l, seq)
        p3 = p2.reshape(bsz, seq, PACKED_COLS)
        oa = _diff_attention(p3, bias_tiles, diff_lambda[l], diff_subln[l], lam_init)
        ob = _dsa_attention(p3, bias_tiles, dsa_kv_norm[l], wuv_t_pad[l])
        oc = _moba_attention(p3, bias_tiles)
        x2 = _merge(x2, p2, gate_b[l],
                    oa.reshape(bsz * seq, -1), ob.reshape(bsz * seq, -1), oc.reshape(bsz * seq, -1),
                    wa, wb, wc, wo, l, mod[l], seq)
        x2 = _mlp(x2, mod[l], norm_mlp[l], w1, w2, l, norm_final, seq, final=(l == depth - 1))
    return x2.reshape(bsz, seq, d)
```
